```python
import jax, jax.numpy as jnp
from jax import lax
import numpy as np

D_MODEL = 1024
BATCH = 2
SEQ = 8192
DEPTH = 1
DEC_BATCH = 128
DEC_SEQ = 8
PAST_LEN = 16384
PAGE_SIZE = 128

LRU_WIDTH = D_MODEL
LRU_BLOCKS = 16
LRU_BLOCK_DIM = LRU_WIDTH // LRU_BLOCKS
LRU_C = 8.0
CONV_W = 4
N_HEADS = 16
N_KV_HEADS = 4
GQA = N_HEADS // N_KV_HEADS
HEAD_DIM = D_MODEL // N_HEADS
WINDOW = 128
ATTN_BLOCK = 128
N_GROUPS = 4
EXPERTS_PER_GROUP = 8
TOP_K_IN_GROUP = 2
D_EXPERT = D_MODEL // 4
RMS_EPS = 1e-6
NEG_INF = -1e30
IN_COLS = LRU_WIDTH + (N_HEADS + 2 * N_KV_HEADS) * HEAD_DIM + 2 * D_MODEL

kernel_name = 'hawk_swa_sink_hiermoe_step'


def _rmsnorm(x, g):
    xf = x.astype(jnp.float32)
    y = xf * lax.rsqrt(jnp.mean(xf * xf, axis=-1, keepdims=True) + RMS_EPS)
    return (y * g.astype(jnp.float32)).astype(x.dtype)


def _causal_conv(xl, buf, w, b):
    t = xl.shape[1]
    xp = jnp.concatenate([buf.astype(xl.dtype), xl], axis=1)
    y = b + sum(xp[:, j:j + t] * w[j] for j in range(CONV_W))
    return y, xp[:, -(CONV_W - 1):]


def _lin_combine(left, right):
    a_l, b_l = left
    a_r, b_r = right
    return a_l * a_r, a_r * b_l + b_r


def _rg_lru(x, h0, wa, ba, wx, bx, lam):
    bsz, t, w = x.shape
    xb = x.reshape(bsz, t, LRU_BLOCKS, LRU_BLOCK_DIM)
    r = jax.nn.sigmoid(jnp.einsum('bthi,hij->bthj', xb, wa).reshape(bsz, t, w) + ba)
    ig = jax.nn.sigmoid(jnp.einsum('bthi,hij->bthj', xb, wx).reshape(bsz, t, w) + bx)
    log_a = -LRU_C * r.astype(jnp.float32) * jax.nn.softplus(-lam.astype(jnp.float32))
    a = jnp.exp(log_a)
    bvals = jnp.sqrt(-jnp.expm1(2.0 * log_a)) * (ig * x).astype(jnp.float32)
    bvals = bvals.at[:, 0].add(a[:, 0] * h0.astype(jnp.float32))
    _, h = lax.associative_scan(_lin_combine, (a, bvals), axis=1)
    return h.astype(x.dtype), h[:, -1].astype(h0.dtype)


def _sink_attend(q, k, v, dist, valid, slopes, sinks):
    s = jnp.einsum('...tkgd,...skd->...kgts', q, k).astype(jnp.float32) * (HEAD_DIM ** -0.5)
    s = s - slopes * dist.astype(jnp.float32)
    s = jnp.where(valid, s, NEG_INF)
    m = jnp.maximum(jnp.max(s, axis=-1, keepdims=True), sinks)
    p = jnp.exp(s - m)
    denom = jnp.sum(p, axis=-1, keepdims=True) + jnp.exp(sinks - m)
    p = (p / denom).astype(v.dtype)
    return jnp.einsum('...kgts,...skd->...tkgd', p, v)


def _attn_consts(sinks):
    slopes = jnp.exp2(-8.0 * jnp.arange(1, N_HEADS + 1, dtype=jnp.float32) / N_HEADS)
    return slopes.reshape(N_KV_HEADS, GQA, 1, 1), sinks.astype(jnp.float32).reshape(N_KV_HEADS, GQA, 1, 1)


def _window_attn_prompt(q, k, v, sinks):
    bsz, t = q.shape[:2]
    nb = t // ATTN_BLOCK
    qb = q.reshape(bsz, nb, ATTN_BLOCK, N_KV_HEADS, GQA, HEAD_DIM)

    def band(z):
        cur = z.reshape(bsz, nb, ATTN_BLOCK, N_KV_HEADS, HEAD_DIM)
        prev = jnp.concatenate([jnp.zeros_like(cur[:, :1]), cur[:, :-1]], axis=1)
        return jnp.concatenate([prev, cur], axis=2)

    kb, vb = band(k), band(v)
    qi = jnp.arange(ATTN_BLOCK)[:, None]
    kj = jnp.arange(2 * ATTN_BLOCK)[None, :]
    dist = ATTN_BLOCK + qi - kj
    first = (jnp.arange(nb) == 0)[:, None, None]
    valid = (dist >= 0) & (dist < WINDOW) & ~(first & (kj < ATTN_BLOCK))[None] if False else (dist >= 0) & (dist < WINDOW) & ~(first & (kj < ATTN_BLOCK))
    slopes, snk = _attn_consts(sinks)
    o = _sink_attend(qb, kb, vb, dist, valid[:, None, None], slopes, snk)
    return o.reshape(bsz, t, N_HEADS * HEAD_DIM)


def _window_attn_sample(q, k, v, k_buf, v_buf, sinks):
    bsz, t = q.shape[:2]
    kx = jnp.concatenate([k_buf.astype(k.dtype), k], axis=1)
    vx = jnp.concatenate([v_buf.astype(v.dtype), v], axis=1)
    k_pos = jnp.arange(WINDOW + t) - WINDOW
    q_pos = jnp.arange(t)
    dist = q_pos[:, None] - k_pos[None, :]
    valid = (dist >= 0) & (dist < WINDOW)
    slopes, snk = _attn_consts(sinks)
    o = _sink_attend(q, kx, vx, dist, valid, slopes, snk)
    return o.reshape(bsz, t, N_HEADS * HEAD_DIM), kx[:, -WINDOW:], vx[:, -WINDOW:]


def _moe(x, w_rg, w_re, w_gate, w_up, w_down):
    bsz, t, d = x.shape
    xt = x.reshape(bsz * t, d)
    pg = jax.nn.softmax((xt @ w_rg).astype(jnp.float32), axis=-1)
    p_grp, g_idx = lax.top_k(pg, 1)
    el = jnp.einsum('nd,gde->nge', xt, w_re).astype(jnp.float32)
    el_sel = jnp.take_along_axis(el, g_idx[:, :, None], axis=1)[:, 0]
    pe = jax.nn.softmax(el_sel, axis=-1)
    top_v, top_i = lax.top_k(pe, TOP_K_IN_GROUP)
    top_v = top_v / jnp.sum(top_v, axis=-1, keepdims=True)
    w_exp = jnp.sum(jax.nn.one_hot(top_i, EXPERTS_PER_GROUP, dtype=jnp.float32) * top_v[..., None], axis=1)
    w_grp = jax.nn.one_hot(g_idx[:, 0], N_GROUPS, dtype=jnp.float32) * p_grp
    combine = (w_grp[:, :, None] * w_exp[:, None, :]).astype(x.dtype)
    out = jnp.zeros_like(xt)
    for g in range(N_GROUPS):
        hg = jax.nn.silu(jnp.einsum('nd,edf->nef', xt, w_gate[g])) * jnp.einsum('nd,edf->nef', xt, w_up[g])
        out = out + jnp.einsum('nef,efd->nd', hg * combine[:, g, :, None], w_down[g])
    return out.reshape(bsz, t, d)


def _layer(x, conv_buf, h0, k_buf, v_buf, norm_mix_g, w_in, conv_w, conv_b, lru_wa, lru_ba, lru_wx,
           lru_bx, lru_lambda, attn_sinks, w_lru_out, w_attn_out, w_o, norm_moe_g, w_router_group,
           w_router_expert, moe_w_gate, moe_w_up, moe_w_down):
    bsz, t, _ = x.shape
    xn = _rmsnorm(x, norm_mix_g)
    proj = xn @ w_in
    c0 = LRU_WIDTH
    c1 = c0 + N_HEADS * HEAD_DIM
    c2 = c1 + N_KV_HEADS * HEAD_DIM
    c3 = c2 + N_KV_HEADS * HEAD_DIM
    c4 = c3 + D_MODEL
    xl = proj[..., :c0]
    q = proj[..., c0:c1].reshape(bsz, t, N_KV_HEADS, GQA, HEAD_DIM)
    k = proj[..., c1:c2].reshape(bsz, t, N_KV_HEADS, HEAD_DIM)
    v = proj[..., c2:c3].reshape(bsz, t, N_KV_HEADS, HEAD_DIM)
    g_lru = proj[..., c3:c4]
    g_attn = proj[..., c4:]
    if conv_buf is None:
        conv_buf = jnp.zeros((bsz, CONV_W - 1, LRU_WIDTH), x.dtype)
        h0 = jnp.zeros((bsz, LRU_WIDTH), x.dtype)
    xc, new_conv = _causal_conv(xl, conv_buf, conv_w, conv_b)
    y_lru, h_last = _rg_lru(xc, h0, lru_wa, lru_ba, lru_wx, lru_bx, lru_lambda)
    if k_buf is None:
        y_attn = _window_attn_prompt(q, k, v, attn_sinks)
        new_k, new_v = k[:, -WINDOW:], v[:, -WINDOW:]
    else:
        y_attn, new_k, new_v = _window_attn_sample(q, k, v, k_buf, v_buf, attn_sinks)
    merged = jax.nn.sigmoid(g_lru) * (y_lru @ w_lru_out) + jax.nn.sigmoid(g_attn) * (y_attn @ w_attn_out)
    x = x + merged @ w_o
    x = x + _moe(_rmsnorm(x, norm_moe_g), w_router_group, w_router_expert, moe_w_gate, moe_w_up, moe_w_down)
    return x, new_conv, h_last, new_k, new_v


def setup_inputs(seed: int = 0) -> dict:
    key = jax.random.key(seed)
    ks = jax.random.split(key, 32)

    def nrm(k, shape, scale):
        return jax.random.normal(k, shape, jnp.float32) * scale

    u = jax.random.uniform(ks[14], (DEPTH, LRU_WIDTH), jnp.float32, minval=0.9, maxval=0.999)
    a0 = u ** (1.0 / LRU_C)
    lam = jnp.log(a0) - jnp.log1p(-a0)
    return {
        'x_prompt': nrm(ks[0], (BATCH, SEQ, D_MODEL), 1.0),
        'x_sample': nrm(ks[1], (DEC_BATCH, DEC_SEQ, D_MODEL), 1.0),
        'state_conv': nrm(ks[2], (DEPTH, DEC_BATCH, CONV_W - 1, LRU_WIDTH), 1.0),
        'state_lru_h': nrm(ks[3], (DEPTH, DEC_BATCH, LRU_WIDTH), 0.5),
        'cache_win_k': nrm(ks[4], (DEPTH, DEC_BATCH, WINDOW, N_KV_HEADS, HEAD_DIM), 1.0),
        'cache_win_v': nrm(ks[5], (DEPTH, DEC_BATCH, WINDOW, N_KV_HEADS, HEAD_DIM), 1.0),
        'norm_mix_g': 1.0 + nrm(ks[6], (DEPTH, D_MODEL), 0.02),
        'w_in': nrm(ks[7], (DEPTH, D_MODEL, IN_COLS), D_MODEL ** -0.5),
        'conv_w': nrm(ks[8], (DEPTH, CONV_W, LRU_WIDTH), CONV_W ** -0.5),
        'conv_b': nrm(ks[9], (DEPTH, LRU_WIDTH), 0.01),
        'lru_wa': nrm(ks[10], (DEPTH, LRU_BLOCKS, LRU_BLOCK_DIM, LRU_BLOCK_DIM), LRU_BLOCK_DIM ** -0.5),
        'lru_ba': nrm(ks[11], (DEPTH, LRU_WIDTH), 0.01),
        'lru_wx': nrm(ks[12], (DEPTH, LRU_BLOCKS, LRU_BLOCK_DIM, LRU_BLOCK_DIM), LRU_BLOCK_DIM ** -0.5),
        'lru_bx': nrm(ks[13], (DEPTH, LRU_WIDTH), 0.01),
        'lru_lambda': lam,
        'attn_sinks': nrm(ks[15], (DEPTH, N_HEADS), 0.5),
        'w_lru_out': nrm(ks[16], (DEPTH, LRU_WIDTH, D_MODEL), LRU_WIDTH ** -0.5),
        'w_attn_out': nrm(ks[17], (DEPTH, N_HEADS * HEAD_DIM, D_MODEL), (N_HEADS * HEAD_DIM) ** -0.5),
        'w_o': nrm(ks[18], (DEPTH, D_MODEL, D_MODEL), D_MODEL ** -0.5),
        'norm_moe_g': 1.0 + nrm(ks[19], (DEPTH, D_MODEL), 0.02),
        'w_router_group': nrm(ks[20], (DEPTH, D_MODEL, N_GROUPS), D_MODEL ** -0.5),
        'w_router_expert': nrm(ks[21], (DEPTH, N_GROUPS, D_MODEL, EXPERTS_PER_GROUP), D_MODEL ** -0.5),
        'moe_w_gate': nrm(ks[22], (DEPTH, N_GROUPS, EXPERTS_PER_GROUP, D_MODEL, D_EXPERT), D_MODEL ** -0.5),
        'moe_w_up': nrm(ks[23], (DEPTH, N_GROUPS, EXPERTS_PER_GROUP, D_MODEL, D_EXPERT), D_MODEL ** -0.5),
        'moe_w_down': nrm(ks[24], (DEPTH, N_GROUPS, EXPERTS_PER_GROUP, D_EXPERT, D_MODEL), D_EXPERT ** -0.5),
        'norm_final_g': 1.0 + nrm(ks[25], (D_MODEL,), 0.02),
    }


def reference(x_prompt, x_sample, state_conv, state_lru_h, cache_win_k, cache_win_v, norm_mix_g, w_in,
              conv_w, conv_b, lru_wa, lru_ba, lru_wx, lru_bx, lru_lambda, attn_sinks, w_lru_out,
              w_attn_out, w_o, norm_moe_g, w_router_group, w_router_expert, moe_w_gate, moe_w_up,
              moe_w_down, norm_final_g):
    xp, xs = x_prompt, x_sample
    conv_p, h_p, k_p, v_p = [], [], [], []
    conv_s, h_s, k_s, v_s = [], [], [], []
    for l in range(DEPTH):
        params = (norm_mix_g[l], w_in[l], conv_w[l], conv_b[l], lru_wa[l], lru_ba[l], lru_wx[l],
                  lru_bx[l], lru_lambda[l], attn_sinks[l], w_lru_out[l], w_attn_out[l], w_o[l],
                  norm_moe_g[l], w_router_group[l], w_router_expert[l], moe_w_gate[l], moe_w_up[l],
                  moe_w_down[l])
        xp, c1, h1, k1, v1 = _layer(xp, None, None, None, None, *params)
        xs, c2, h2, k2, v2 = _layer(xs, state_conv[l], state_lru_h[l], cache_win_k[l], cache_win_v[l], *params)
        conv_p.append(c1); h_p.append(h1); k_p.append(k1); v_p.append(v1)
        conv_s.append(c2); h_s.append(h2); k_s.append(k2); v_s.append(v2)
    y_prompt = _rmsnorm(xp, norm_final_g)
    y_sample = _rmsnorm(xs, norm_final_g)
    return (y_prompt, y_sample, jnp.stack(conv_p), jnp.stack(h_p), jnp.stack(k_p), jnp.stack(v_p),
            jnp.stack(conv_s), jnp.stack(h_s), jnp.stack(k_s), jnp.stack(v_s))
```

```python
import functools

import jax
import jax.numpy as jnp
from jax import lax
from jax.experimental import pallas as pl
from jax.experimental.pallas import tpu as pltpu

D_MODEL = 1024
LRU_WIDTH = D_MODEL
LRU_BLOCKS = 16
LRU_BLOCK_DIM = LRU_WIDTH // LRU_BLOCKS
LRU_C = 8.0
CONV_W = 4
N_HEADS = 16
N_KV_HEADS = 4
GQA = N_HEADS // N_KV_HEADS
HEAD_DIM = D_MODEL // N_HEADS
KV_COLS = N_KV_HEADS * HEAD_DIM
WINDOW = 128
N_GROUPS = 4
EXPERTS_PER_GROUP = 8
N_EXPERTS = N_GROUPS * EXPERTS_PER_GROUP
D_EXPERT = D_MODEL // 4
RMS_EPS = 1e-6
NEG_INF = -1e30
IN_COLS = LRU_WIDTH + (N_HEADS + 2 * N_KV_HEADS) * HEAD_DIM + 2 * D_MODEL

SUBLANES = 8
LANES = 128
ROUTER_COLS = LANES
GATE_PACK = 256

F32 = jnp.float32
BF16 = jnp.bfloat16


def _sigmoid(z):
    return 1.0 / (1.0 + jnp.exp(-z))


def _expm1(z):
    u = jnp.exp(z)
    return jnp.where(u == 1.0, z, jnp.where(u == 0.0, -1.0, (u - 1.0) * z / jnp.log(u)))


def _rms_scale(x):
    return x * lax.rsqrt(jnp.mean(x * x, axis=-1, keepdims=True) + RMS_EPS)


def _inproj_kernel(x_ref, g_ref, w_ref, xl_ref, q_ref, k_ref, v_ref, gl_ref, ga_ref):
    xb = (_rms_scale(x_ref[...]) * g_ref[...]).astype(BF16)

    def proj(c0, c1):
        return jnp.dot(xb, w_ref[:, c0:c1], preferred_element_type=F32)

    c0 = LRU_WIDTH
    c1 = c0 + N_HEADS * HEAD_DIM
    c2 = c1 + KV_COLS
    c3 = c2 + KV_COLS
    c4 = c3 + D_MODEL
    xl_ref[...] = proj(0, c0)
    q_ref[...] = (proj(c0, c1) * (HEAD_DIM ** -0.5)).astype(BF16)
    k_ref[...] = proj(c1, c2)
    v_ref[...] = proj(c2, c3)
    gl_ref[...] = _sigmoid(proj(c3, c4)).astype(BF16)
    ga_ref[...] = _sigmoid(proj(c4, IN_COLS)).astype(BF16)


def _inproj(x2, g, w_bf, tm):
    n = x2.shape[0]
    row = lambda i: (i, 0)
    fixed = lambda i: (0, 0)
    outs = [
        jax.ShapeDtypeStruct((n, LRU_WIDTH), F32),
        jax.ShapeDtypeStruct((n, N_HEADS * HEAD_DIM), BF16),
        jax.ShapeDtypeStruct((n, KV_COLS), F32),
        jax.ShapeDtypeStruct((n, KV_COLS), F32),
        jax.ShapeDtypeStruct((n, D_MODEL), BF16),
        jax.ShapeDtypeStruct((n, D_MODEL), BF16),
    ]
    return pl.pallas_call(
        _inproj_kernel,
        out_shape=outs,
        grid=(n // tm,),
        in_specs=[
            pl.BlockSpec((tm, D_MODEL), row),
            pl.BlockSpec((1, D_MODEL), fixed),
            pl.BlockSpec((D_MODEL, IN_COLS), fixed),
        ],
        out_specs=[pl.BlockSpec((tm, o.shape[1]), row) for o in outs],
        compiler_params=pltpu.CompilerParams(dimension_semantics=("parallel",)),
        name="inproj",
    )(x2, g, w_bf)


def _lru_kernel(xl_ref, prev_ref, h0_ref, cw_ref, cb_ref, wbd_ref, ba_ref, bx_ref, lam_ref,
                y_ref, hl_ref, xs_ref, xc_ref, a_ref, b_ref, hc_ref, *, bb, tc):
    t = pl.program_id(1)

    @pl.when(t == 0)
    def _():
        xs_ref[:, 0:SUBLANES, :] = prev_ref[...]
        hc_ref[...] = h0_ref[...]

    @pl.when(t > 0)
    def _():
        xs_ref[:, 0:SUBLANES, :] = xs_ref[:, tc:tc + SUBLANES, :]

    for s in range(bb):
        xs_ref[s, SUBLANES:SUBLANES + tc, :] = xl_ref[s * tc:(s + 1) * tc, :]
    cw = cw_ref[...]
    for s in range(bb):
        acc = cb_ref[...] + xs_ref[s, SUBLANES:SUBLANES + tc, :] * cw[CONV_W - 1:CONV_W, :]
        for j in range(CONV_W - 1):
            off = SUBLANES - (CONV_W - 1) + j
            acc = acc + xs_ref[s, off:off + tc, :] * cw[j:j + 1, :]
        xc_ref[s * tc:(s + 1) * tc, :] = acc

    lam = lam_ref[...]
    softplus_neg_lam = jnp.maximum(-lam, 0.0) + jnp.log1p(jnp.exp(-jnp.abs(lam)))
    for g in range(LRU_WIDTH // GATE_PACK):
        cols = slice(g * GATE_PACK, (g + 1) * GATE_PACK)
        xc = xc_ref[:, cols]
        gates = jnp.dot(xc.astype(BF16), wbd_ref[g], preferred_element_type=F32)
        r = _sigmoid(gates[:, :GATE_PACK] + ba_ref[:, cols])
        ig = _sigmoid(gates[:, GATE_PACK:] + bx_ref[:, cols])
        log_a = (-LRU_C) * r * softplus_neg_lam[:, cols]
        a_ref[:, cols] = jnp.exp(log_a)
        b_ref[:, cols] = jnp.sqrt(-_expm1(2.0 * log_a)) * (ig * xc)

    row = lax.broadcasted_iota(jnp.int32, (SUBLANES, LRU_WIDTH), 0)
    for s in range(bb):
        def group(gi, h8, s=s):
            r0 = pl.multiple_of(s * tc + gi * SUBLANES, SUBLANES)
            a = a_ref[pl.ds(r0, SUBLANES), :]
            b = b_ref[pl.ds(r0, SUBLANES), :]
            for d in (1, 2, 4):
                a_up = jnp.where(row >= d, pltpu.roll(a, d, 0), 1.0)
                b_up = jnp.where(row >= d, pltpu.roll(b, d, 0), 0.0)
                b = a * b_up + b
                a = a * a_up
            h = a * h8 + b
            b_ref[pl.ds(r0, SUBLANES), :] = h
            return jnp.broadcast_to(h[SUBLANES - 1:SUBLANES, :], (SUBLANES, LRU_WIDTH))

        h8 = lax.fori_loop(0, tc // SUBLANES, group,
                           jnp.broadcast_to(hc_ref[s], (SUBLANES, LRU_WIDTH)))
        hc_ref[s] = h8[0:1, :]

    y_ref[...] = b_ref[...].astype(y_ref.dtype)

    @pl.when(t == pl.num_programs(1) - 1)
    def _():
        hl_ref[...] = hc_ref[...]


def _lru(xl2, prev8, h0, cw, cb, wbd, ba, bx, lam, *, nseq, seqlen, bb, tc):
    nt = seqlen // tc
    assert bb == 1 or nt == 1
    rows = bb * tc
    fixed2 = lambda b, t: (0, 0)
    fixed3 = lambda b, t: (0, 0, 0)
    return pl.pallas_call(
        functools.partial(_lru_kernel, bb=bb, tc=tc),
        out_shape=[
            jax.ShapeDtypeStruct((nseq * seqlen, LRU_WIDTH), BF16),
            jax.ShapeDtypeStruct((nseq, 1, LRU_WIDTH), F32),
        ],
        grid=(nseq // bb, nt),
        in_specs=[
            pl.BlockSpec((rows, LRU_WIDTH), lambda b, t: (b * nt + t, 0)),
            pl.BlockSpec((bb, SUBLANES, LRU_WIDTH), lambda b, t: (b, 0, 0)),
            pl.BlockSpec((bb, 1, LRU_WIDTH), lambda b, t: (b, 0, 0)),
            pl.BlockSpec((CONV_W, LRU_WIDTH), fixed2),
            pl.BlockSpec((1, LRU_WIDTH), fixed2),
            pl.BlockSpec((LRU_WIDTH // GATE_PACK, GATE_PACK, 2 * GATE_PACK), fixed3),
            pl.BlockSpec((1, LRU_WIDTH), fixed2),
            pl.BlockSpec((1, LRU_WIDTH), fixed2),
            pl.BlockSpec((1, LRU_WIDTH), fixed2),
        ],
        out_specs=[
            pl.BlockSpec((rows, LRU_WIDTH), lambda b, t: (b * nt + t, 0)),
            pl.BlockSpec((bb, 1, LRU_WIDTH), lambda b, t: (b, 0, 0)),
        ],
        scratch_shapes=[
            pltpu.VMEM((bb, tc + SUBLANES, LRU_WIDTH), F32),
            pltpu.VMEM((rows, LRU_WIDTH), F32),
            pltpu.VMEM((rows, LRU_WIDTH), F32),
            pltpu.VMEM((rows, LRU_WIDTH), F32),
            pltpu.VMEM((bb, 1, LRU_WIDTH), F32),
        ],
        compiler_params=pltpu.CompilerParams(dimension_semantics=("parallel", "arbitrary")),
        name="conv_rglru",
    )(xl2, prev8, h0, cw, cb, wbd, ba, bx, lam)


def _softmax_sink_pv(s, sink, v_bf):
    m = jnp.maximum(jnp.max(s, axis=-1, keepdims=True), sink)
    p = jnp.exp(s - m)
    denom = jnp.sum(p, axis=-1, keepdims=True) + jnp.exp(sink - m)
    o = jnp.dot(p.astype(BF16), v_bf, preferred_element_type=F32)
    return o / denom


def _attn_prompt_kernel(sink_ref, q_ref, kc_ref, kp_ref, vc_ref, vp_ref, bias_ref, o_ref):
    bsel = jnp.where(pl.program_id(1) == 0, 0, 1)
    kcat = jnp.concatenate([kp_ref[0], kc_ref[0]], axis=0).astype(BF16)
    vcat = jnp.concatenate([vp_ref[0], vc_ref[0]], axis=0).astype(BF16)
    q = q_ref[0]
    outs = []
    for h in range(N_HEADS):
        kv = slice((h // GQA) * HEAD_DIM, (h // GQA + 1) * HEAD_DIM)
        s = lax.dot_general(q[:, h * HEAD_DIM:(h + 1) * HEAD_DIM], kcat[:, kv],
                            (((1,), (1,)), ((), ())), preferred_element_type=F32)
        s = s + bias_ref[bsel, h]
        outs.append(_softmax_sink_pv(s, sink_ref[h], vcat[:, kv]))
    o_ref[0] = jnp.concatenate(outs, axis=-1).astype(o_ref.dtype)


def _attn_prompt(q3, k3, v3, sinks, bias):
    bsz, t, _ = q3.shape
    nb = t // WINDOW
    cur = lambda b, i: (b, i, 0)
    prev = lambda b, i: (b, jnp.maximum(i - 1, 0), 0)
    return pl.pallas_call(
        _attn_prompt_kernel,
        out_shape=jax.ShapeDtypeStruct((bsz, t, N_HEADS * HEAD_DIM), BF16),
        grid=(bsz, nb),
        in_specs=[
            pl.BlockSpec(memory_space=pltpu.SMEM),
            pl.BlockSpec((1, WINDOW, N_HEADS * HEAD_DIM), cur),
            pl.BlockSpec((1, WINDOW, KV_COLS), cur),
            pl.BlockSpec((1, WINDOW, KV_COLS), prev),
            pl.BlockSpec((1, WINDOW, KV_COLS), cur),
            pl.BlockSpec((1, WINDOW, KV_COLS), prev),
            pl.BlockSpec((2, N_HEADS, WINDOW, 2 * WINDOW), lambda b, i: (0, 0, 0, 0)),
        ],
        out_specs=pl.BlockSpec((1, WINDOW, N_HEADS * HEAD_DIM), cur),
        compiler_params=pltpu.CompilerParams(dimension_semantics=("parallel", "arbitrary")),
        name="swa_prompt",
    )(sinks, q3, k3, k3, v3, v3, bias)


def _attn_sample_kernel(q_ref, kn_ref, vn_ref, ck_ref, cv_ref, bias_ref, sink_ref,
                        o_ref, nk_ref, nv_ref, *, bb, t):
    keep = WINDOW - t

    def seq(s, carry):
        ck = ck_ref[s]
        cv = cv_ref[s]
        kn = kn_ref[s]
        vn = vn_ref[s]
        nk_ref[s, 0:keep, :] = ck[t:, :]
        nk_ref[s, keep:WINDOW, :] = kn
        nv_ref[s, 0:keep, :] = cv[t:, :]
        nv_ref[s, keep:WINDOW, :] = vn
        kcat = jnp.concatenate([ck, kn], axis=0).astype(BF16)
        vcat = jnp.concatenate([cv, vn], axis=0).astype(BF16)
        for kh in range(N_KV_HEADS):
            kv = slice(kh * HEAD_DIM, (kh + 1) * HEAD_DIM)
            sc = lax.dot_general(q_ref[s, kh], kcat[:, kv], (((1,), (1,)), ((), ())),
                                 preferred_element_type=F32)
            sc = sc + bias_ref[kh]
            o_ref[s, kh] = _softmax_sink_pv(sc, sink_ref[kh], vcat[:, kv]).astype(o_ref.dtype)
        return carry

    lax.fori_loop(0, bb, seq, 0)


def _attn_sample(q4, kn3, vn3, ck3, cv3, bias, sink_rows, bb):
    nseq, _, rows, _ = q4.shape
    t = kn3.shape[1]
    b3 = lambda i: (i, 0, 0)
    b4 = lambda i: (i, 0, 0, 0)
    return pl.pallas_call(
        functools.partial(_attn_sample_kernel, bb=bb, t=t),
        out_shape=[
            jax.ShapeDtypeStruct((nseq, N_KV_HEADS, rows, HEAD_DIM), BF16),
            jax.ShapeDtypeStruct((nseq, WINDOW, KV_COLS), F32),
            jax.ShapeDtypeStruct((nseq, WINDOW, KV_COLS), F32),
        ],
        grid=(nseq // bb,),
        in_specs=[
            pl.BlockSpec((bb, N_KV_HEADS, rows, HEAD_DIM), b4),
            pl.BlockSpec((bb, t, KV_COLS), b3),
            pl.BlockSpec((bb, t, KV_COLS), b3),
            pl.BlockSpec((bb, WINDOW, KV_COLS), b3),
            pl.BlockSpec((bb, WINDOW, KV_COLS), b3),
            pl.BlockSpec((N_KV_HEADS, rows, WINDOW + t), lambda i: (0, 0, 0)),
            pl.BlockSpec((N_KV_HEADS, rows, 1), lambda i: (0, 0, 0)),
        ],
        out_specs=[
            pl.BlockSpec((bb, N_KV_HEADS, rows, HEAD_DIM), b4),
            pl.BlockSpec((bb, WINDOW, KV_COLS), b3),
            pl.BlockSpec((bb, WINDOW, KV_COLS), b3),
        ],
        compiler_params=pltpu.CompilerParams(dimension_semantics=("parallel",)),
        name="swa_sample",
    )(q4, kn3, vn3, ck3, cv3, bias, sink_rows)


def _router_combine(logits):
    lane = lax.broadcasted_iota(jnp.int32, logits.shape, 1)
    big = jnp.int32(ROUTER_COLS)
    gl = jnp.where(lane < N_GROUPS, logits, -jnp.inf)
    gmax = jnp.max(gl, axis=-1, keepdims=True)
    p_grp = 1.0 / jnp.sum(jnp.exp(gl - gmax), axis=-1, keepdims=True)
    g_idx = jnp.min(jnp.where(gl == gmax, lane, big), axis=-1, keepdims=True)
    e_lo = N_GROUPS + EXPERTS_PER_GROUP * g_idx
    el = jnp.where((lane >= e_lo) & (lane < e_lo + EXPERTS_PER_GROUP), logits, -jnp.inf)
    m1 = jnp.max(el, axis=-1, keepdims=True)
    i1 = jnp.min(jnp.where(el == m1, lane, big), axis=-1, keepdims=True)
    el2 = jnp.where(lane == i1, -jnp.inf, el)
    m2 = jnp.max(el2, axis=-1, keepdims=True)
    i2 = jnp.min(jnp.where(el2 == m2, lane, big), axis=-1, keepdims=True)
    e21 = jnp.exp(m2 - m1)
    w1 = 1.0 / (1.0 + e21)
    w2 = e21 * w1
    return jnp.where(lane == i1, p_grp * w1, jnp.where(lane == i2, p_grp * w2, 0.0))


def _merge_kernel(x_ref, yl_ref, ya_ref, gl_ref, ga_ref, wlo_ref, wao_ref, wo_ref, g_ref,
                  wrh_ref, wrl_ref, x1_ref, xn_ref, comb_ref):
    a = jnp.dot(yl_ref[...], wlo_ref[...], preferred_element_type=F32)
    b = jnp.dot(ya_ref[...], wao_ref[...], preferred_element_type=F32)
    merged = gl_ref[...].astype(F32) * a + ga_ref[...].astype(F32) * b
    x1 = x_ref[...] + jnp.dot(merged.astype(BF16), wo_ref[...], preferred_element_type=F32)
    x1_ref[...] = x1
    xn = _rms_scale(x1) * g_ref[...]
    xh = xn.astype(BF16)
    xn_ref[...] = xh
    xlo = (xn - xh.astype(F32)).astype(BF16)
    logits = (jnp.dot(xh, wrh_ref[...], preferred_element_type=F32)
              + jnp.dot(xlo, wrh_ref[...], preferred_element_type=F32)
              + jnp.dot(xh, wrl_ref[...], preferred_element_type=F32))
    comb_ref[...] = _router_combine(logits)


def _merge(x2, yl, ya, gl, ga, wlo, wao, wo, g, wrh, wrl, tm):
    n = x2.shape[0]
    row = lambda i: (i, 0)
    fixed = lambda i: (0, 0)
    act = pl.BlockSpec((tm, D_MODEL), row)
    wsq = pl.BlockSpec((D_MODEL, D_MODEL), fixed)
    wr = pl.BlockSpec((D_MODEL, ROUTER_COLS), fixed)
    return pl.pallas_call(
        _merge_kernel,
        out_shape=[
            jax.ShapeDtypeStruct((n, D_MODEL), F32),
            jax.ShapeDtypeStruct((n, D_MODEL), BF16),
            jax.ShapeDtypeStruct((n, ROUTER_COLS), F32),
        ],
        grid=(n // tm,),
        in_specs=[act, act, act, act, act, wsq, wsq, wsq,
                  pl.BlockSpec((1, D_MODEL), fixed), wr, wr],
        out_specs=[act, act, pl.BlockSpec((tm, ROUTER_COLS), row)],
        compiler_params=pltpu.CompilerParams(dimension_semantics=("parallel",)),
        name="merge_router",
    )(x2, yl, ya, gl, ga, wlo, wao, wo, g, wrh, wrl)


def _moe_kernel(x1_ref, xn_ref, comb_ref, wgu_ref, wd_ref, g_ref, y_ref, acc_ref):
    e = pl.program_id(1)

    @pl.when(e == 0)
    def _():
        acc_ref[...] = jnp.zeros_like(acc_ref)

    comb = comb_ref[...]
    lane = lax.broadcasted_iota(jnp.int32, comb.shape, 1)
    c = jnp.sum(jnp.where(lane == N_GROUPS + e, comb, 0.0), axis=-1, keepdims=True)
    gu = jnp.dot(xn_ref[...], wgu_ref[0], preferred_element_type=F32)
    gate = gu[:, :D_EXPERT]
    h = gate * _sigmoid(gate) * gu[:, D_EXPERT:] * c
    acc_ref[...] += jnp.dot(h.astype(BF16), wd_ref[0], preferred_element_type=F32)

    @pl.when(e == pl.num_programs(1) - 1)
    def _():
        y_ref[...] = _rms_scale(x1_ref[...] + acc_ref[...]) * g_ref[...]


def _moe(x1, xn, comb, wgu, wd, g, tm):
    n = x1.shape[0]
    row = lambda i, e: (i, 0)
    return pl.pallas_call(
        _moe_kernel,
        out_shape=jax.ShapeDtypeStruct((n, D_MODEL), F32),
        grid=(n // tm, N_EXPERTS),
        in_specs=[
            pl.BlockSpec((tm, D_MODEL), row),
            pl.BlockSpec((tm, D_MODEL), row),
            pl.BlockSpec((tm, ROUTER_COLS), row),
            pl.BlockSpec((1, D_MODEL, 2 * D_EXPERT), lambda i, e: (e, 0, 0)),
            pl.BlockSpec((1, D_EXPERT, D_MODEL), lambda i, e: (e, 0, 0)),
            pl.BlockSpec((1, D_MODEL), lambda i, e: (0, 0)),
        ],
        out_specs=pl.BlockSpec((tm, D_MODEL), row),
        scratch_shapes=[pltpu.VMEM((tm, D_MODEL), F32)],
        compiler_params=pltpu.CompilerParams(dimension_semantics=("parallel", "arbitrary")),
        name="moe_final",
    )(x1, xn, comb, wgu, wd, g)


def _row_tile(n, want):
    tm = min(n, want)
    assert n % tm == 0
    return tm


def _alibi_slopes():
    return jnp.exp2(-8.0 * jnp.arange(1, N_HEADS + 1, dtype=F32) / N_HEADS)


def _prompt_bias():
    qi = jnp.arange(WINDOW)[:, None]
    kj = jnp.arange(2 * WINDOW)[None, :]
    dist = WINDOW + qi - kj
    in_window = (dist >= 0) & (dist < WINDOW)
    valid = jnp.stack([in_window & (kj >= WINDOW), in_window])
    score = -_alibi_slopes()[None, :, None, None] * dist.astype(F32)[None, None]
    return jnp.where(valid[:, None], score, NEG_INF)


def _sample_bias(t):
    q_pos = jnp.tile(jnp.arange(t), GQA)[:, None]
    k_pos = (jnp.arange(WINDOW + t) - WINDOW)[None, :]
    dist = q_pos - k_pos
    valid = (dist >= 0) & (dist < WINDOW)
    slopes = jnp.repeat(_alibi_slopes().reshape(N_KV_HEADS, GQA), t, axis=1)
    score = -slopes[:, :, None] * dist.astype(F32)[None]
    return jnp.where(valid[None], score, NEG_INF)


def _layer_params(norm_mix_g, w_in, conv_w, conv_b, lru_wa, lru_ba, lru_wx, lru_bx, lru_lambda,
                  attn_sinks, w_lru_out, w_attn_out, w_o, norm_moe_g, w_router_group,
                  w_router_expert, moe_w_gate, moe_w_up, moe_w_down):
    per_pack = GATE_PACK // LRU_BLOCK_DIM

    def pack_diag(w):
        w4 = w.reshape(LRU_WIDTH // GATE_PACK, per_pack, LRU_BLOCK_DIM, LRU_BLOCK_DIM)
        eye = jnp.eye(per_pack, dtype=w.dtype)
        return jnp.einsum('pbij,bc->pbicj', w4, eye).reshape(-1, GATE_PACK, GATE_PACK)

    wbd = jnp.concatenate([pack_diag(lru_wa), pack_diag(lru_wx)], axis=-1).astype(BF16)
    w_router = jnp.concatenate(
        [w_router_group, jnp.transpose(w_router_expert, (1, 0, 2)).reshape(D_MODEL, N_EXPERTS)],
        axis=1)
    w_router = jnp.pad(w_router, ((0, 0), (0, ROUTER_COLS - w_router.shape[1])))
    wr_hi = w_router.astype(BF16)
    wr_lo = (w_router - wr_hi.astype(F32)).astype(BF16)
    wgu = jnp.concatenate([moe_w_gate, moe_w_up], axis=-1).reshape(
        N_EXPERTS, D_MODEL, 2 * D_EXPERT).astype(BF16)
    wd = moe_w_down.reshape(N_EXPERTS, D_EXPERT, D_MODEL).astype(BF16)
    row = lambda v: v.reshape(1, -1)
    return dict(
        norm_mix_g=row(norm_mix_g), w_in=w_in.astype(BF16), conv_w=conv_w, conv_b=row(conv_b),
        wbd=wbd, ba=row(lru_ba), bx=row(lru_bx), lam=row(lru_lambda), sinks=attn_sinks,
        wlo=w_lru_out.astype(BF16), wao=w_attn_out.astype(BF16), wo=w_o.astype(BF16),
        norm_moe_g=row(norm_moe_g), wr_hi=wr_hi, wr_lo=wr_lo, wgu=wgu, wd=wd)


def _mix_and_moe(x2, yl, ya, gl, ga, p, norm_final_g):
    n = x2.shape[0]
    x1, xn, comb = _merge(x2, yl, ya, gl, ga, p['wlo'], p['wao'], p['wo'], p['norm_moe_g'],
                          p['wr_hi'], p['wr_lo'], _row_tile(n, 512))
    return _moe(x1, xn, comb, p['wgu'], p['wd'], norm_final_g.reshape(1, -1), _row_tile(n, 1024))


def _prompt_layer(x, p, norm_final_g):
    bsz, t, _ = x.shape
    assert t % WINDOW == 0 and t >= CONV_W - 1
    n = bsz * t
    x2 = x.reshape(n, D_MODEL)
    xl, q, k, v, gl, ga = _inproj(x2, p['norm_mix_g'], p['w_in'], _row_tile(n, 512))
    tc = _row_tile(t, 256)
    yl, h_last = _lru(xl, jnp.zeros((bsz, SUBLANES, LRU_WIDTH), F32),
                      jnp.zeros((bsz, 1, LRU_WIDTH), F32), p['conv_w'], p['conv_b'], p['wbd'],
                      p['ba'], p['bx'], p['lam'], nseq=bsz, seqlen=t, bb=1, tc=tc)
    k3 = k.reshape(bsz, t, KV_COLS)
    v3 = v.reshape(bsz, t, KV_COLS)
    ya = _attn_prompt(q.reshape(bsz, t, -1), k3, v3, p['sinks'], _prompt_bias())
    y = _mix_and_moe(x2, yl, ya.reshape(n, -1), gl, ga, p, norm_final_g)
    new_conv = xl.reshape(bsz, t, LRU_WIDTH)[:, t - (CONV_W - 1):]
    new_k = k3[:, t - WINDOW:].reshape(bsz, WINDOW, N_KV_HEADS, HEAD_DIM)
    new_v = v3[:, t - WINDOW:].reshape(bsz, WINDOW, N_KV_HEADS, HEAD_DIM)
    return y.reshape(bsz, t, D_MODEL), new_conv, h_last.reshape(bsz, LRU_WIDTH), new_k, new_v


def _sample_layer(x, conv_buf, h0, k_buf, v_buf, p, norm_final_g):
    bsz, t, _ = x.shape
    assert t % SUBLANES == 0 and CONV_W - 1 <= t <= WINDOW
    n = bsz * t
    x2 = x.reshape(n, D_MODEL)
    xl, q, k, v, gl, ga = _inproj(x2, p['norm_mix_g'], p['w_in'], _row_tile(n, 512))
    bb = _row_tile(bsz, 16)
    prev8 = jnp.pad(conv_buf, ((0, 0), (SUBLANES - (CONV_W - 1), 0), (0, 0)))
    yl, h_last = _lru(xl, prev8, h0.reshape(bsz, 1, LRU_WIDTH), p['conv_w'], p['conv_b'],
                      p['wbd'], p['ba'], p['bx'], p['lam'], nseq=bsz, seqlen=t, bb=bb, tc=t)
    q4 = q.reshape(bsz, t, N_KV_HEADS, GQA, HEAD_DIM).transpose(0, 2, 3, 1, 4).reshape(
        bsz, N_KV_HEADS, GQA * t, HEAD_DIM)
    sink_rows = jnp.repeat(p['sinks'].reshape(N_KV_HEADS, GQA), t, axis=1)[:, :, None]
    o4, new_k, new_v = _attn_sample(
        q4, k.reshape(bsz, t, KV_COLS), v.reshape(bsz, t, KV_COLS),
        k_buf.reshape(bsz, WINDOW, KV_COLS), v_buf.reshape(bsz, WINDOW, KV_COLS),
        _sample_bias(t), sink_rows, _row_tile(bsz, 8))
    ya = o4.reshape(bsz, N_KV_HEADS, GQA, t, HEAD_DIM).transpose(0, 3, 1, 2, 4).reshape(n, -1)
    y = _mix_and_moe(x2, yl, ya, gl, ga, p, norm_final_g)
    new_conv = xl.reshape(bsz, t, LRU_WIDTH)[:, t - (CONV_W - 1):]
    shape = (bsz, WINDOW, N_KV_HEADS, HEAD_DIM)
    return (y.reshape(bsz, t, D_MODEL), new_conv, h_last.reshape(bsz, LRU_WIDTH),
            new_k.reshape(shape), new_v.reshape(shape))


def kernel(x_prompt, x_sample, state_conv, state_lru_h, cache_win_k, cache_win_v, norm_mix_g, w_in, conv_w, conv_b, lru_wa, lru_ba, lru_wx, lru_bx, lru_lambda, attn_sinks, w_lru_out, w_attn_out, w_o, norm_moe_g, w_router_group, w_router_expert, moe_w_gate, moe_w_up, moe_w_down, norm_final_g):
    depth = w_in.shape[0]
    assert depth == 1, "the final norm is fused into the single layer's MoE kernel"
    p = _layer_params(norm_mix_g[0], w_in[0], conv_w[0], conv_b[0], lru_wa[0], lru_ba[0],
                      lru_wx[0], lru_bx[0], lru_lambda[0], attn_sinks[0], w_lru_out[0],
                      w_attn_out[0], w_o[0], norm_moe_g[0], w_router_group[0],
                      w_router_expert[0], moe_w_gate[0], moe_w_up[0], moe_w_down[0])
    yp, c1, h1, k1, v1 = _prompt_layer(x_prompt, p, norm_final_g)
    ys, c2, h2, k2, v2 = _sample_layer(x_sample, state_conv[0], state_lru_h[0], cache_win_k[0],
                                       cache_win_v[0], p, norm_final_g)
    return (yp, ys, c1[None], h1[None], k1[None], v1[None], c2[None], h2[None], k2[None], v2[None])
```

```python
import functools
import math

import jax
import jax.numpy as jnp
from jax import lax
from jax.experimental import pallas as pl
from jax.experimental.pallas import tpu as pltpu

D_MODEL = 1024
LRU_WIDTH = D_MODEL
LRU_BLOCKS = 16
LRU_BLOCK_DIM = LRU_WIDTH // LRU_BLOCKS
LRU_C = 8.0
CONV_W = 4
N_HEADS = 16
N_KV_HEADS = 4
GQA = N_HEADS // N_KV_HEADS
HEAD_DIM = D_MODEL // N_HEADS
KV_COLS = N_KV_HEADS * HEAD_DIM
WINDOW = 128
N_GROUPS = 4
EXPERTS_PER_GROUP = 8
N_EXPERTS = N_GROUPS * EXPERTS_PER_GROUP
D_EXPERT = D_MODEL // 4
RMS_EPS = 1e-6
NEG_INF = -1e30
IN_COLS = LRU_WIDTH + (N_HEADS + 2 * N_KV_HEADS) * HEAD_DIM + 2 * D_MODEL

SUBLANES = 8
LANES = 128
ROUTER_COLS = LANES
GATE_PACK = 256

MERGE_ROWS = 512
DISPATCH_ROWS = 1024
EXPERT_ROWS = 256

F32 = jnp.float32
BF16 = jnp.bfloat16


def _sigmoid(z):
    return 1.0 / (1.0 + jnp.exp(-z))


def _expm1(z):
    u = jnp.exp(z)
    return jnp.where(u == 1.0, z, jnp.where(u == 0.0, -1.0, (u - 1.0) * z / jnp.log(u)))


def _rms_scale(x):
    return x * lax.rsqrt(jnp.mean(x * x, axis=-1, keepdims=True) + RMS_EPS)


def _inproj_kernel(x_ref, g_ref, w_ref, xl_ref, q_ref, k_ref, v_ref, gl_ref, ga_ref):
    xb = (_rms_scale(x_ref[...]) * g_ref[...]).astype(BF16)

    def proj(c0, c1):
        return jnp.dot(xb, w_ref[:, c0:c1], preferred_element_type=F32)

    c0 = LRU_WIDTH
    c1 = c0 + N_HEADS * HEAD_DIM
    c2 = c1 + KV_COLS
    c3 = c2 + KV_COLS
    c4 = c3 + D_MODEL
    xl_ref[...] = proj(0, c0)
    q_ref[...] = (proj(c0, c1) * (HEAD_DIM ** -0.5)).astype(BF16)
    k_ref[...] = proj(c1, c2)
    v_ref[...] = proj(c2, c3)
    gl_ref[...] = _sigmoid(proj(c3, c4)).astype(BF16)
    ga_ref[...] = _sigmoid(proj(c4, IN_COLS)).astype(BF16)


def _inproj(x2, g, w_bf, tm):
    n = x2.shape[0]
    row = lambda i: (i, 0)
    fixed = lambda i: (0, 0)
    outs = [
        jax.ShapeDtypeStruct((n, LRU_WIDTH), F32),
        jax.ShapeDtypeStruct((n, N_HEADS * HEAD_DIM), BF16),
        jax.ShapeDtypeStruct((n, KV_COLS), F32),
        jax.ShapeDtypeStruct((n, KV_COLS), F32),
        jax.ShapeDtypeStruct((n, D_MODEL), BF16),
        jax.ShapeDtypeStruct((n, D_MODEL), BF16),
    ]
    return pl.pallas_call(
        _inproj_kernel,
        out_shape=outs,
        grid=(n // tm,),
        in_specs=[
            pl.BlockSpec((tm, D_MODEL), row),
            pl.BlockSpec((1, D_MODEL), fixed),
            pl.BlockSpec((D_MODEL, IN_COLS), fixed),
        ],
        out_specs=[pl.BlockSpec((tm, o.shape[1]), row) for o in outs],
        compiler_params=pltpu.CompilerParams(dimension_semantics=("parallel",)),
        name="inproj",
    )(x2, g, w_bf)


def _lru_kernel(xl_ref, prev_ref, h0_ref, cw_ref, cb_ref, wbd_ref, ba_ref, bx_ref, lam_ref,
                y_ref, hl_ref, xs_ref, xc_ref, a_ref, b_ref, hc_ref, *, bb, tc):
    t = pl.program_id(1)

    @pl.when(t == 0)
    def _():
        xs_ref[:, 0:SUBLANES, :] = prev_ref[...]
        hc_ref[...] = h0_ref[...]

    @pl.when(t > 0)
    def _():
        xs_ref[:, 0:SUBLANES, :] = xs_ref[:, tc:tc + SUBLANES, :]

    for s in range(bb):
        xs_ref[s, SUBLANES:SUBLANES + tc, :] = xl_ref[s * tc:(s + 1) * tc, :]
    cw = cw_ref[...]
    for s in range(bb):
        acc = cb_ref[...] + xs_ref[s, SUBLANES:SUBLANES + tc, :] * cw[CONV_W - 1:CONV_W, :]
        for j in range(CONV_W - 1):
            off = SUBLANES - (CONV_W - 1) + j
            acc = acc + xs_ref[s, off:off + tc, :] * cw[j:j + 1, :]
        xc_ref[s * tc:(s + 1) * tc, :] = acc

    lam = lam_ref[...]
    softplus_neg_lam = jnp.maximum(-lam, 0.0) + jnp.log1p(jnp.exp(-jnp.abs(lam)))
    for g in range(LRU_WIDTH // GATE_PACK):
        cols = slice(g * GATE_PACK, (g + 1) * GATE_PACK)
        xc = xc_ref[:, cols]
        gates = jnp.dot(xc.astype(BF16), wbd_ref[g], preferred_element_type=F32)
        r = _sigmoid(gates[:, :GATE_PACK] + ba_ref[:, cols])
        ig = _sigmoid(gates[:, GATE_PACK:] + bx_ref[:, cols])
        log_a = (-LRU_C) * r * softplus_neg_lam[:, cols]
        a_ref[:, cols] = jnp.exp(log_a)
        b_ref[:, cols] = jnp.sqrt(-_expm1(2.0 * log_a)) * (ig * xc)

    row = lax.broadcasted_iota(jnp.int32, (SUBLANES, LRU_WIDTH), 0)
    for s in range(bb):
        def group(gi, h8, s=s):
            r0 = pl.multiple_of(s * tc + gi * SUBLANES, SUBLANES)
            a = a_ref[pl.ds(r0, SUBLANES), :]
            b = b_ref[pl.ds(r0, SUBLANES), :]
            for d in (1, 2, 4):
                a_up = jnp.where(row >= d, pltpu.roll(a, d, 0), 1.0)
                b_up = jnp.where(row >= d, pltpu.roll(b, d, 0), 0.0)
                b = a * b_up + b
                a = a * a_up
            h = a * h8 + b
            b_ref[pl.ds(r0, SUBLANES), :] = h
            return jnp.broadcast_to(h[SUBLANES - 1:SUBLANES, :], (SUBLANES, LRU_WIDTH))

        h8 = lax.fori_loop(0, tc // SUBLANES, group,
                           jnp.broadcast_to(hc_ref[s], (SUBLANES, LRU_WIDTH)))
        hc_ref[s] = h8[0:1, :]

    y_ref[...] = b_ref[...].astype(y_ref.dtype)

    @pl.when(t == pl.num_programs(1) - 1)
    def _():
        hl_ref[...] = hc_ref[...]


def _lru(xl2, prev8, h0, cw, cb, wbd, ba, bx, lam, *, nseq, seqlen, bb, tc):
    nt = seqlen // tc
    assert bb == 1 or nt == 1
    rows = bb * tc
    fixed2 = lambda b, t: (0, 0)
    fixed3 = lambda b, t: (0, 0, 0)
    return pl.pallas_call(
        functools.partial(_lru_kernel, bb=bb, tc=tc),
        out_shape=[
            jax.ShapeDtypeStruct((nseq * seqlen, LRU_WIDTH), BF16),
            jax.ShapeDtypeStruct((nseq, 1, LRU_WIDTH), F32),
        ],
        grid=(nseq // bb, nt),
        in_specs=[
            pl.BlockSpec((rows, LRU_WIDTH), lambda b, t: (b * nt + t, 0)),
            pl.BlockSpec((bb, SUBLANES, LRU_WIDTH), lambda b, t: (b, 0, 0)),
            pl.BlockSpec((bb, 1, LRU_WIDTH), lambda b, t: (b, 0, 0)),
            pl.BlockSpec((CONV_W, LRU_WIDTH), fixed2),
            pl.BlockSpec((1, LRU_WIDTH), fixed2),
            pl.BlockSpec((LRU_WIDTH // GATE_PACK, GATE_PACK, 2 * GATE_PACK), fixed3),
            pl.BlockSpec((1, LRU_WIDTH), fixed2),
            pl.BlockSpec((1, LRU_WIDTH), fixed2),
            pl.BlockSpec((1, LRU_WIDTH), fixed2),
        ],
        out_specs=[
            pl.BlockSpec((rows, LRU_WIDTH), lambda b, t: (b * nt + t, 0)),
            pl.BlockSpec((bb, 1, LRU_WIDTH), lambda b, t: (b, 0, 0)),
        ],
        scratch_shapes=[
            pltpu.VMEM((bb, tc + SUBLANES, LRU_WIDTH), F32),
            pltpu.VMEM((rows, LRU_WIDTH), F32),
            pltpu.VMEM((rows, LRU_WIDTH), F32),
            pltpu.VMEM((rows, LRU_WIDTH), F32),
            pltpu.VMEM((bb, 1, LRU_WIDTH), F32),
        ],
        compiler_params=pltpu.CompilerParams(dimension_semantics=("parallel", "arbitrary")),
        name="conv_rglru",
    )(xl2, prev8, h0, cw, cb, wbd, ba, bx, lam)


def _softmax_sink_pv(s, sink, v_bf):
    m = jnp.maximum(jnp.max(s, axis=-1, keepdims=True), sink)
    p = jnp.exp(s - m)
    denom = jnp.sum(p, axis=-1, keepdims=True) + jnp.exp(sink - m)
    o = jnp.dot(p.astype(BF16), v_bf, preferred_element_type=F32)
    return o / denom


def _attn_prompt_kernel(sink_ref, q_ref, kc_ref, kp_ref, vc_ref, vp_ref, bias_ref, o_ref):
    bsel = jnp.where(pl.program_id(1) == 0, 0, 1)
    kcat = jnp.concatenate([kp_ref[0], kc_ref[0]], axis=0).astype(BF16)
    vcat = jnp.concatenate([vp_ref[0], vc_ref[0]], axis=0).astype(BF16)
    q = q_ref[0]
    outs = []
    for h in range(N_HEADS):
        kv = slice((h // GQA) * HEAD_DIM, (h // GQA + 1) * HEAD_DIM)
        s = lax.dot_general(q[:, h * HEAD_DIM:(h + 1) * HEAD_DIM], kcat[:, kv],
                            (((1,), (1,)), ((), ())), preferred_element_type=F32)
        s = s + bias_ref[bsel, h]
        outs.append(_softmax_sink_pv(s, sink_ref[h], vcat[:, kv]))
    o_ref[0] = jnp.concatenate(outs, axis=-1).astype(o_ref.dtype)


def _attn_prompt(q3, k3, v3, sinks, bias):
    bsz, t, _ = q3.shape
    nb = t // WINDOW
    cur = lambda b, i: (b, i, 0)
    prev = lambda b, i: (b, jnp.maximum(i - 1, 0), 0)
    return pl.pallas_call(
        _attn_prompt_kernel,
        out_shape=jax.ShapeDtypeStruct((bsz, t, N_HEADS * HEAD_DIM), BF16),
        grid=(bsz, nb),
        in_specs=[
            pl.BlockSpec(memory_space=pltpu.SMEM),
            pl.BlockSpec((1, WINDOW, N_HEADS * HEAD_DIM), cur),
            pl.BlockSpec((1, WINDOW, KV_COLS), cur),
            pl.BlockSpec((1, WINDOW, KV_COLS), prev),
            pl.BlockSpec((1, WINDOW, KV_COLS), cur),
            pl.BlockSpec((1, WINDOW, KV_COLS), prev),
            pl.BlockSpec((2, N_HEADS, WINDOW, 2 * WINDOW), lambda b, i: (0, 0, 0, 0)),
        ],
        out_specs=pl.BlockSpec((1, WINDOW, N_HEADS * HEAD_DIM), cur),
        compiler_params=pltpu.CompilerParams(dimension_semantics=("parallel", "arbitrary")),
        name="swa_prompt",
    )(sinks, q3, k3, k3, v3, v3, bias)


def _attn_sample_kernel(q_ref, kn_ref, vn_ref, ck_ref, cv_ref, bias_ref, sink_ref,
                        o_ref, nk_ref, nv_ref, *, bb, t):
    keep = WINDOW - t

    def seq(s, carry):
        ck = ck_ref[s]
        cv = cv_ref[s]
        kn = kn_ref[s]
        vn = vn_ref[s]
        nk_ref[s, 0:keep, :] = ck[t:, :]
        nk_ref[s, keep:WINDOW, :] = kn
        nv_ref[s, 0:keep, :] = cv[t:, :]
        nv_ref[s, keep:WINDOW, :] = vn
        kcat = jnp.concatenate([ck, kn], axis=0).astype(BF16)
        vcat = jnp.concatenate([cv, vn], axis=0).astype(BF16)
        for kh in range(N_KV_HEADS):
            kv = slice(kh * HEAD_DIM, (kh + 1) * HEAD_DIM)
            sc = lax.dot_general(q_ref[s, kh], kcat[:, kv], (((1,), (1,)), ((), ())),
                                 preferred_element_type=F32)
            sc = sc + bias_ref[kh]
            o_ref[s, kh] = _softmax_sink_pv(sc, sink_ref[kh], vcat[:, kv]).astype(o_ref.dtype)
        return carry

    lax.fori_loop(0, bb, seq, 0)


def _attn_sample(q4, kn3, vn3, ck3, cv3, bias, sink_rows, bb):
    nseq, _, rows, _ = q4.shape
    t = kn3.shape[1]
    b3 = lambda i: (i, 0, 0)
    b4 = lambda i: (i, 0, 0, 0)
    return pl.pallas_call(
        functools.partial(_attn_sample_kernel, bb=bb, t=t),
        out_shape=[
            jax.ShapeDtypeStruct((nseq, N_KV_HEADS, rows, HEAD_DIM), BF16),
            jax.ShapeDtypeStruct((nseq, WINDOW, KV_COLS), F32),
            jax.ShapeDtypeStruct((nseq, WINDOW, KV_COLS), F32),
        ],
        grid=(nseq // bb,),
        in_specs=[
            pl.BlockSpec((bb, N_KV_HEADS, rows, HEAD_DIM), b4),
            pl.BlockSpec((bb, t, KV_COLS), b3),
            pl.BlockSpec((bb, t, KV_COLS), b3),
            pl.BlockSpec((bb, WINDOW, KV_COLS), b3),
            pl.BlockSpec((bb, WINDOW, KV_COLS), b3),
            pl.BlockSpec((N_KV_HEADS, rows, WINDOW + t), lambda i: (0, 0, 0)),
            pl.BlockSpec((N_KV_HEADS, rows, 1), lambda i: (0, 0, 0)),
        ],
        out_specs=[
            pl.BlockSpec((bb, N_KV_HEADS, rows, HEAD_DIM), b4),
            pl.BlockSpec((bb, WINDOW, KV_COLS), b3),
            pl.BlockSpec((bb, WINDOW, KV_COLS), b3),
        ],
        compiler_params=pltpu.CompilerParams(dimension_semantics=("parallel",)),
        name="swa_sample",
    )(q4, kn3, vn3, ck3, cv3, bias, sink_rows)


R_E1, R_E2, R_RANK1, R_RANK2, R_W1, R_W2 = range(6)


def _pack_bf16_pairs(x):
    c = x.shape[1] // 2
    bits = lax.bitcast_convert_type(x.astype(BF16).astype(F32), jnp.uint32)
    return (bits[:, :c] >> 16) | (bits[:, c:] & jnp.uint32(0xFFFF0000))


def _unpack_bf16_pairs(w):
    lo = lax.bitcast_convert_type(w << 16, F32)
    hi = lax.bitcast_convert_type(w & jnp.uint32(0xFFFF0000), F32)
    return jnp.concatenate([lo, hi], axis=1)


def _route(logits, run):
    rows = logits.shape[0]
    lane = lax.broadcasted_iota(jnp.int32, logits.shape, 1)
    big = jnp.int32(ROUTER_COLS)
    gl = jnp.where(lane < N_GROUPS, logits, -jnp.inf)
    gmax = jnp.max(gl, axis=-1, keepdims=True)
    p_grp = 1.0 / jnp.sum(jnp.exp(gl - gmax), axis=-1, keepdims=True)
    g_idx = jnp.min(jnp.where(gl == gmax, lane, big), axis=-1, keepdims=True)
    e_lo = N_GROUPS + EXPERTS_PER_GROUP * g_idx
    el = jnp.where((lane >= e_lo) & (lane < e_lo + EXPERTS_PER_GROUP), logits, -jnp.inf)
    m1 = jnp.max(el, axis=-1, keepdims=True)
    i1 = jnp.min(jnp.where(el == m1, lane, big), axis=-1, keepdims=True)
    el2 = jnp.where(lane == i1, -jnp.inf, el)
    m2 = jnp.max(el2, axis=-1, keepdims=True)
    i2 = jnp.min(jnp.where(el2 == m2, lane, big), axis=-1, keepdims=True)
    e21 = jnp.exp(m2 - m1)
    w1 = p_grp / (1.0 + e21)
    w2 = e21 * w1
    hit1 = lane == i1
    hit2 = lane == i2
    onehot = jnp.where(hit1 | hit2, 1.0, 0.0).astype(BF16)
    r_i = lax.broadcasted_iota(jnp.int32, (rows, rows), 0)
    c_i = lax.broadcasted_iota(jnp.int32, (rows, rows), 1)
    tril = jnp.where(c_i <= r_i, 1.0, 0.0).astype(BF16)
    upto = jnp.dot(tril, onehot, preferred_element_type=F32) + run
    rank1 = jnp.sum(jnp.where(hit1, upto, 0.0), axis=-1, keepdims=True) - 1.0
    rank2 = jnp.sum(jnp.where(hit2, upto, 0.0), axis=-1, keepdims=True) - 1.0
    rec = jnp.zeros(logits.shape, F32)
    for idx, val in ((R_E1, (i1 - N_GROUPS).astype(F32)), (R_E2, (i2 - N_GROUPS).astype(F32)),
                     (R_RANK1, rank1), (R_RANK2, rank2), (R_W1, w1), (R_W2, w2)):
        rec = jnp.where(lane == idx, val, rec)
    return rec, upto[rows - 1:rows, :]


def _merge_kernel(*refs, a_tiles):
    acts_a, acts_b = refs[0:5], refs[5:10]
    wlo_ref, wao_ref, wo_ref, g_ref, wrh_ref, wrl_ref, x1_ref, xn_ref, rec_ref, cnt_ref = refs[10:]

    @pl.when(pl.program_id(0) == 0)
    def _():
        cnt_ref[...] = jnp.zeros_like(cnt_ref)

    def tile(x_ref, yl_ref, ya_ref, gl_ref, ga_ref):
        a = jnp.dot(yl_ref[...], wlo_ref[...], preferred_element_type=F32)
        b = jnp.dot(ya_ref[...], wao_ref[...], preferred_element_type=F32)
        merged = gl_ref[...].astype(F32) * a + ga_ref[...].astype(F32) * b
        x1 = x_ref[...] + jnp.dot(merged.astype(BF16), wo_ref[...], preferred_element_type=F32)
        x1_ref[...] = x1
        xn = _rms_scale(x1) * g_ref[...]
        xh = xn.astype(BF16)
        xn_ref[...] = _pack_bf16_pairs(xn)
        xlo = (xn - xh.astype(F32)).astype(BF16)
        logits = (jnp.dot(xh, wrh_ref[...], preferred_element_type=F32)
                  + jnp.dot(xlo, wrh_ref[...], preferred_element_type=F32)
                  + jnp.dot(xh, wrl_ref[...], preferred_element_type=F32))
        rec, run = _route(logits, cnt_ref[...])
        rec_ref[...] = rec
        cnt_ref[...] = run

    pl.when(pl.program_id(0) < a_tiles)(lambda: tile(*acts_a))
    pl.when(pl.program_id(0) >= a_tiles)(lambda: tile(*acts_b))


def _merge(acts_a, acts_b, p, tm):
    n_a, n_b = acts_a[0].shape[0], acts_b[0].shape[0]
    n = n_a + n_b
    a_tiles = n_a // tm
    row = lambda i: (i, 0)
    fixed = lambda i: (0, 0)
    act_a = pl.BlockSpec((tm, D_MODEL), lambda i: (jnp.minimum(i, a_tiles - 1), 0))
    act_b = pl.BlockSpec((tm, D_MODEL), lambda i: (jnp.maximum(i - a_tiles, 0), 0))
    wsq = pl.BlockSpec((D_MODEL, D_MODEL), fixed)
    wr = pl.BlockSpec((D_MODEL, ROUTER_COLS), fixed)
    return pl.pallas_call(
        functools.partial(_merge_kernel, a_tiles=a_tiles),
        out_shape=[
            jax.ShapeDtypeStruct((n, D_MODEL), F32),
            jax.ShapeDtypeStruct((n, D_MODEL // 2), jnp.uint32),
            jax.ShapeDtypeStruct((n, ROUTER_COLS), F32),
            jax.ShapeDtypeStruct((1, ROUTER_COLS), F32),
        ],
        grid=(n // tm,),
        in_specs=[act_a] * 5 + [act_b] * 5 + [wsq, wsq, wsq, pl.BlockSpec((1, D_MODEL), fixed),
                                              wr, wr],
        out_specs=[pl.BlockSpec((tm, D_MODEL), row),
                   pl.BlockSpec((tm, D_MODEL // 2), row),
                   pl.BlockSpec((tm, ROUTER_COLS), row),
                   pl.BlockSpec((1, ROUTER_COLS), fixed)],
        compiler_params=pltpu.CompilerParams(dimension_semantics=("arbitrary",)),
        name="merge_router",
    )(*acts_a, *acts_b, p['wlo'], p['wao'], p['wo'], p['norm_moe_g'], p['wr_hi'], p['wr_lo'])


def _dispatch_kernel(pos_ref, xn_ref, xs_in_ref, xs_ref, sem):
    del xs_in_ref
    tm = xn_ref.shape[0]

    def issue(i, carry):
        for s in range(2):
            pltpu.make_async_copy(xn_ref.at[pl.ds(i, 1)],
                                  xs_ref.at[pl.ds(pos_ref[0, s, i], 1)], sem).start(priority=s)
        return carry

    lax.fori_loop(0, tm, issue, 0)
    for s in range(2):
        pltpu.make_async_copy(xn_ref, xs_ref.at[pl.ds(0, tm)], sem).wait()


def _dispatch(pos3, xn, n_rows, tm):
    n, c = xn.shape
    xs0 = jnp.zeros((n_rows, c), xn.dtype)
    return pl.pallas_call(
        _dispatch_kernel,
        out_shape=jax.ShapeDtypeStruct((n_rows, c), xn.dtype),
        grid=(n // tm,),
        in_specs=[
            pl.BlockSpec((1, 2, tm), lambda i: (i, 0, 0), memory_space=pltpu.SMEM),
            pl.BlockSpec((tm, c), lambda i: (i, 0)),
            pl.BlockSpec(memory_space=pl.ANY),
        ],
        out_specs=pl.BlockSpec(memory_space=pl.ANY),
        scratch_shapes=[pltpu.SemaphoreType.DMA],
        input_output_aliases={2: 0},
        compiler_params=pltpu.CompilerParams(dimension_semantics=("arbitrary",)),
        name="moe_dispatch",
    )(pos3, xn, xs0)


def _experts_kernel(te_ref, nu_ref, xs_ref, wgu_ref, wd_ref, ys_ref):
    del te_ref

    @pl.when(pl.program_id(0) < nu_ref[0])
    def _():
        x = _unpack_bf16_pairs(xs_ref[...]).astype(BF16)
        gu = jnp.dot(x, wgu_ref[0], preferred_element_type=F32)
        gate = gu[:, :D_EXPERT]
        h = gate * _sigmoid(gate) * gu[:, D_EXPERT:]
        ys_ref[...] = _pack_bf16_pairs(jnp.dot(h.astype(BF16), wd_ref[0], preferred_element_type=F32))

    @pl.when(pl.program_id(0) >= nu_ref[0])
    def _():
        ys_ref[...] = jnp.zeros_like(ys_ref)


def _experts(tile_expert, n_used, xs, wgu, wd, te_rows):
    n_rows, c = xs.shape
    last = lambda t, te, nu: jnp.minimum(t, nu[0] - 1)
    return pl.pallas_call(
        _experts_kernel,
        out_shape=jax.ShapeDtypeStruct((n_rows, c), xs.dtype),
        grid_spec=pltpu.PrefetchScalarGridSpec(
            num_scalar_prefetch=2,
            grid=(n_rows // te_rows,),
            in_specs=[
                pl.BlockSpec((te_rows, c), lambda t, te, nu: (last(t, te, nu), 0)),
                pl.BlockSpec((1, D_MODEL, 2 * D_EXPERT), lambda t, te, nu: (te[last(t, te, nu)], 0, 0)),
                pl.BlockSpec((1, D_EXPERT, D_MODEL), lambda t, te, nu: (te[last(t, te, nu)], 0, 0)),
            ],
            out_specs=pl.BlockSpec((te_rows, c), lambda t, te, nu: (t, 0)),
        ),
        compiler_params=pltpu.CompilerParams(dimension_semantics=("arbitrary",)),
        name="moe_experts",
    )(tile_expert, n_used, xs, wgu, wd)


def _combine_kernel(pos_ref, ys_ref, x1_ref, rec_ref, g_ref, ya_ref, yb_ref, buf_ref, sem, *, a_tiles):
    tm = x1_ref.shape[0]

    def issue(i, carry):
        for s in range(2):
            pltpu.make_async_copy(ys_ref.at[pl.ds(pos_ref[0, s, i], 1)],
                                  buf_ref.at[s, pl.ds(i, 1)], sem).start(priority=s)
        return carry

    lax.fori_loop(0, tm, issue, 0)
    for s in range(2):
        pltpu.make_async_copy(ys_ref.at[pl.ds(0, tm)], buf_ref.at[s], sem).wait()
    rec = rec_ref[...]
    w1 = rec[:, R_W1:R_W1 + 1]
    w2 = rec[:, R_W2:R_W2 + 1]
    moe = w1 * _unpack_bf16_pairs(buf_ref[0]) + w2 * _unpack_bf16_pairs(buf_ref[1])
    y = _rms_scale(x1_ref[...] + moe) * g_ref[...]

    @pl.when(pl.program_id(0) < a_tiles)
    def _():
        ya_ref[...] = y

    @pl.when(pl.program_id(0) >= a_tiles)
    def _():
        yb_ref[...] = y


def _combine(pos3, ys, x1, rec, g, n_a, tm):
    n = x1.shape[0]
    c = ys.shape[1]
    a_tiles = n_a // tm
    row = lambda i: (i, 0)
    return pl.pallas_call(
        functools.partial(_combine_kernel, a_tiles=a_tiles),
        out_shape=[jax.ShapeDtypeStruct((n_a, D_MODEL), F32),
                   jax.ShapeDtypeStruct((n - n_a, D_MODEL), F32)],
        grid=(n // tm,),
        in_specs=[
            pl.BlockSpec((1, 2, tm), lambda i: (i, 0, 0), memory_space=pltpu.SMEM),
            pl.BlockSpec(memory_space=pl.ANY),
            pl.BlockSpec((tm, D_MODEL), row),
            pl.BlockSpec((tm, ROUTER_COLS), row),
            pl.BlockSpec((1, D_MODEL), lambda i: (0, 0)),
        ],
        out_specs=[pl.BlockSpec((tm, D_MODEL), lambda i: (jnp.minimum(i, a_tiles - 1), 0)),
                   pl.BlockSpec((tm, D_MODEL), lambda i: (jnp.maximum(i - a_tiles, 0), 0))],
        scratch_shapes=[pltpu.VMEM((2, tm, c), ys.dtype), pltpu.SemaphoreType.DMA],
        compiler_params=pltpu.CompilerParams(dimension_semantics=("arbitrary",)),
        name="moe_combine",
    )(pos3, ys, x1, rec, g)


def _moe_plan(rec, counts, te_rows):
    n = rec.shape[0]
    cnt = counts[0, N_GROUPS:N_GROUPS + N_EXPERTS].astype(jnp.int32)
    tiles = (cnt + te_rows - 1) // te_rows
    tile_end = jnp.cumsum(tiles)
    row_off = (tile_end - tiles) * te_rows
    e = rec[:, R_E1:R_E2 + 1].astype(jnp.int32)
    rank = rec[:, R_RANK1:R_RANK2 + 1].astype(jnp.int32)
    pos = jnp.take(row_off, e) + rank
    by_tile = lambda tm: pos.reshape(n // tm, tm, 2).transpose(0, 2, 1)
    n_tiles = (2 * n + te_rows - 1) // te_rows + N_EXPERTS
    tile_expert = jnp.minimum(
        jnp.sum(jnp.arange(n_tiles)[:, None] >= tile_end[None, :], axis=1), N_EXPERTS - 1)
    return by_tile, tile_expert.astype(jnp.int32), tile_end[-1:].astype(jnp.int32), n_tiles * te_rows


def _row_tile(n, want):
    tm = min(n, want)
    while n % tm or tm % SUBLANES:
        tm -= 1
    return tm


def _alibi_slopes():
    return jnp.exp2(-8.0 * jnp.arange(1, N_HEADS + 1, dtype=F32) / N_HEADS)


def _prompt_bias():
    qi = jnp.arange(WINDOW)[:, None]
    kj = jnp.arange(2 * WINDOW)[None, :]
    dist = WINDOW + qi - kj
    in_window = (dist >= 0) & (dist < WINDOW)
    valid = jnp.stack([in_window & (kj >= WINDOW), in_window])
    score = -_alibi_slopes()[None, :, None, None] * dist.astype(F32)[None, None]
    return jnp.where(valid[:, None], score, NEG_INF)


def _sample_bias(t):
    q_pos = jnp.tile(jnp.arange(t), GQA)[:, None]
    k_pos = (jnp.arange(WINDOW + t) - WINDOW)[None, :]
    dist = q_pos - k_pos
    valid = (dist >= 0) & (dist < WINDOW)
    slopes = jnp.repeat(_alibi_slopes().reshape(N_KV_HEADS, GQA), t, axis=1)
    score = -slopes[:, :, None] * dist.astype(F32)[None]
    return jnp.where(valid[None], score, NEG_INF)


def _layer_params(norm_mix_g, w_in, conv_w, conv_b, lru_wa, lru_ba, lru_wx, lru_bx, lru_lambda,
                  attn_sinks, w_lru_out, w_attn_out, w_o, norm_moe_g, w_router_group,
                  w_router_expert, moe_w_gate, moe_w_up, moe_w_down):
    per_pack = GATE_PACK // LRU_BLOCK_DIM

    def pack_diag(w):
        w4 = w.reshape(LRU_WIDTH // GATE_PACK, per_pack, LRU_BLOCK_DIM, LRU_BLOCK_DIM)
        eye = jnp.eye(per_pack, dtype=w.dtype)
        return jnp.einsum('pbij,bc->pbicj', w4, eye).reshape(-1, GATE_PACK, GATE_PACK)

    wbd = jnp.concatenate([pack_diag(lru_wa), pack_diag(lru_wx)], axis=-1).astype(BF16)
    w_router = jnp.concatenate(
        [w_router_group, jnp.transpose(w_router_expert, (1, 0, 2)).reshape(D_MODEL, N_EXPERTS)],
        axis=1)
    w_router = jnp.pad(w_router, ((0, 0), (0, ROUTER_COLS - w_router.shape[1])))
    wr_hi = w_router.astype(BF16)
    wr_lo = (w_router - wr_hi.astype(F32)).astype(BF16)
    wgu = jnp.concatenate([moe_w_gate, moe_w_up], axis=-1).reshape(
        N_EXPERTS, D_MODEL, 2 * D_EXPERT).astype(BF16)
    wd = moe_w_down.reshape(N_EXPERTS, D_EXPERT, D_MODEL).astype(BF16)
    row = lambda v: v.reshape(1, -1)
    return dict(
        norm_mix_g=row(norm_mix_g), w_in=w_in.astype(BF16), conv_w=conv_w, conv_b=row(conv_b),
        wbd=wbd, ba=row(lru_ba), bx=row(lru_bx), lam=row(lru_lambda), sinks=attn_sinks,
        wlo=w_lru_out.astype(BF16), wao=w_attn_out.astype(BF16), wo=w_o.astype(BF16),
        norm_moe_g=row(norm_moe_g), wr_hi=wr_hi, wr_lo=wr_lo, wgu=wgu, wd=wd)


def _prompt_mixers(x, p):
    bsz, t, _ = x.shape
    assert t % WINDOW == 0 and t >= CONV_W - 1
    n = bsz * t
    x2 = x.reshape(n, D_MODEL)
    xl, q, k, v, gl, ga = _inproj(x2, p['norm_mix_g'], p['w_in'], _row_tile(n, 512))
    tc = _row_tile(t, 256)
    yl, h_last = _lru(xl, jnp.zeros((bsz, SUBLANES, LRU_WIDTH), F32),
                      jnp.zeros((bsz, 1, LRU_WIDTH), F32), p['conv_w'], p['conv_b'], p['wbd'],
                      p['ba'], p['bx'], p['lam'], nseq=bsz, seqlen=t, bb=1, tc=tc)
    k3 = k.reshape(bsz, t, KV_COLS)
    v3 = v.reshape(bsz, t, KV_COLS)
    ya = _attn_prompt(q.reshape(bsz, t, -1), k3, v3, p['sinks'], _prompt_bias())
    new_conv = xl.reshape(bsz, t, LRU_WIDTH)[:, t - (CONV_W - 1):]
    new_k = k3[:, t - WINDOW:].reshape(bsz, WINDOW, N_KV_HEADS, HEAD_DIM)
    new_v = v3[:, t - WINDOW:].reshape(bsz, WINDOW, N_KV_HEADS, HEAD_DIM)
    return ((x2, yl, ya.reshape(n, -1), gl, ga),
            (new_conv, h_last.reshape(bsz, LRU_WIDTH), new_k, new_v))


def _sample_mixers(x, conv_buf, h0, k_buf, v_buf, p):
    bsz, t, _ = x.shape
    assert t % SUBLANES == 0 and CONV_W - 1 <= t <= WINDOW
    n = bsz * t
    x2 = x.reshape(n, D_MODEL)
    xl, q, k, v, gl, ga = _inproj(x2, p['norm_mix_g'], p['w_in'], _row_tile(n, 512))
    bb = _row_tile(bsz, 16)
    prev8 = jnp.pad(conv_buf, ((0, 0), (SUBLANES - (CONV_W - 1), 0), (0, 0)))
    yl, h_last = _lru(xl, prev8, h0.reshape(bsz, 1, LRU_WIDTH), p['conv_w'], p['conv_b'],
                      p['wbd'], p['ba'], p['bx'], p['lam'], nseq=bsz, seqlen=t, bb=bb, tc=t)
    q4 = q.reshape(bsz, t, N_KV_HEADS, GQA, HEAD_DIM).transpose(0, 2, 3, 1, 4).reshape(
        bsz, N_KV_HEADS, GQA * t, HEAD_DIM)
    sink_rows = jnp.repeat(p['sinks'].reshape(N_KV_HEADS, GQA), t, axis=1)[:, :, None]
    o4, new_k, new_v = _attn_sample(
        q4, k.reshape(bsz, t, KV_COLS), v.reshape(bsz, t, KV_COLS),
        k_buf.reshape(bsz, WINDOW, KV_COLS), v_buf.reshape(bsz, WINDOW, KV_COLS),
        _sample_bias(t), sink_rows, _row_tile(bsz, 8))
    ya = o4.reshape(bsz, N_KV_HEADS, GQA, t, HEAD_DIM).transpose(0, 3, 1, 2, 4).reshape(n, -1)
    new_conv = xl.reshape(bsz, t, LRU_WIDTH)[:, t - (CONV_W - 1):]
    shape = (bsz, WINDOW, N_KV_HEADS, HEAD_DIM)
    return ((x2, yl, ya, gl, ga),
            (new_conv, h_last.reshape(bsz, LRU_WIDTH), new_k.reshape(shape), new_v.reshape(shape)))


def _merge_and_moe(acts_a, acts_b, p, norm_final_g):
    n_a, n_b = acts_a[0].shape[0], acts_b[0].shape[0]
    n = n_a + n_b
    tm = _row_tile(math.gcd(n_a, n_b), MERGE_ROWS)
    x1, xn, rec, cnt = _merge(acts_a, acts_b, p, tm)
    by_tile, tile_expert, n_used, n_rows = _moe_plan(rec, cnt, EXPERT_ROWS)
    tm_d = _row_tile(n, DISPATCH_ROWS)
    xs = _dispatch(by_tile(tm_d), xn, n_rows, tm_d)
    ys = _experts(tile_expert, n_used, xs, p['wgu'], p['wd'], EXPERT_ROWS)
    return _combine(by_tile(tm), ys, x1, rec, norm_final_g.reshape(1, -1), n_a, tm)


def kernel(x_prompt, x_sample, state_conv, state_lru_h, cache_win_k, cache_win_v, norm_mix_g, w_in, conv_w, conv_b, lru_wa, lru_ba, lru_wx, lru_bx, lru_lambda, attn_sinks, w_lru_out, w_attn_out, w_o, norm_moe_g, w_router_group, w_router_expert, moe_w_gate, moe_w_up, moe_w_down, norm_final_g):
    depth = w_in.shape[0]
    assert depth == 1, "the final norm is fused into the single layer's MoE kernel"
    p = _layer_params(norm_mix_g[0], w_in[0], conv_w[0], conv_b[0], lru_wa[0], lru_ba[0],
                      lru_wx[0], lru_bx[0], lru_lambda[0], attn_sinks[0], w_lru_out[0],
                      w_attn_out[0], w_o[0], norm_moe_g[0], w_router_group[0],
                      w_router_expert[0], moe_w_gate[0], moe_w_up[0], moe_w_down[0])
    acts_p, (c1, h1, k1, v1) = _prompt_mixers(x_prompt, p)
    acts_s, (c2, h2, k2, v2) = _sample_mixers(x_sample, state_conv[0], state_lru_h[0],
                                              cache_win_k[0], cache_win_v[0], p)
    yp, ys = _merge_and_moe(acts_p, acts_s, p, norm_final_g)
    return (yp.reshape(x_prompt.shape), ys.reshape(x_sample.shape), c1[None], h1[None], k1[None],
            v1[None], c2[None], h2[None], k2[None], v2[None])
```

```python
import functools
import math

import jax
import jax.numpy as jnp
from jax import lax
from jax.experimental import pallas as pl
from jax.experimental.pallas import tpu as pltpu

D_MODEL = 1024
LRU_WIDTH = D_MODEL
LRU_BLOCKS = 16
LRU_BLOCK_DIM = LRU_WIDTH // LRU_BLOCKS
LRU_C = 8.0
CONV_W = 4
N_HEADS = 16
N_KV_HEADS = 4
GQA = N_HEADS // N_KV_HEADS
HEAD_DIM = D_MODEL // N_HEADS
KV_COLS = N_KV_HEADS * HEAD_DIM
WINDOW = 128
N_GROUPS = 4
EXPERTS_PER_GROUP = 8
N_EXPERTS = N_GROUPS * EXPERTS_PER_GROUP
D_EXPERT = D_MODEL // 4
RMS_EPS = 1e-6
NEG_INF = -1e30
IN_COLS = LRU_WIDTH + (N_HEADS + 2 * N_KV_HEADS) * HEAD_DIM + 2 * D_MODEL

SUBLANES = 8
LANES = 128
ROUTER_COLS = LANES
GATE_PACK = 256

MERGE_ROWS = 512
DISPATCH_ROWS = 1024
EXPERT_ROWS = 256
ISSUE_UNROLL = 8
SAMPLE_SEQ_UNROLL = 4

F32 = jnp.float32
BF16 = jnp.bfloat16


def _sigmoid(z):
    return 1.0 / (1.0 + jnp.exp(-z))


def _expm1(z):
    u = jnp.exp(z)
    return jnp.where(u == 1.0, z, jnp.where(u == 0.0, -1.0, (u - 1.0) * z / jnp.log(u)))


def _rms_scale(x):
    return x * lax.rsqrt(jnp.mean(x * x, axis=-1, keepdims=True) + RMS_EPS)


def _inproj_kernel(x_ref, g_ref, w_ref, xl_ref, q_ref, k_ref, v_ref, gl_ref, ga_ref):
    xb = (_rms_scale(x_ref[...]) * g_ref[...]).astype(BF16)

    def proj(c0, c1):
        return jnp.dot(xb, w_ref[:, c0:c1], preferred_element_type=F32)

    c0 = LRU_WIDTH
    c1 = c0 + N_HEADS * HEAD_DIM
    c2 = c1 + KV_COLS
    c3 = c2 + KV_COLS
    c4 = c3 + D_MODEL
    xl_ref[...] = proj(0, c0)
    q_ref[...] = (proj(c0, c1) * (HEAD_DIM ** -0.5)).astype(BF16)
    k_ref[...] = proj(c1, c2)
    v_ref[...] = proj(c2, c3)
    gl_ref[...] = _sigmoid(proj(c3, c4)).astype(BF16)
    ga_ref[...] = _sigmoid(proj(c4, IN_COLS)).astype(BF16)


def _inproj(x2, g, w_bf, tm):
    n = x2.shape[0]
    row = lambda i: (i, 0)
    fixed = lambda i: (0, 0)
    outs = [
        jax.ShapeDtypeStruct((n, LRU_WIDTH), F32),
        jax.ShapeDtypeStruct((n, N_HEADS * HEAD_DIM), BF16),
        jax.ShapeDtypeStruct((n, KV_COLS), F32),
        jax.ShapeDtypeStruct((n, KV_COLS), F32),
        jax.ShapeDtypeStruct((n, D_MODEL), BF16),
        jax.ShapeDtypeStruct((n, D_MODEL), BF16),
    ]
    return pl.pallas_call(
        _inproj_kernel,
        out_shape=outs,
        grid=(n // tm,),
        in_specs=[
            pl.BlockSpec((tm, D_MODEL), row),
            pl.BlockSpec((1, D_MODEL), fixed),
            pl.BlockSpec((D_MODEL, IN_COLS), fixed),
        ],
        out_specs=[pl.BlockSpec((tm, o.shape[1]), row) for o in outs],
        compiler_params=pltpu.CompilerParams(dimension_semantics=("parallel",)),
        name="inproj",
    )(x2, g, w_bf)


def _lru_kernel(xl_ref, prev_ref, h0_ref, cw_ref, cb_ref, wbd_ref, ba_ref, bx_ref, lam_ref,
                y_ref, hl_ref, xs_ref, xc_ref, a_ref, b_ref, hc_ref, *, bb, tc):
    t = pl.program_id(1)

    @pl.when(t == 0)
    def _():
        xs_ref[:, 0:SUBLANES, :] = prev_ref[...]
        hc_ref[...] = h0_ref[...]

    @pl.when(t > 0)
    def _():
        xs_ref[:, 0:SUBLANES, :] = xs_ref[:, tc:tc + SUBLANES, :]

    for s in range(bb):
        xs_ref[s, SUBLANES:SUBLANES + tc, :] = xl_ref[s * tc:(s + 1) * tc, :]
    cw = cw_ref[...]
    for s in range(bb):
        acc = cb_ref[...] + xs_ref[s, SUBLANES:SUBLANES + tc, :] * cw[CONV_W - 1:CONV_W, :]
        for j in range(CONV_W - 1):
            off = SUBLANES - (CONV_W - 1) + j
            acc = acc + xs_ref[s, off:off + tc, :] * cw[j:j + 1, :]
        xc_ref[s * tc:(s + 1) * tc, :] = acc

    lam = lam_ref[...]
    softplus_neg_lam = jnp.maximum(-lam, 0.0) + jnp.log1p(jnp.exp(-jnp.abs(lam)))
    for g in range(LRU_WIDTH // GATE_PACK):
        cols = slice(g * GATE_PACK, (g + 1) * GATE_PACK)
        xc = xc_ref[:, cols]
        gates = jnp.dot(xc.astype(BF16), wbd_ref[g], preferred_element_type=F32)
        r = _sigmoid(gates[:, :GATE_PACK] + ba_ref[:, cols])
        ig = _sigmoid(gates[:, GATE_PACK:] + bx_ref[:, cols])
        log_a = (-LRU_C) * r * softplus_neg_lam[:, cols]
        a_ref[:, cols] = jnp.exp(log_a)
        b_ref[:, cols] = jnp.sqrt(-_expm1(2.0 * log_a)) * (ig * xc)

    row = lax.broadcasted_iota(jnp.int32, (SUBLANES, LRU_WIDTH), 0)
    for s in range(bb):
        def group(gi, h8, s=s):
            r0 = pl.multiple_of(s * tc + gi * SUBLANES, SUBLANES)
            a = a_ref[pl.ds(r0, SUBLANES), :]
            b = b_ref[pl.ds(r0, SUBLANES), :]
            for d in (1, 2, 4):
                a_up = jnp.where(row >= d, pltpu.roll(a, d, 0), 1.0)
                b_up = jnp.where(row >= d, pltpu.roll(b, d, 0), 0.0)
                b = a * b_up + b
                a = a * a_up
            h = a * h8 + b
            b_ref[pl.ds(r0, SUBLANES), :] = h
            return jnp.broadcast_to(h[SUBLANES - 1:SUBLANES, :], (SUBLANES, LRU_WIDTH))

        h8 = lax.fori_loop(0, tc // SUBLANES, group,
                           jnp.broadcast_to(hc_ref[s], (SUBLANES, LRU_WIDTH)))
        hc_ref[s] = h8[0:1, :]

    y_ref[...] = b_ref[...].astype(y_ref.dtype)

    @pl.when(t == pl.num_programs(1) - 1)
    def _():
        hl_ref[...] = hc_ref[...]


def _lru(xl2, prev8, h0, cw, cb, wbd, ba, bx, lam, *, nseq, seqlen, bb, tc):
    nt = seqlen // tc
    assert bb == 1 or nt == 1
    rows = bb * tc
    fixed2 = lambda b, t: (0, 0)
    fixed3 = lambda b, t: (0, 0, 0)
    return pl.pallas_call(
        functools.partial(_lru_kernel, bb=bb, tc=tc),
        out_shape=[
            jax.ShapeDtypeStruct((nseq * seqlen, LRU_WIDTH), BF16),
            jax.ShapeDtypeStruct((nseq, 1, LRU_WIDTH), F32),
        ],
        grid=(nseq // bb, nt),
        in_specs=[
            pl.BlockSpec((rows, LRU_WIDTH), lambda b, t: (b * nt + t, 0)),
            pl.BlockSpec((bb, SUBLANES, LRU_WIDTH), lambda b, t: (b, 0, 0)),
            pl.BlockSpec((bb, 1, LRU_WIDTH), lambda b, t: (b, 0, 0)),
            pl.BlockSpec((CONV_W, LRU_WIDTH), fixed2),
            pl.BlockSpec((1, LRU_WIDTH), fixed2),
            pl.BlockSpec((LRU_WIDTH // GATE_PACK, GATE_PACK, 2 * GATE_PACK), fixed3),
            pl.BlockSpec((1, LRU_WIDTH), fixed2),
            pl.BlockSpec((1, LRU_WIDTH), fixed2),
            pl.BlockSpec((1, LRU_WIDTH), fixed2),
        ],
        out_specs=[
            pl.BlockSpec((rows, LRU_WIDTH), lambda b, t: (b * nt + t, 0)),
            pl.BlockSpec((bb, 1, LRU_WIDTH), lambda b, t: (b, 0, 0)),
        ],
        scratch_shapes=[
            pltpu.VMEM((bb, tc + SUBLANES, LRU_WIDTH), F32),
            pltpu.VMEM((rows, LRU_WIDTH), F32),
            pltpu.VMEM((rows, LRU_WIDTH), F32),
            pltpu.VMEM((rows, LRU_WIDTH), F32),
            pltpu.VMEM((bb, 1, LRU_WIDTH), F32),
        ],
        compiler_params=pltpu.CompilerParams(dimension_semantics=("parallel", "arbitrary")),
        name="conv_rglru",
    )(xl2, prev8, h0, cw, cb, wbd, ba, bx, lam)


def _softmax_sink_pv(s, sink, v_bf):
    m = jnp.maximum(jnp.max(s, axis=-1, keepdims=True), sink)
    p = jnp.exp(s - m)
    denom = jnp.sum(p, axis=-1, keepdims=True) + jnp.exp(sink - m)
    o = jnp.dot(p.astype(BF16), v_bf, preferred_element_type=F32)
    return o / denom


def _attn_prompt_kernel(sink_ref, q_ref, kc_ref, kp_ref, vc_ref, vp_ref, bias_ref, o_ref):
    bsel = jnp.where(pl.program_id(1) == 0, 0, 1)
    kcat = jnp.concatenate([kp_ref[0], kc_ref[0]], axis=0).astype(BF16)
    vcat = jnp.concatenate([vp_ref[0], vc_ref[0]], axis=0).astype(BF16)
    q = q_ref[0]
    outs = []
    for h in range(N_HEADS):
        kv = slice((h // GQA) * HEAD_DIM, (h // GQA + 1) * HEAD_DIM)
        s = lax.dot_general(q[:, h * HEAD_DIM:(h + 1) * HEAD_DIM], kcat[:, kv],
                            (((1,), (1,)), ((), ())), preferred_element_type=F32)
        s = s + bias_ref[bsel, h]
        outs.append(_softmax_sink_pv(s, sink_ref[h], vcat[:, kv]))
    o_ref[0] = jnp.concatenate(outs, axis=-1).astype(o_ref.dtype)


def _attn_prompt(q3, k3, v3, sinks, bias):
    bsz, t, _ = q3.shape
    nb = t // WINDOW
    cur = lambda b, i: (b, i, 0)
    prev = lambda b, i: (b, jnp.maximum(i - 1, 0), 0)
    return pl.pallas_call(
        _attn_prompt_kernel,
        out_shape=jax.ShapeDtypeStruct((bsz, t, N_HEADS * HEAD_DIM), BF16),
        grid=(bsz, nb),
        in_specs=[
            pl.BlockSpec(memory_space=pltpu.SMEM),
            pl.BlockSpec((1, WINDOW, N_HEADS * HEAD_DIM), cur),
            pl.BlockSpec((1, WINDOW, KV_COLS), cur),
            pl.BlockSpec((1, WINDOW, KV_COLS), prev),
            pl.BlockSpec((1, WINDOW, KV_COLS), cur),
            pl.BlockSpec((1, WINDOW, KV_COLS), prev),
            pl.BlockSpec((2, N_HEADS, WINDOW, 2 * WINDOW), lambda b, i: (0, 0, 0, 0)),
        ],
        out_specs=pl.BlockSpec((1, WINDOW, N_HEADS * HEAD_DIM), cur),
        compiler_params=pltpu.CompilerParams(dimension_semantics=("parallel", "arbitrary")),
        name="swa_prompt",
    )(sinks, q3, k3, k3, v3, v3, bias)


def _attn_sample_kernel(q_ref, kn_ref, vn_ref, ck_ref, cv_ref, bias_ref, sink_ref,
                        o_ref, nk_ref, nv_ref, *, bb, t):
    keep = WINDOW - t

    def one_seq(s):
        ck = ck_ref[s]
        cv = cv_ref[s]
        kn = kn_ref[s]
        vn = vn_ref[s]
        nk_ref[s, 0:keep, :] = ck[t:, :]
        nk_ref[s, keep:WINDOW, :] = kn
        nv_ref[s, 0:keep, :] = cv[t:, :]
        nv_ref[s, keep:WINDOW, :] = vn
        kcat = jnp.concatenate([ck, kn], axis=0).astype(BF16)
        vcat = jnp.concatenate([cv, vn], axis=0).astype(BF16)
        for kh in range(N_KV_HEADS):
            kv = slice(kh * HEAD_DIM, (kh + 1) * HEAD_DIM)
            sc = lax.dot_general(q_ref[s, kh], kcat[:, kv], (((1,), (1,)), ((), ())),
                                 preferred_element_type=F32)
            sc = sc + bias_ref[kh]
            o_ref[s, kh] = _softmax_sink_pv(sc, sink_ref[kh], vcat[:, kv]).astype(o_ref.dtype)

    def seqs(j, carry):
        for u in range(SAMPLE_SEQ_UNROLL):
            one_seq(j * SAMPLE_SEQ_UNROLL + u)
        return carry

    lax.fori_loop(0, bb // SAMPLE_SEQ_UNROLL, seqs, 0)


def _attn_sample(q4, kn3, vn3, ck3, cv3, bias, sink_rows, bb):
    nseq, _, rows, _ = q4.shape
    t = kn3.shape[1]
    b3 = lambda i: (i, 0, 0)
    b4 = lambda i: (i, 0, 0, 0)
    return pl.pallas_call(
        functools.partial(_attn_sample_kernel, bb=bb, t=t),
        out_shape=[
            jax.ShapeDtypeStruct((nseq, N_KV_HEADS, rows, HEAD_DIM), BF16),
            jax.ShapeDtypeStruct((nseq, WINDOW, KV_COLS), F32),
            jax.ShapeDtypeStruct((nseq, WINDOW, KV_COLS), F32),
        ],
        grid=(nseq // bb,),
        in_specs=[
            pl.BlockSpec((bb, N_KV_HEADS, rows, HEAD_DIM), b4),
            pl.BlockSpec((bb, t, KV_COLS), b3),
            pl.BlockSpec((bb, t, KV_COLS), b3),
            pl.BlockSpec((bb, WINDOW, KV_COLS), b3),
            pl.BlockSpec((bb, WINDOW, KV_COLS), b3),
            pl.BlockSpec((N_KV_HEADS, rows, WINDOW + t), lambda i: (0, 0, 0)),
            pl.BlockSpec((N_KV_HEADS, rows, 1), lambda i: (0, 0, 0)),
        ],
        out_specs=[
            pl.BlockSpec((bb, N_KV_HEADS, rows, HEAD_DIM), b4),
            pl.BlockSpec((bb, WINDOW, KV_COLS), b3),
            pl.BlockSpec((bb, WINDOW, KV_COLS), b3),
        ],
        compiler_params=pltpu.CompilerParams(dimension_semantics=("parallel",)),
        name="swa_sample",
    )(q4, kn3, vn3, ck3, cv3, bias, sink_rows)


R_E1, R_E2, R_RANK1, R_RANK2, R_W1, R_W2 = range(6)


def _route(logits, run):
    rows = logits.shape[0]
    lane = lax.broadcasted_iota(jnp.int32, logits.shape, 1)
    big = jnp.int32(ROUTER_COLS)
    gl = jnp.where(lane < N_GROUPS, logits, -jnp.inf)
    gmax = jnp.max(gl, axis=-1, keepdims=True)
    p_grp = 1.0 / jnp.sum(jnp.exp(gl - gmax), axis=-1, keepdims=True)
    g_idx = jnp.min(jnp.where(gl == gmax, lane, big), axis=-1, keepdims=True)
    e_lo = N_GROUPS + EXPERTS_PER_GROUP * g_idx
    el = jnp.where((lane >= e_lo) & (lane < e_lo + EXPERTS_PER_GROUP), logits, -jnp.inf)
    m1 = jnp.max(el, axis=-1, keepdims=True)
    i1 = jnp.min(jnp.where(el == m1, lane, big), axis=-1, keepdims=True)
    el2 = jnp.where(lane == i1, -jnp.inf, el)
    m2 = jnp.max(el2, axis=-1, keepdims=True)
    i2 = jnp.min(jnp.where(el2 == m2, lane, big), axis=-1, keepdims=True)
    e21 = jnp.exp(m2 - m1)
    w1 = p_grp / (1.0 + e21)
    w2 = e21 * w1
    hit1 = lane == i1
    hit2 = lane == i2
    onehot = jnp.where(hit1 | hit2, 1.0, 0.0).astype(BF16)
    r_i = lax.broadcasted_iota(jnp.int32, (rows, rows), 0)
    c_i = lax.broadcasted_iota(jnp.int32, (rows, rows), 1)
    tril = jnp.where(c_i <= r_i, 1.0, 0.0).astype(BF16)
    upto = jnp.dot(tril, onehot, preferred_element_type=F32) + run
    rank1 = jnp.sum(jnp.where(hit1, upto, 0.0), axis=-1, keepdims=True) - 1.0
    rank2 = jnp.sum(jnp.where(hit2, upto, 0.0), axis=-1, keepdims=True) - 1.0
    rec = jnp.zeros(logits.shape, F32)
    for idx, val in ((R_E1, (i1 - N_GROUPS).astype(F32)), (R_E2, (i2 - N_GROUPS).astype(F32)),
                     (R_RANK1, rank1), (R_RANK2, rank2), (R_W1, w1), (R_W2, w2)):
        rec = jnp.where(lane == idx, val, rec)
    return rec, upto[rows - 1:rows, :]


def _merge_kernel(*refs, a_tiles):
    acts_a, acts_b = refs[0:5], refs[5:10]
    wlo_ref, wao_ref, wo_ref, g_ref, wrh_ref, wrl_ref, x1_ref, xn_ref, rec_ref, cnt_ref = refs[10:]

    @pl.when(pl.program_id(0) == 0)
    def _():
        cnt_ref[...] = jnp.zeros_like(cnt_ref)

    def tile(x_ref, yl_ref, ya_ref, gl_ref, ga_ref):
        a = jnp.dot(yl_ref[...], wlo_ref[...], preferred_element_type=F32)
        b = jnp.dot(ya_ref[...], wao_ref[...], preferred_element_type=F32)
        merged = gl_ref[...].astype(F32) * a + ga_ref[...].astype(F32) * b
        x1 = x_ref[...] + jnp.dot(merged.astype(BF16), wo_ref[...], preferred_element_type=F32)
        x1_ref[...] = x1
        xn = _rms_scale(x1) * g_ref[...]
        xh = xn.astype(BF16)
        xn_ref[...] = xn
        xlo = (xn - xh.astype(F32)).astype(BF16)
        logits = (jnp.dot(xh, wrh_ref[...], preferred_element_type=F32)
                  + jnp.dot(xlo, wrh_ref[...], preferred_element_type=F32)
                  + jnp.dot(xh, wrl_ref[...], preferred_element_type=F32))
        rec, run = _route(logits, cnt_ref[...])
        rec_ref[...] = rec
        cnt_ref[...] = run

    pl.when(pl.program_id(0) < a_tiles)(lambda: tile(*acts_a))
    pl.when(pl.program_id(0) >= a_tiles)(lambda: tile(*acts_b))


def _merge(acts_a, acts_b, p, tm):
    n_a, n_b = acts_a[0].shape[0], acts_b[0].shape[0]
    n = n_a + n_b
    a_tiles = n_a // tm
    row = lambda i: (i, 0)
    fixed = lambda i: (0, 0)
    act_a = pl.BlockSpec((tm, D_MODEL), lambda i: (jnp.minimum(i, a_tiles - 1), 0))
    act_b = pl.BlockSpec((tm, D_MODEL), lambda i: (jnp.maximum(i - a_tiles, 0), 0))
    wsq = pl.BlockSpec((D_MODEL, D_MODEL), fixed)
    wr = pl.BlockSpec((D_MODEL, ROUTER_COLS), fixed)
    return pl.pallas_call(
        functools.partial(_merge_kernel, a_tiles=a_tiles),
        out_shape=[
            jax.ShapeDtypeStruct((n, D_MODEL), F32),
            jax.ShapeDtypeStruct((n, D_MODEL), F32),
            jax.ShapeDtypeStruct((n, ROUTER_COLS), F32),
            jax.ShapeDtypeStruct((1, ROUTER_COLS), F32),
        ],
        grid=(n // tm,),
        in_specs=[act_a] * 5 + [act_b] * 5 + [wsq, wsq, wsq, pl.BlockSpec((1, D_MODEL), fixed),
                                              wr, wr],
        out_specs=[pl.BlockSpec((tm, D_MODEL), row),
                   pl.BlockSpec((tm, D_MODEL), row),
                   pl.BlockSpec((tm, ROUTER_COLS), row),
                   pl.BlockSpec((1, ROUTER_COLS), fixed)],
        compiler_params=pltpu.CompilerParams(dimension_semantics=("arbitrary",)),
        name="merge_router",
    )(*acts_a, *acts_b, p['wlo'], p['wao'], p['wo'], p['norm_moe_g'], p['wr_hi'], p['wr_lo'])


def _dispatch_kernel(pos_ref, xn_ref, xs_in_ref, xs_ref, sem):
    del xs_in_ref
    tm = xn_ref.shape[0]

    def issue(g, carry):
        base = pl.multiple_of(g * ISSUE_UNROLL, ISSUE_UNROLL)
        for u in range(ISSUE_UNROLL):
            for s in range(2):
                pltpu.make_async_copy(xn_ref.at[pl.ds(base + u, 1)],
                                      xs_ref.at[pl.ds(pos_ref[0, s, base + u], 1)],
                                      sem).start(priority=s)
        return carry

    lax.fori_loop(0, tm // ISSUE_UNROLL, issue, 0)
    for s in range(2):
        pltpu.make_async_copy(xn_ref, xs_ref.at[pl.ds(0, tm)], sem).wait()


def _dispatch(pos3, xn, n_rows, tm):
    n, c = xn.shape
    xs0 = jnp.zeros((n_rows, c), xn.dtype)
    return pl.pallas_call(
        _dispatch_kernel,
        out_shape=jax.ShapeDtypeStruct((n_rows, c), xn.dtype),
        grid=(n // tm,),
        in_specs=[
            pl.BlockSpec((1, 2, tm), lambda i: (i, 0, 0), memory_space=pltpu.SMEM),
            pl.BlockSpec((tm, c), lambda i: (i, 0)),
            pl.BlockSpec(memory_space=pl.ANY),
        ],
        out_specs=pl.BlockSpec(memory_space=pl.ANY),
        scratch_shapes=[pltpu.SemaphoreType.DMA],
        input_output_aliases={2: 0},
        compiler_params=pltpu.CompilerParams(dimension_semantics=("arbitrary",)),
        name="moe_dispatch",
    )(pos3, xn, xs0)


def _experts_kernel(te_ref, nu_ref, xs_ref, wg_ref, wu_ref, wd_ref, ys_ref):
    del te_ref

    @pl.when(pl.program_id(0) < nu_ref[0])
    def _():
        x = xs_ref[...].astype(BF16)
        gate = jnp.dot(x, wg_ref[0].astype(BF16), preferred_element_type=F32)
        up = jnp.dot(x, wu_ref[0].astype(BF16), preferred_element_type=F32)
        h = gate * _sigmoid(gate) * up
        ys_ref[...] = jnp.dot(h.astype(BF16), wd_ref[0].astype(BF16), preferred_element_type=F32)

    @pl.when(pl.program_id(0) >= nu_ref[0])
    def _():
        ys_ref[...] = jnp.zeros_like(ys_ref)


def _experts(tile_expert, n_used, xs, wg, wu, wd, te_rows):
    n_rows, c = xs.shape
    last = lambda t, te, nu: jnp.minimum(t, nu[0] - 1)
    return pl.pallas_call(
        _experts_kernel,
        out_shape=jax.ShapeDtypeStruct((n_rows, c), xs.dtype),
        grid_spec=pltpu.PrefetchScalarGridSpec(
            num_scalar_prefetch=2,
            grid=(n_rows // te_rows,),
            in_specs=[
                pl.BlockSpec((te_rows, c), lambda t, te, nu: (last(t, te, nu), 0)),
                pl.BlockSpec((1, D_MODEL, D_EXPERT), lambda t, te, nu: (te[last(t, te, nu)], 0, 0)),
                pl.BlockSpec((1, D_MODEL, D_EXPERT), lambda t, te, nu: (te[last(t, te, nu)], 0, 0)),
                pl.BlockSpec((1, D_EXPERT, D_MODEL), lambda t, te, nu: (te[last(t, te, nu)], 0, 0)),
            ],
            out_specs=pl.BlockSpec((te_rows, c), lambda t, te, nu: (t, 0)),
        ),
        compiler_params=pltpu.CompilerParams(dimension_semantics=("arbitrary",)),
        name="moe_experts",
    )(tile_expert, n_used, xs, wg, wu, wd)


def _combine_kernel(pos_ref, ys_ref, x1_ref, rec_ref, g_ref, ya_ref, yb_ref, buf_ref, sem, *, a_tiles):
    tm = x1_ref.shape[0]

    def issue(g, carry):
        base = pl.multiple_of(g * ISSUE_UNROLL, ISSUE_UNROLL)
        for u in range(ISSUE_UNROLL):
            for s in range(2):
                pltpu.make_async_copy(ys_ref.at[pl.ds(pos_ref[0, s, base + u], 1)],
                                      buf_ref.at[s, pl.ds(base + u, 1)], sem).start(priority=s)
        return carry

    lax.fori_loop(0, tm // ISSUE_UNROLL, issue, 0)
    for s in range(2):
        pltpu.make_async_copy(ys_ref.at[pl.ds(0, tm)], buf_ref.at[s], sem).wait()
    rec = rec_ref[...]
    w1 = rec[:, R_W1:R_W1 + 1]
    w2 = rec[:, R_W2:R_W2 + 1]
    moe = w1 * buf_ref[0] + w2 * buf_ref[1]
    y = _rms_scale(x1_ref[...] + moe) * g_ref[...]

    @pl.when(pl.program_id(0) < a_tiles)
    def _():
        ya_ref[...] = y

    @pl.when(pl.program_id(0) >= a_tiles)
    def _():
        yb_ref[...] = y


def _combine(pos3, ys, x1, rec, g, n_a, tm):
    n = x1.shape[0]
    c = ys.shape[1]
    a_tiles = n_a // tm
    row = lambda i: (i, 0)
    return pl.pallas_call(
        functools.partial(_combine_kernel, a_tiles=a_tiles),
        out_shape=[jax.ShapeDtypeStruct((n_a, D_MODEL), F32),
                   jax.ShapeDtypeStruct((n - n_a, D_MODEL), F32)],
        grid=(n // tm,),
        in_specs=[
            pl.BlockSpec((1, 2, tm), lambda i: (i, 0, 0), memory_space=pltpu.SMEM),
            pl.BlockSpec(memory_space=pl.ANY),
            pl.BlockSpec((tm, D_MODEL), row),
            pl.BlockSpec((tm, ROUTER_COLS), row),
            pl.BlockSpec((1, D_MODEL), lambda i: (0, 0)),
        ],
        out_specs=[pl.BlockSpec((tm, D_MODEL), lambda i: (jnp.minimum(i, a_tiles - 1), 0)),
                   pl.BlockSpec((tm, D_MODEL), lambda i: (jnp.maximum(i - a_tiles, 0), 0))],
        scratch_shapes=[pltpu.VMEM((2, tm, c), ys.dtype), pltpu.SemaphoreType.DMA],
        compiler_params=pltpu.CompilerParams(dimension_semantics=("arbitrary",)),
        name="moe_combine",
    )(pos3, ys, x1, rec, g)


def _moe_plan(rec, counts, te_rows):
    n = rec.shape[0]
    cnt = counts[0, N_GROUPS:N_GROUPS + N_EXPERTS].astype(jnp.int32)
    tiles = (cnt + te_rows - 1) // te_rows
    tile_end = jnp.cumsum(tiles)
    row_off = (tile_end - tiles) * te_rows
    e = rec[:, R_E1:R_E2 + 1].astype(jnp.int32)
    rank = rec[:, R_RANK1:R_RANK2 + 1].astype(jnp.int32)
    hit = e[:, :, None] == jnp.arange(N_EXPERTS, dtype=jnp.int32)
    pos = jnp.sum(jnp.where(hit, row_off, 0), axis=-1) + rank
    by_tile = lambda tm: pos.reshape(n // tm, tm, 2).transpose(0, 2, 1)
    n_tiles = (2 * n + te_rows - 1) // te_rows + N_EXPERTS
    tile_expert = jnp.minimum(
        jnp.sum(jnp.arange(n_tiles)[:, None] >= tile_end[None, :], axis=1), N_EXPERTS - 1)
    return by_tile, tile_expert.astype(jnp.int32), tile_end[-1:].astype(jnp.int32), n_tiles * te_rows


def _row_tile(n, want):
    tm = min(n, want)
    while n % tm or tm % SUBLANES:
        tm -= 1
    return tm


def _alibi_slopes():
    return jnp.exp2(-8.0 * jnp.arange(1, N_HEADS + 1, dtype=F32) / N_HEADS)


def _prompt_bias():
    qi = jnp.arange(WINDOW)[:, None]
    kj = jnp.arange(2 * WINDOW)[None, :]
    dist = WINDOW + qi - kj
    in_window = (dist >= 0) & (dist < WINDOW)
    valid = jnp.stack([in_window & (kj >= WINDOW), in_window])
    score = -_alibi_slopes()[None, :, None, None] * dist.astype(F32)[None, None]
    return jnp.where(valid[:, None], score, NEG_INF)


def _sample_bias(t):
    q_pos = jnp.tile(jnp.arange(t), GQA)[:, None]
    k_pos = (jnp.arange(WINDOW + t) - WINDOW)[None, :]
    dist = q_pos - k_pos
    valid = (dist >= 0) & (dist < WINDOW)
    slopes = jnp.repeat(_alibi_slopes().reshape(N_KV_HEADS, GQA), t, axis=1)
    score = -slopes[:, :, None] * dist.astype(F32)[None]
    return jnp.where(valid[None], score, NEG_INF)


def _layer_params(norm_mix_g, w_in, conv_w, conv_b, lru_wa, lru_ba, lru_wx, lru_bx, lru_lambda,
                  attn_sinks, w_lru_out, w_attn_out, w_o, norm_moe_g, w_router_group,
                  w_router_expert, moe_w_gate, moe_w_up, moe_w_down):
    per_pack = GATE_PACK // LRU_BLOCK_DIM

    def pack_diag(w):
        w4 = w.reshape(LRU_WIDTH // GATE_PACK, per_pack, LRU_BLOCK_DIM, LRU_BLOCK_DIM)
        eye = jnp.eye(per_pack, dtype=w.dtype)
        return jnp.einsum('pbij,bc->pbicj', w4, eye).reshape(-1, GATE_PACK, GATE_PACK)

    wbd = jnp.concatenate([pack_diag(lru_wa), pack_diag(lru_wx)], axis=-1).astype(BF16)
    w_router = jnp.concatenate(
        [w_router_group, jnp.transpose(w_router_expert, (1, 0, 2)).reshape(D_MODEL, N_EXPERTS)],
        axis=1)
    w_router = jnp.pad(w_router, ((0, 0), (0, ROUTER_COLS - w_router.shape[1])))
    wr_hi = w_router.astype(BF16)
    wr_lo = (w_router - wr_hi.astype(F32)).astype(BF16)
    wg = moe_w_gate.reshape(N_EXPERTS, D_MODEL, D_EXPERT)
    wu = moe_w_up.reshape(N_EXPERTS, D_MODEL, D_EXPERT)
    wd = moe_w_down.reshape(N_EXPERTS, D_EXPERT, D_MODEL)
    row = lambda v: v.reshape(1, -1)
    return dict(
        norm_mix_g=row(norm_mix_g), w_in=w_in.astype(BF16), conv_w=conv_w, conv_b=row(conv_b),
        wbd=wbd, ba=row(lru_ba), bx=row(lru_bx), lam=row(lru_lambda), sinks=attn_sinks,
        wlo=w_lru_out.astype(BF16), wao=w_attn_out.astype(BF16), wo=w_o.astype(BF16),
        norm_moe_g=row(norm_moe_g), wr_hi=wr_hi, wr_lo=wr_lo, wg=wg, wu=wu, wd=wd)


def _prompt_mixers(x, p):
    bsz, t, _ = x.shape
    assert t % WINDOW == 0 and t >= CONV_W - 1
    n = bsz * t
    x2 = x.reshape(n, D_MODEL)
    xl, q, k, v, gl, ga = _inproj(x2, p['norm_mix_g'], p['w_in'], _row_tile(n, 512))
    tc = _row_tile(t, 256)
    yl, h_last = _lru(xl, jnp.zeros((bsz, SUBLANES, LRU_WIDTH), F32),
                      jnp.zeros((bsz, 1, LRU_WIDTH), F32), p['conv_w'], p['conv_b'], p['wbd'],
                      p['ba'], p['bx'], p['lam'], nseq=bsz, seqlen=t, bb=1, tc=tc)
    k3 = k.reshape(bsz, t, KV_COLS)
    v3 = v.reshape(bsz, t, KV_COLS)
    ya = _attn_prompt(q.reshape(bsz, t, -1), k3, v3, p['sinks'], _prompt_bias())
    new_conv = xl.reshape(bsz, t, LRU_WIDTH)[:, t - (CONV_W - 1):]
    new_k = k3[:, t - WINDOW:].reshape(bsz, WINDOW, N_KV_HEADS, HEAD_DIM)
    new_v = v3[:, t - WINDOW:].reshape(bsz, WINDOW, N_KV_HEADS, HEAD_DIM)
    return ((x2, yl, ya.reshape(n, -1), gl, ga),
            (new_conv, h_last.reshape(bsz, LRU_WIDTH), new_k, new_v))


def _sample_mixers(x, conv_buf, h0, k_buf, v_buf, p):
    bsz, t, _ = x.shape
    assert t % SUBLANES == 0 and CONV_W - 1 <= t <= WINDOW
    n = bsz * t
    x2 = x.reshape(n, D_MODEL)
    xl, q, k, v, gl, ga = _inproj(x2, p['norm_mix_g'], p['w_in'], _row_tile(n, 512))
    bb = _row_tile(bsz, 16)
    prev8 = jnp.pad(conv_buf, ((0, 0), (SUBLANES - (CONV_W - 1), 0), (0, 0)))
    yl, h_last = _lru(xl, prev8, h0.reshape(bsz, 1, LRU_WIDTH), p['conv_w'], p['conv_b'],
                      p['wbd'], p['ba'], p['bx'], p['lam'], nseq=bsz, seqlen=t, bb=bb, tc=t)
    q4 = q.reshape(bsz, t, N_KV_HEADS, GQA, HEAD_DIM).transpose(0, 2, 3, 1, 4).reshape(
        bsz, N_KV_HEADS, GQA * t, HEAD_DIM)
    sink_rows = jnp.repeat(p['sinks'].reshape(N_KV_HEADS, GQA), t, axis=1)[:, :, None]
    o4, new_k, new_v = _attn_sample(
        q4, k.reshape(bsz, t, KV_COLS), v.reshape(bsz, t, KV_COLS),
        k_buf.reshape(bsz, WINDOW, KV_COLS), v_buf.reshape(bsz, WINDOW, KV_COLS),
        _sample_bias(t), sink_rows, _row_tile(bsz, 2 * SAMPLE_SEQ_UNROLL))
    ya = o4.reshape(bsz, N_KV_HEADS, GQA, t, HEAD_DIM).transpose(0, 3, 1, 2, 4).reshape(n, -1)
    new_conv = xl.reshape(bsz, t, LRU_WIDTH)[:, t - (CONV_W - 1):]
    shape = (bsz, WINDOW, N_KV_HEADS, HEAD_DIM)
    return ((x2, yl, ya, gl, ga),
            (new_conv, h_last.reshape(bsz, LRU_WIDTH), new_k.reshape(shape), new_v.reshape(shape)))


def _merge_and_moe(acts_a, acts_b, p, norm_final_g):
    n_a, n_b = acts_a[0].shape[0], acts_b[0].shape[0]
    n = n_a + n_b
    tm = _row_tile(math.gcd(n_a, n_b), MERGE_ROWS)
    x1, xn, rec, cnt = _merge(acts_a, acts_b, p, tm)
    by_tile, tile_expert, n_used, n_rows = _moe_plan(rec, cnt, EXPERT_ROWS)
    tm_d = _row_tile(n, DISPATCH_ROWS)
    xs = _dispatch(by_tile(tm_d), xn, n_rows, tm_d)
    ys = _experts(tile_expert, n_used, xs, p['wg'], p['wu'], p['wd'], EXPERT_ROWS)
    return _combine(by_tile(tm), ys, x1, rec, norm_final_g.reshape(1, -1), n_a, tm)


def kernel(x_prompt, x_sample, state_conv, state_lru_h, cache_win_k, cache_win_v, norm_mix_g, w_in, conv_w, conv_b, lru_wa, lru_ba, lru_wx, lru_bx, lru_lambda, attn_sinks, w_lru_out, w_attn_out, w_o, norm_moe_g, w_router_group, w_router_expert, moe_w_gate, moe_w_up, moe_w_down, norm_final_g):
    depth = w_in.shape[0]
    assert depth == 1, "the final norm is fused into the single layer's MoE kernel"
    p = _layer_params(norm_mix_g[0], w_in[0], conv_w[0], conv_b[0], lru_wa[0], lru_ba[0],
                      lru_wx[0], lru_bx[0], lru_lambda[0], attn_sinks[0], w_lru_out[0],
                      w_attn_out[0], w_o[0], norm_moe_g[0], w_router_group[0],
                      w_router_expert[0], moe_w_gate[0], moe_w_up[0], moe_w_down[0])
    acts_p, (c1, h1, k1, v1) = _prompt_mixers(x_prompt, p)
    acts_s, (c2, h2, k2, v2) = _sample_mixers(x_sample, state_conv[0], state_lru_h[0],
                                              cache_win_k[0], cache_win_v[0], p)
    yp, ys = _merge_and_moe(acts_p, acts_s, p, norm_final_g)
    return (yp.reshape(x_prompt.shape), ys.reshape(x_sample.shape), c1[None], h1[None], k1[None],
            v1[None], c2[None], h2[None], k2[None], v2[None])
```

```python
import functools
import math

import jax
import jax.numpy as jnp
from jax import lax
from jax.experimental import pallas as pl
from jax.experimental.pallas import tpu as pltpu

D_MODEL = 1024
LRU_WIDTH = D_MODEL
LRU_BLOCKS = 16
LRU_BLOCK_DIM = LRU_WIDTH // LRU_BLOCKS
LRU_C = 8.0
CONV_W = 4
N_HEADS = 16
N_KV_HEADS = 4
GQA = N_HEADS // N_KV_HEADS
HEAD_DIM = D_MODEL // N_HEADS
KV_COLS = N_KV_HEADS * HEAD_DIM
WINDOW = 128
N_GROUPS = 4
EXPERTS_PER_GROUP = 8
N_EXPERTS = N_GROUPS * EXPERTS_PER_GROUP
D_EXPERT = D_MODEL // 4
RMS_EPS = 1e-6
NEG_INF = -1e30
IN_COLS = LRU_WIDTH + (N_HEADS + 2 * N_KV_HEADS) * HEAD_DIM + 2 * D_MODEL

SUBLANES = 8
LANES = 128
ROUTER_COLS = LANES
GATE_PACK = 256

MERGE_ROWS = 512
EXPERT_ROWS = 256
SAMPLE_SEQS = 8

F32 = jnp.float32
BF16 = jnp.bfloat16


def _sigmoid(z):
    return 1.0 / (1.0 + jnp.exp(-z))


def _expm1(z):
    u = jnp.exp(z)
    return jnp.where(u == 1.0, z, jnp.where(u == 0.0, -1.0, (u - 1.0) * z / jnp.log(u)))


def _rms_scale(x):
    return x * lax.rsqrt(jnp.mean(x * x, axis=-1, keepdims=True) + RMS_EPS)


def _inproj_kernel(x_ref, g_ref, w_ref, xl_ref, q_ref, k_ref, v_ref, gl_ref, ga_ref):
    xb = (_rms_scale(x_ref[...]) * g_ref[...]).astype(BF16)

    def proj(c0, c1):
        return jnp.dot(xb, w_ref[:, c0:c1], preferred_element_type=F32)

    c0 = LRU_WIDTH
    c1 = c0 + N_HEADS * HEAD_DIM
    c2 = c1 + KV_COLS
    c3 = c2 + KV_COLS
    c4 = c3 + D_MODEL
    xl_ref[...] = proj(0, c0)
    q_ref[...] = (proj(c0, c1) * (HEAD_DIM ** -0.5)).astype(BF16)
    k_ref[...] = proj(c1, c2)
    v_ref[...] = proj(c2, c3)
    gl_ref[...] = _sigmoid(proj(c3, c4)).astype(BF16)
    ga_ref[...] = _sigmoid(proj(c4, IN_COLS)).astype(BF16)


def _inproj(x2, g, w_bf, tm):
    n = x2.shape[0]
    row = lambda i: (i, 0)
    fixed = lambda i: (0, 0)
    outs = [
        jax.ShapeDtypeStruct((n, LRU_WIDTH), F32),
        jax.ShapeDtypeStruct((n, N_HEADS * HEAD_DIM), BF16),
        jax.ShapeDtypeStruct((n, KV_COLS), F32),
        jax.ShapeDtypeStruct((n, KV_COLS), F32),
        jax.ShapeDtypeStruct((n, D_MODEL), BF16),
        jax.ShapeDtypeStruct((n, D_MODEL), BF16),
    ]
    return pl.pallas_call(
        _inproj_kernel,
        out_shape=outs,
        grid=(n // tm,),
        in_specs=[
            pl.BlockSpec((tm, D_MODEL), row),
            pl.BlockSpec((1, D_MODEL), fixed),
            pl.BlockSpec((D_MODEL, IN_COLS), fixed),
        ],
        out_specs=[pl.BlockSpec((tm, o.shape[1]), row) for o in outs],
        compiler_params=pltpu.CompilerParams(dimension_semantics=("parallel",)),
        name="inproj",
    )(x2, g, w_bf)


def _lru_kernel(xl_ref, prev_ref, h0_ref, cw_ref, cb_ref, wbd_ref, ba_ref, bx_ref, lam_ref,
                y_ref, hl_ref, xs_ref, xc_ref, a_ref, b_ref, hc_ref, *, bb, tc):
    t = pl.program_id(1)

    @pl.when(t == 0)
    def _():
        xs_ref[:, 0:SUBLANES, :] = prev_ref[...]
        hc_ref[...] = h0_ref[...]

    @pl.when(t > 0)
    def _():
        xs_ref[:, 0:SUBLANES, :] = xs_ref[:, tc:tc + SUBLANES, :]

    for s in range(bb):
        xs_ref[s, SUBLANES:SUBLANES + tc, :] = xl_ref[s * tc:(s + 1) * tc, :]
    cw = cw_ref[...]
    for s in range(bb):
        acc = cb_ref[...] + xs_ref[s, SUBLANES:SUBLANES + tc, :] * cw[CONV_W - 1:CONV_W, :]
        for j in range(CONV_W - 1):
            off = SUBLANES - (CONV_W - 1) + j
            acc = acc + xs_ref[s, off:off + tc, :] * cw[j:j + 1, :]
        xc_ref[s * tc:(s + 1) * tc, :] = acc

    lam = lam_ref[...]
    softplus_neg_lam = jnp.maximum(-lam, 0.0) + jnp.log1p(jnp.exp(-jnp.abs(lam)))
    for g in range(LRU_WIDTH // GATE_PACK):
        cols = slice(g * GATE_PACK, (g + 1) * GATE_PACK)
        xc = xc_ref[:, cols]
        gates = jnp.dot(xc.astype(BF16), wbd_ref[g], preferred_element_type=F32)
        r = _sigmoid(gates[:, :GATE_PACK] + ba_ref[:, cols])
        ig = _sigmoid(gates[:, GATE_PACK:] + bx_ref[:, cols])
        log_a = (-LRU_C) * r * softplus_neg_lam[:, cols]
        a_ref[:, cols] = jnp.exp(log_a)
        b_ref[:, cols] = jnp.sqrt(-_expm1(2.0 * log_a)) * (ig * xc)

    row = lax.broadcasted_iota(jnp.int32, (SUBLANES, LRU_WIDTH), 0)
    for s in range(bb):
        def group(gi, h8, s=s):
            r0 = pl.multiple_of(s * tc + gi * SUBLANES, SUBLANES)
            a = a_ref[pl.ds(r0, SUBLANES), :]
            b = b_ref[pl.ds(r0, SUBLANES), :]
            for d in (1, 2, 4):
                a_up = jnp.where(row >= d, pltpu.roll(a, d, 0), 1.0)
                b_up = jnp.where(row >= d, pltpu.roll(b, d, 0), 0.0)
                b = a * b_up + b
                a = a * a_up
            h = a * h8 + b
            b_ref[pl.ds(r0, SUBLANES), :] = h
            return jnp.broadcast_to(h[SUBLANES - 1:SUBLANES, :], (SUBLANES, LRU_WIDTH))

        h8 = lax.fori_loop(0, tc // SUBLANES, group,
                           jnp.broadcast_to(hc_ref[s], (SUBLANES, LRU_WIDTH)))
        hc_ref[s] = h8[0:1, :]

    y_ref[...] = b_ref[...].astype(y_ref.dtype)

    @pl.when(t == pl.num_programs(1) - 1)
    def _():
        hl_ref[...] = hc_ref[...]


def _lru(xl2, prev8, h0, cw, cb, wbd, ba, bx, lam, *, nseq, seqlen, bb, tc):
    nt = seqlen // tc
    assert bb == 1 or nt == 1
    rows = bb * tc
    fixed2 = lambda b, t: (0, 0)
    fixed3 = lambda b, t: (0, 0, 0)
    return pl.pallas_call(
        functools.partial(_lru_kernel, bb=bb, tc=tc),
        out_shape=[
            jax.ShapeDtypeStruct((nseq * seqlen, LRU_WIDTH), BF16),
            jax.ShapeDtypeStruct((nseq, 1, LRU_WIDTH), F32),
        ],
        grid=(nseq // bb, nt),
        in_specs=[
            pl.BlockSpec((rows, LRU_WIDTH), lambda b, t: (b * nt + t, 0)),
            pl.BlockSpec((bb, SUBLANES, LRU_WIDTH), lambda b, t: (b, 0, 0)),
            pl.BlockSpec((bb, 1, LRU_WIDTH), lambda b, t: (b, 0, 0)),
            pl.BlockSpec((CONV_W, LRU_WIDTH), fixed2),
            pl.BlockSpec((1, LRU_WIDTH), fixed2),
            pl.BlockSpec((LRU_WIDTH // GATE_PACK, GATE_PACK, 2 * GATE_PACK), fixed3),
            pl.BlockSpec((1, LRU_WIDTH), fixed2),
            pl.BlockSpec((1, LRU_WIDTH), fixed2),
            pl.BlockSpec((1, LRU_WIDTH), fixed2),
        ],
        out_specs=[
            pl.BlockSpec((rows, LRU_WIDTH), lambda b, t: (b * nt + t, 0)),
            pl.BlockSpec((bb, 1, LRU_WIDTH), lambda b, t: (b, 0, 0)),
        ],
        scratch_shapes=[
            pltpu.VMEM((bb, tc + SUBLANES, LRU_WIDTH), F32),
            pltpu.VMEM((rows, LRU_WIDTH), F32),
            pltpu.VMEM((rows, LRU_WIDTH), F32),
            pltpu.VMEM((rows, LRU_WIDTH), F32),
            pltpu.VMEM((bb, 1, LRU_WIDTH), F32),
        ],
        compiler_params=pltpu.CompilerParams(dimension_semantics=("parallel", "arbitrary")),
        name="conv_rglru",
    )(xl2, prev8, h0, cw, cb, wbd, ba, bx, lam)


def _softmax_sink_pv(s, sink, v_bf):
    m = jnp.maximum(jnp.max(s, axis=-1, keepdims=True), sink)
    p = jnp.exp(s - m)
    denom = jnp.sum(p, axis=-1, keepdims=True) + jnp.exp(sink - m)
    o = jnp.dot(p.astype(BF16), v_bf, preferred_element_type=F32)
    return o / denom


def _attn_prompt_kernel(sink_ref, q_ref, kc_ref, kp_ref, vc_ref, vp_ref, bias_ref, o_ref):
    bsel = jnp.where(pl.program_id(1) == 0, 0, 1)
    kcat = jnp.concatenate([kp_ref[0], kc_ref[0]], axis=0).astype(BF16)
    vcat = jnp.concatenate([vp_ref[0], vc_ref[0]], axis=0).astype(BF16)
    q = q_ref[0]
    outs = []
    for h in range(N_HEADS):
        kv = slice((h // GQA) * HEAD_DIM, (h // GQA + 1) * HEAD_DIM)
        s = lax.dot_general(q[:, h * HEAD_DIM:(h + 1) * HEAD_DIM], kcat[:, kv],
                            (((1,), (1,)), ((), ())), preferred_element_type=F32)
        s = s + bias_ref[bsel, h]
        outs.append(_softmax_sink_pv(s, sink_ref[h], vcat[:, kv]))
    o_ref[0] = jnp.concatenate(outs, axis=-1).astype(o_ref.dtype)


def _attn_prompt(q3, k3, v3, sinks, bias):
    bsz, t, _ = q3.shape
    nb = t // WINDOW
    cur = lambda b, i: (b, i, 0)
    prev = lambda b, i: (b, jnp.maximum(i - 1, 0), 0)
    return pl.pallas_call(
        _attn_prompt_kernel,
        out_shape=jax.ShapeDtypeStruct((bsz, t, N_HEADS * HEAD_DIM), BF16),
        grid=(bsz, nb),
        in_specs=[
            pl.BlockSpec(memory_space=pltpu.SMEM),
            pl.BlockSpec((1, WINDOW, N_HEADS * HEAD_DIM), cur),
            pl.BlockSpec((1, WINDOW, KV_COLS), cur),
            pl.BlockSpec((1, WINDOW, KV_COLS), prev),
            pl.BlockSpec((1, WINDOW, KV_COLS), cur),
            pl.BlockSpec((1, WINDOW, KV_COLS), prev),
            pl.BlockSpec((2, N_HEADS, WINDOW, 2 * WINDOW), lambda b, i: (0, 0, 0, 0)),
        ],
        out_specs=pl.BlockSpec((1, WINDOW, N_HEADS * HEAD_DIM), cur),
        compiler_params=pltpu.CompilerParams(dimension_semantics=("parallel", "arbitrary")),
        name="swa_prompt",
    )(sinks, q3, k3, k3, v3, v3, bias)


def _attn_sample_kernel(q_ref, kn_ref, vn_ref, ck_ref, cv_ref, bias_ref, sink_ref,
                        o_ref, nk_ref, nv_ref, *, bb, t):
    keep = WINDOW - t
    for s in range(bb):
        nk_ref[s, 0:keep, :] = ck_ref[s, t:WINDOW, :]
        nk_ref[s, keep:WINDOW, :] = kn_ref[s]
        nv_ref[s, 0:keep, :] = cv_ref[s, t:WINDOW, :]
        nv_ref[s, keep:WINDOW, :] = vn_ref[s]
    kall = jnp.concatenate([ck_ref[...].reshape(bb * WINDOW, KV_COLS),
                            kn_ref[...].reshape(bb * t, KV_COLS)], axis=0).astype(BF16)
    vall = jnp.concatenate([cv_ref[...].reshape(bb * WINDOW, KV_COLS),
                            vn_ref[...].reshape(bb * t, KV_COLS)], axis=0).astype(BF16)
    rows = q_ref.shape[2]
    for kh in range(N_KV_HEADS):
        kv = slice(kh * HEAD_DIM, (kh + 1) * HEAD_DIM)
        qh = q_ref[:, kh].reshape(bb * rows, HEAD_DIM)
        sc = lax.dot_general(qh, kall[:, kv], (((1,), (1,)), ((), ())),
                             preferred_element_type=F32)
        o = _softmax_sink_pv(sc + bias_ref[kh], sink_ref[kh], vall[:, kv])
        o_ref[:, kh] = o.reshape(bb, rows, HEAD_DIM).astype(o_ref.dtype)


def _attn_sample(q4, kn3, vn3, ck3, cv3, bias, sink_rows, bb):
    nseq, _, rows, _ = q4.shape
    t = kn3.shape[1]
    b3 = lambda i: (i, 0, 0)
    b4 = lambda i: (i, 0, 0, 0)
    return pl.pallas_call(
        functools.partial(_attn_sample_kernel, bb=bb, t=t),
        out_shape=[
            jax.ShapeDtypeStruct((nseq, N_KV_HEADS, rows, HEAD_DIM), BF16),
            jax.ShapeDtypeStruct((nseq, WINDOW, KV_COLS), F32),
            jax.ShapeDtypeStruct((nseq, WINDOW, KV_COLS), F32),
        ],
        grid=(nseq // bb,),
        in_specs=[
            pl.BlockSpec((bb, N_KV_HEADS, rows, HEAD_DIM), b4),
            pl.BlockSpec((bb, t, KV_COLS), b3),
            pl.BlockSpec((bb, t, KV_COLS), b3),
            pl.BlockSpec((bb, WINDOW, KV_COLS), b3),
            pl.BlockSpec((bb, WINDOW, KV_COLS), b3),
            pl.BlockSpec((N_KV_HEADS, bb * rows, bb * (WINDOW + t)), lambda i: (0, 0, 0)),
            pl.BlockSpec((N_KV_HEADS, bb * rows, 1), lambda i: (0, 0, 0)),
        ],
        out_specs=[
            pl.BlockSpec((bb, N_KV_HEADS, rows, HEAD_DIM), b4),
            pl.BlockSpec((bb, WINDOW, KV_COLS), b3),
            pl.BlockSpec((bb, WINDOW, KV_COLS), b3),
        ],
        compiler_params=pltpu.CompilerParams(dimension_semantics=("parallel",)),
        name="swa_sample",
    )(q4, kn3, vn3, ck3, cv3, bias, sink_rows)


R_E1, R_E2, R_D1, R_D2, R_W1, R_W2 = range(6)


def _route(logits):
    rows = logits.shape[0]
    lane = lax.broadcasted_iota(jnp.int32, logits.shape, 1)
    big = jnp.int32(ROUTER_COLS)
    gl = jnp.where(lane < N_GROUPS, logits, -jnp.inf)
    gmax = jnp.max(gl, axis=-1, keepdims=True)
    p_grp = 1.0 / jnp.sum(jnp.exp(gl - gmax), axis=-1, keepdims=True)
    g_idx = jnp.min(jnp.where(gl == gmax, lane, big), axis=-1, keepdims=True)
    e_lo = N_GROUPS + EXPERTS_PER_GROUP * g_idx
    el = jnp.where((lane >= e_lo) & (lane < e_lo + EXPERTS_PER_GROUP), logits, -jnp.inf)
    m1 = jnp.max(el, axis=-1, keepdims=True)
    i1 = jnp.min(jnp.where(el == m1, lane, big), axis=-1, keepdims=True)
    el2 = jnp.where(lane == i1, -jnp.inf, el)
    m2 = jnp.max(el2, axis=-1, keepdims=True)
    i2 = jnp.min(jnp.where(el2 == m2, lane, big), axis=-1, keepdims=True)
    e21 = jnp.exp(m2 - m1)
    w1 = p_grp / (1.0 + e21)
    w2 = e21 * w1
    hit1 = lane == i1
    hit2 = lane == i2
    onehot = jnp.where(hit1 | hit2, 1.0, 0.0).astype(BF16)
    r_i = lax.broadcasted_iota(jnp.int32, (rows, rows), 0)
    c_i = lax.broadcasted_iota(jnp.int32, (rows, rows), 1)
    tril = jnp.where(c_i <= r_i, 1.0, 0.0).astype(BF16)
    upto = jnp.dot(tril, onehot, preferred_element_type=F32)
    count = upto[rows - 1:rows, :]
    run8 = jnp.floor((count + (SUBLANES - 1)) * (1.0 / SUBLANES))
    k_i = lax.broadcasted_iota(jnp.int32, (ROUTER_COLS, ROUTER_COLS), 0)
    j_i = lax.broadcasted_iota(jnp.int32, (ROUTER_COLS, ROUTER_COLS), 1)
    before = jnp.where(k_i < j_i, 1.0, 0.0).astype(BF16)
    start8 = jnp.dot(jnp.broadcast_to(run8, (SUBLANES, ROUTER_COLS)).astype(BF16), before,
                     preferred_element_type=F32)[0:1, :]
    dest = SUBLANES * start8 + upto - 1.0
    d1 = jnp.sum(jnp.where(hit1, dest, 0.0), axis=-1, keepdims=True)
    d2 = jnp.sum(jnp.where(hit2, dest, 0.0), axis=-1, keepdims=True)
    rec = jnp.zeros(logits.shape, F32)
    for idx, val in ((R_E1, (i1 - N_GROUPS).astype(F32)), (R_E2, (i2 - N_GROUPS).astype(F32)),
                     (R_D1, d1), (R_D2, d2), (R_W1, w1), (R_W2, w2)):
        rec = jnp.where(lane == idx, val, rec)
    return rec, count


def _merge_kernel(*refs, a_tiles):
    acts_a, acts_b = refs[0:5], refs[5:10]
    wlo_ref, wao_ref, wo_ref, g_ref, wrh_ref, wrl_ref, x1_ref, xn_ref, rec_ref, cnt_ref = refs[10:]

    def tile(x_ref, yl_ref, ya_ref, gl_ref, ga_ref):
        a = jnp.dot(yl_ref[...], wlo_ref[...], preferred_element_type=F32)
        b = jnp.dot(ya_ref[...], wao_ref[...], preferred_element_type=F32)
        merged = gl_ref[...].astype(F32) * a + ga_ref[...].astype(F32) * b
        x1 = x_ref[...] + jnp.dot(merged.astype(BF16), wo_ref[...], preferred_element_type=F32)
        x1_ref[...] = x1
        xn = _rms_scale(x1) * g_ref[...]
        xh = xn.astype(BF16)
        xn_ref[...] = xh
        xlo = (xn - xh.astype(F32)).astype(BF16)
        logits = (jnp.dot(xh, wrh_ref[...], preferred_element_type=F32)
                  + jnp.dot(xlo, wrh_ref[...], preferred_element_type=F32)
                  + jnp.dot(xh, wrl_ref[...], preferred_element_type=F32))
        rec, count = _route(logits)
        rec_ref[...] = rec
        cnt_ref[0] = jnp.broadcast_to(count, (SUBLANES, ROUTER_COLS))

    pl.when(pl.program_id(0) < a_tiles)(lambda: tile(*acts_a))
    pl.when(pl.program_id(0) >= a_tiles)(lambda: tile(*acts_b))


def _merge(acts_a, acts_b, p, tm):
    n_a, n_b = acts_a[0].shape[0], acts_b[0].shape[0]
    n = n_a + n_b
    a_tiles = n_a // tm
    row = lambda i: (i, 0)
    fixed = lambda i: (0, 0)
    act_a = pl.BlockSpec((tm, D_MODEL), lambda i: (jnp.minimum(i, a_tiles - 1), 0))
    act_b = pl.BlockSpec((tm, D_MODEL), lambda i: (jnp.maximum(i - a_tiles, 0), 0))
    wsq = pl.BlockSpec((D_MODEL, D_MODEL), fixed)
    wr = pl.BlockSpec((D_MODEL, ROUTER_COLS), fixed)
    return pl.pallas_call(
        functools.partial(_merge_kernel, a_tiles=a_tiles),
        out_shape=[
            jax.ShapeDtypeStruct((n, D_MODEL), F32),
            jax.ShapeDtypeStruct((n, D_MODEL), BF16),
            jax.ShapeDtypeStruct((n, ROUTER_COLS), F32),
            jax.ShapeDtypeStruct((n // tm, SUBLANES, ROUTER_COLS), F32),
        ],
        grid=(n // tm,),
        in_specs=[act_a] * 5 + [act_b] * 5 + [wsq, wsq, wsq, pl.BlockSpec((1, D_MODEL), fixed),
                                              wr, wr],
        out_specs=[pl.BlockSpec((tm, D_MODEL), row),
                   pl.BlockSpec((tm, D_MODEL), row),
                   pl.BlockSpec((tm, ROUTER_COLS), row),
                   pl.BlockSpec((1, SUBLANES, ROUTER_COLS), lambda i: (i, 0, 0))],
        compiler_params=pltpu.CompilerParams(dimension_semantics=("parallel",)),
        name="merge_router",
    )(*acts_a, *acts_b, p['wlo'], p['wao'], p['wo'], p['norm_moe_g'], p['wr_hi'], p['wr_lo'])


def _tile_rows(tm):
    return -(-(2 * tm + N_EXPERTS * (SUBLANES - 1)) // LANES) * LANES


def _for_each_piece(i, start_ref, n8_ref, fn):
    def per_expert(e, row8):
        n8 = n8_ref[i * N_EXPERTS + e]
        dst0 = start_ref[i * N_EXPERTS + e]

        def piece(k, carry):
            fn(pl.multiple_of((row8 + k) * SUBLANES, SUBLANES),
               pl.multiple_of(dst0 + k * SUBLANES, SUBLANES))
            return carry

        lax.fori_loop(0, n8, piece, 0)
        return row8 + n8

    lax.fori_loop(0, N_EXPERTS, per_expert, 0)


def _dispatch_kernel(start_ref, n8_ref, tot8_ref, pad_start_ref, pad_n8_ref, pad_tot8_ref,
                     used_ref, xn_ref, rec_ref, xs_ref, buf_ref, zero_ref, sem, zsem):
    i = pl.program_id(0)
    slot = i % 2
    rv = buf_ref.shape[1]
    rec_t = jnp.transpose(rec_ref[...])
    row = lax.broadcasted_iota(jnp.int32, (rv, rec_t.shape[1]), 0).astype(F32)
    place = jnp.where((row == rec_t[R_D1:R_D1 + 1, :]) | (row == rec_t[R_D2:R_D2 + 1, :]), 1.0, 0.0)
    buf_ref[slot] = jnp.dot(place.astype(BF16), xn_ref[...], preferred_element_type=F32)

    def piece_copy(s, src_row, dst_row):
        return pltpu.make_async_copy(buf_ref.at[s, pl.ds(src_row, SUBLANES)],
                                     xs_ref.at[pl.ds(dst_row, SUBLANES)], sem.at[s])

    def drain(s, n8):
        lax.fori_loop(0, n8, lambda k, c: (piece_copy(s, 0, 0).wait(), c)[1], 0)

    @pl.when(i > 0)
    def _():
        drain(1 - slot, tot8_ref[i - 1])

    _for_each_piece(i, start_ref, n8_ref, lambda src_row, dst_row: piece_copy(slot, src_row, dst_row).start())

    @pl.when(i == pl.num_programs(0) - 1)
    def _():
        zero_ref[...] = jnp.zeros_like(zero_ref)

        def pad_expert(e, carry):
            def piece(k, c):
                dst_row = pl.multiple_of(pad_start_ref[e] + k * SUBLANES, SUBLANES)
                pltpu.make_async_copy(zero_ref.at[pl.ds(0, SUBLANES)],
                                      xs_ref.at[pl.ds(dst_row, SUBLANES)], sem.at[slot]).start()
                return c

            lax.fori_loop(0, pad_n8_ref[e], piece, 0)
            return carry

        lax.fori_loop(0, N_EXPERTS, pad_expert, 0)

        te_rows = zero_ref.shape[0]

        def tail_copy(t):
            return pltpu.make_async_copy(
                zero_ref, xs_ref.at[pl.ds(pl.multiple_of(t * te_rows, te_rows), te_rows)], zsem)

        n_tiles = xs_ref.shape[0] // te_rows
        lax.fori_loop(used_ref[0], n_tiles, lambda t, c: (tail_copy(t).start(), c)[1], 0)
        drain(slot, tot8_ref[i] + pad_tot8_ref[0])
        lax.fori_loop(used_ref[0], n_tiles, lambda t, c: (tail_copy(t).wait(), c)[1], 0)


def _dispatch(plan, pad_plan, n_used, xn, rec, n_rows, tm, te_rows):
    n = xn.shape[0]
    rv = _tile_rows(tm)
    row = lambda i, *_: (i, 0)
    return pl.pallas_call(
        _dispatch_kernel,
        out_shape=jax.ShapeDtypeStruct((n_rows, D_MODEL), F32),
        grid_spec=pltpu.PrefetchScalarGridSpec(
            num_scalar_prefetch=7,
            grid=(n // tm,),
            in_specs=[pl.BlockSpec((tm, D_MODEL), row), pl.BlockSpec((tm, ROUTER_COLS), row)],
            out_specs=pl.BlockSpec(memory_space=pl.ANY),
            scratch_shapes=[pltpu.VMEM((2, rv, D_MODEL), F32),
                            pltpu.VMEM((te_rows, D_MODEL), F32), pltpu.SemaphoreType.DMA((2,)),
                            pltpu.SemaphoreType.DMA],
        ),
        compiler_params=pltpu.CompilerParams(dimension_semantics=("arbitrary",)),
        name="moe_dispatch",
    )(*plan, *pad_plan, n_used, xn, rec)


def _experts_kernel(te_ref, nu_ref, xs_ref, wg_ref, wu_ref, wd_ref, ys_ref):
    del te_ref

    @pl.when(pl.program_id(0) < nu_ref[0])
    def _():
        x = xs_ref[...].astype(BF16)
        gate = jnp.dot(x, wg_ref[0].astype(BF16), preferred_element_type=F32)
        up = jnp.dot(x, wu_ref[0].astype(BF16), preferred_element_type=F32)
        h = gate * _sigmoid(gate) * up
        ys_ref[...] = jnp.dot(h.astype(BF16), wd_ref[0].astype(BF16), preferred_element_type=F32)

    @pl.when(pl.program_id(0) >= nu_ref[0])
    def _():
        ys_ref[...] = jnp.zeros_like(ys_ref)


def _experts(tile_expert, n_used, xs, wg, wu, wd, te_rows):
    n_rows, c = xs.shape
    last = lambda t, te, nu: jnp.minimum(t, nu[0] - 1)
    return pl.pallas_call(
        _experts_kernel,
        out_shape=jax.ShapeDtypeStruct((n_rows, c), xs.dtype),
        grid_spec=pltpu.PrefetchScalarGridSpec(
            num_scalar_prefetch=2,
            grid=(n_rows // te_rows,),
            in_specs=[
                pl.BlockSpec((te_rows, c), lambda t, te, nu: (last(t, te, nu), 0)),
                pl.BlockSpec((1, D_MODEL, D_EXPERT), lambda t, te, nu: (te[last(t, te, nu)], 0, 0)),
                pl.BlockSpec((1, D_MODEL, D_EXPERT), lambda t, te, nu: (te[last(t, te, nu)], 0, 0)),
                pl.BlockSpec((1, D_EXPERT, D_MODEL), lambda t, te, nu: (te[last(t, te, nu)], 0, 0)),
            ],
            out_specs=pl.BlockSpec((te_rows, c), lambda t, te, nu: (t, 0)),
        ),
        compiler_params=pltpu.CompilerParams(dimension_semantics=("arbitrary",)),
        name="moe_experts",
    )(tile_expert, n_used, xs, wg, wu, wd)


def _combine_kernel(start_ref, n8_ref, tot8_ref, ys_ref, x1_ref, rec_ref, g_ref, ya_ref, yb_ref,
                    buf_ref, sem, *, a_tiles):
    i = pl.program_id(0)
    last = pl.num_programs(0) - 1
    slot = i % 2
    rv = buf_ref.shape[1]

    def piece_copy(s, src_row, dst_row):
        return pltpu.make_async_copy(ys_ref.at[pl.ds(src_row, SUBLANES)],
                                     buf_ref.at[s, pl.ds(dst_row, SUBLANES)], sem.at[s])

    def fetch(tile, s):
        _for_each_piece(tile, start_ref, n8_ref,
                        lambda row, src_row: piece_copy(s, src_row, row).start())

    @pl.when(i == 0)
    def _():
        buf_ref[...] = jnp.zeros_like(buf_ref)
        fetch(0, 0)

    @pl.when(i < last)
    def _():
        fetch(i + 1, 1 - slot)

    lax.fori_loop(0, tot8_ref[i], lambda k, c: (piece_copy(slot, 0, 0).wait(), c)[1], 0)
    rec = rec_ref[...]
    col = lax.broadcasted_iota(jnp.int32, (rec.shape[0], rv), 1).astype(F32)
    weigh = (jnp.where(col == rec[:, R_D1:R_D1 + 1], rec[:, R_W1:R_W1 + 1], 0.0)
             + jnp.where(col == rec[:, R_D2:R_D2 + 1], rec[:, R_W2:R_W2 + 1], 0.0))
    moe = jnp.dot(weigh.astype(BF16), buf_ref[slot].astype(BF16), preferred_element_type=F32)
    y = _rms_scale(x1_ref[...] + moe) * g_ref[...]

    @pl.when(i < a_tiles)
    def _():
        ya_ref[...] = y

    @pl.when(i >= a_tiles)
    def _():
        yb_ref[...] = y


def _combine(plan, ys, x1, rec, g, n_a, tm):
    n = x1.shape[0]
    a_tiles = n_a // tm
    rv = _tile_rows(tm)
    row = lambda i, *_: (i, 0)
    return pl.pallas_call(
        functools.partial(_combine_kernel, a_tiles=a_tiles),
        out_shape=[jax.ShapeDtypeStruct((n_a, D_MODEL), F32),
                   jax.ShapeDtypeStruct((n - n_a, D_MODEL), F32)],
        grid_spec=pltpu.PrefetchScalarGridSpec(
            num_scalar_prefetch=3,
            grid=(n // tm,),
            in_specs=[
                pl.BlockSpec(memory_space=pl.ANY),
                pl.BlockSpec((tm, D_MODEL), row),
                pl.BlockSpec((tm, ROUTER_COLS), row),
                pl.BlockSpec((1, D_MODEL), lambda i, *_: (0, 0)),
            ],
            out_specs=[
                pl.BlockSpec((tm, D_MODEL), lambda i, *_: (jnp.minimum(i, a_tiles - 1), 0)),
                pl.BlockSpec((tm, D_MODEL), lambda i, *_: (jnp.maximum(i - a_tiles, 0), 0))],
            scratch_shapes=[pltpu.VMEM((2, rv, D_MODEL), F32), pltpu.SemaphoreType.DMA((2,))],
        ),
        compiler_params=pltpu.CompilerParams(dimension_semantics=("arbitrary",)),
        name="moe_combine",
    )(*plan, ys, x1, rec, g)


def _moe_plan(counts, n, tm, te_rows):
    cnt = counts[:, 0, N_GROUPS:N_GROUPS + N_EXPERTS].astype(jnp.int32)
    n8 = (cnt + SUBLANES - 1) // SUBLANES
    run = n8 * SUBLANES
    tiles = (jnp.sum(run, axis=0) + te_rows - 1) // te_rows
    tile_end = jnp.cumsum(tiles)
    first_row = (tile_end - tiles) * te_rows
    start = first_row + jnp.cumsum(run, axis=0) - run
    pad_n8 = (tiles * te_rows - jnp.sum(run, axis=0)) // SUBLANES
    pad_plan = (first_row + jnp.sum(run, axis=0), pad_n8, jnp.sum(pad_n8, keepdims=True))
    n_tiles = (2 * n + (n // tm) * N_EXPERTS * (SUBLANES - 1) + te_rows - 1) // te_rows + N_EXPERTS
    tile_expert = jnp.minimum(
        jnp.sum(jnp.arange(n_tiles)[:, None] >= tile_end[None, :], axis=1), N_EXPERTS - 1)
    plan = (start.reshape(-1), n8.reshape(-1), jnp.sum(n8, axis=1))
    return (plan, pad_plan, tile_expert.astype(jnp.int32), tile_end[-1:].astype(jnp.int32),
            n_tiles * te_rows)


def _row_tile(n, want):
    tm = min(n, want)
    while n % tm or tm % SUBLANES:
        tm -= 1
    return tm


def _alibi_slopes():
    return jnp.exp2(-8.0 * jnp.arange(1, N_HEADS + 1, dtype=F32) / N_HEADS)


def _prompt_bias():
    qi = jnp.arange(WINDOW)[:, None]
    kj = jnp.arange(2 * WINDOW)[None, :]
    dist = WINDOW + qi - kj
    in_window = (dist >= 0) & (dist < WINDOW)
    valid = jnp.stack([in_window & (kj >= WINDOW), in_window])
    score = -_alibi_slopes()[None, :, None, None] * dist.astype(F32)[None, None]
    return jnp.where(valid[:, None], score, NEG_INF)


def _sample_bias(t, bb):
    q_pos = jnp.tile(jnp.arange(t), bb * GQA)[:, None]
    q_seq = jnp.repeat(jnp.arange(bb), GQA * t)[:, None]
    k_pos = jnp.concatenate([jnp.tile(jnp.arange(WINDOW) - WINDOW, bb),
                             jnp.tile(jnp.arange(t), bb)])[None, :]
    k_seq = jnp.concatenate([jnp.repeat(jnp.arange(bb), WINDOW),
                             jnp.repeat(jnp.arange(bb), t)])[None, :]
    dist = q_pos - k_pos
    valid = (dist >= 0) & (dist < WINDOW) & (q_seq == k_seq)
    slopes = jnp.tile(jnp.repeat(_alibi_slopes().reshape(N_KV_HEADS, GQA), t, axis=1), (1, bb))
    score = -slopes[:, :, None] * dist.astype(F32)[None]
    return jnp.where(valid[None], score, NEG_INF)


def _layer_params(norm_mix_g, w_in, conv_w, conv_b, lru_wa, lru_ba, lru_wx, lru_bx, lru_lambda,
                  attn_sinks, w_lru_out, w_attn_out, w_o, norm_moe_g, w_router_group,
                  w_router_expert, moe_w_gate, moe_w_up, moe_w_down):
    per_pack = GATE_PACK // LRU_BLOCK_DIM

    def pack_diag(w):
        w4 = w.reshape(LRU_WIDTH // GATE_PACK, per_pack, LRU_BLOCK_DIM, LRU_BLOCK_DIM)
        eye = jnp.eye(per_pack, dtype=w.dtype)
        return jnp.einsum('pbij,bc->pbicj', w4, eye).reshape(-1, GATE_PACK, GATE_PACK)

    wbd = jnp.concatenate([pack_diag(lru_wa), pack_diag(lru_wx)], axis=-1).astype(BF16)
    w_router = jnp.concatenate(
        [w_router_group, jnp.transpose(w_router_expert, (1, 0, 2)).reshape(D_MODEL, N_EXPERTS)],
        axis=1)
    w_router = jnp.pad(w_router, ((0, 0), (0, ROUTER_COLS - w_router.shape[1])))
    wr_hi = w_router.astype(BF16)
    wr_lo = (w_router - wr_hi.astype(F32)).astype(BF16)
    wg = moe_w_gate.reshape(N_EXPERTS, D_MODEL, D_EXPERT)
    wu = moe_w_up.reshape(N_EXPERTS, D_MODEL, D_EXPERT)
    wd = moe_w_down.reshape(N_EXPERTS, D_EXPERT, D_MODEL)
    row = lambda v: v.reshape(1, -1)
    return dict(
        norm_mix_g=row(norm_mix_g), w_in=w_in.astype(BF16), conv_w=conv_w, conv_b=row(conv_b),
        wbd=wbd, ba=row(lru_ba), bx=row(lru_bx), lam=row(lru_lambda), sinks=attn_sinks,
        wlo=w_lru_out.astype(BF16), wao=w_attn_out.astype(BF16), wo=w_o.astype(BF16),
        norm_moe_g=row(norm_moe_g), wr_hi=wr_hi, wr_lo=wr_lo, wg=wg, wu=wu, wd=wd)


def _prompt_mixers(x, p):
    bsz, t, _ = x.shape
    assert t % WINDOW == 0 and t >= CONV_W - 1
    n = bsz * t
    x2 = x.reshape(n, D_MODEL)
    xl, q, k, v, gl, ga = _inproj(x2, p['norm_mix_g'], p['w_in'], _row_tile(n, 512))
    tc = _row_tile(t, 256)
    yl, h_last = _lru(xl, jnp.zeros((bsz, SUBLANES, LRU_WIDTH), F32),
                      jnp.zeros((bsz, 1, LRU_WIDTH), F32), p['conv_w'], p['conv_b'], p['wbd'],
                      p['ba'], p['bx'], p['lam'], nseq=bsz, seqlen=t, bb=1, tc=tc)
    k3 = k.reshape(bsz, t, KV_COLS)
    v3 = v.reshape(bsz, t, KV_COLS)
    ya = _attn_prompt(q.reshape(bsz, t, -1), k3, v3, p['sinks'], _prompt_bias())
    new_conv = xl.reshape(bsz, t, LRU_WIDTH)[:, t - (CONV_W - 1):]
    new_k = k3[:, t - WINDOW:].reshape(bsz, WINDOW, N_KV_HEADS, HEAD_DIM)
    new_v = v3[:, t - WINDOW:].reshape(bsz, WINDOW, N_KV_HEADS, HEAD_DIM)
    return ((x2, yl, ya.reshape(n, -1), gl, ga),
            (new_conv, h_last.reshape(bsz, LRU_WIDTH), new_k, new_v))


def _sample_mixers(x, conv_buf, h0, k_buf, v_buf, p):
    bsz, t, _ = x.shape
    assert t % SUBLANES == 0 and CONV_W - 1 <= t <= WINDOW
    n = bsz * t
    x2 = x.reshape(n, D_MODEL)
    xl, q, k, v, gl, ga = _inproj(x2, p['norm_mix_g'], p['w_in'], _row_tile(n, 512))
    bb = _row_tile(bsz, 16)
    prev8 = jnp.pad(conv_buf, ((0, 0), (SUBLANES - (CONV_W - 1), 0), (0, 0)))
    yl, h_last = _lru(xl, prev8, h0.reshape(bsz, 1, LRU_WIDTH), p['conv_w'], p['conv_b'],
                      p['wbd'], p['ba'], p['bx'], p['lam'], nseq=bsz, seqlen=t, bb=bb, tc=t)
    q4 = q.reshape(bsz, t, N_KV_HEADS, GQA, HEAD_DIM).transpose(0, 2, 3, 1, 4).reshape(
        bsz, N_KV_HEADS, GQA * t, HEAD_DIM)
    bb_attn = _row_tile(bsz, SAMPLE_SEQS)
    sink_rows = jnp.tile(jnp.repeat(p['sinks'].reshape(N_KV_HEADS, GQA), t, axis=1),
                         (1, bb_attn))[:, :, None]
    o4, new_k, new_v = _attn_sample(
        q4, k.reshape(bsz, t, KV_COLS), v.reshape(bsz, t, KV_COLS),
        k_buf.reshape(bsz, WINDOW, KV_COLS), v_buf.reshape(bsz, WINDOW, KV_COLS),
        _sample_bias(t, bb_attn), sink_rows, bb_attn)
    ya = o4.reshape(bsz, N_KV_HEADS, GQA, t, HEAD_DIM).transpose(0, 3, 1, 2, 4).reshape(n, -1)
    new_conv = xl.reshape(bsz, t, LRU_WIDTH)[:, t - (CONV_W - 1):]
    shape = (bsz, WINDOW, N_KV_HEADS, HEAD_DIM)
    return ((x2, yl, ya, gl, ga),
            (new_conv, h_last.reshape(bsz, LRU_WIDTH), new_k.reshape(shape), new_v.reshape(shape)))


def _merge_and_moe(acts_a, acts_b, p, norm_final_g):
    n_a, n_b = acts_a[0].shape[0], acts_b[0].shape[0]
    n = n_a + n_b
    tm = _row_tile(math.gcd(n_a, n_b), MERGE_ROWS)
    x1, xn, rec, cnt = _merge(acts_a, acts_b, p, tm)
    plan, pad_plan, tile_expert, n_used, n_rows = _moe_plan(cnt, n, tm, EXPERT_ROWS)
    xs = _dispatch(plan, pad_plan, n_used, xn, rec, n_rows, tm, EXPERT_ROWS)
    ys = _experts(tile_expert, n_used, xs, p['wg'], p['wu'], p['wd'], EXPERT_ROWS)
    return _combine(plan, ys, x1, rec, norm_final_g.reshape(1, -1), n_a, tm)


def kernel(x_prompt, x_sample, state_conv, state_lru_h, cache_win_k, cache_win_v, norm_mix_g, w_in, conv_w, conv_b, lru_wa, lru_ba, lru_wx, lru_bx, lru_lambda, attn_sinks, w_lru_out, w_attn_out, w_o, norm_moe_g, w_router_group, w_router_expert, moe_w_gate, moe_w_up, moe_w_down, norm_final_g):
    depth = w_in.shape[0]
    assert depth == 1, "the final norm is fused into the single layer's MoE kernel"
    p = _layer_params(norm_mix_g[0], w_in[0], conv_w[0], conv_b[0], lru_wa[0], lru_ba[0],
                      lru_wx[0], lru_bx[0], lru_lambda[0], attn_sinks[0], w_lru_out[0],
                      w_attn_out[0], w_o[0], norm_moe_g[0], w_router_group[0],
                      w_router_expert[0], moe_w_gate[0], moe_w_up[0], moe_w_down[0])
    acts_p, (c1, h1, k1, v1) = _prompt_mixers(x_prompt, p)
    acts_s, (c2, h2, k2, v2) = _sample_mixers(x_sample, state_conv[0], state_lru_h[0],
                                              cache_win_k[0], cache_win_v[0], p)
    yp, ys = _merge_and_moe(acts_p, acts_s, p, norm_final_g)
    return (yp.reshape(x_prompt.shape), ys.reshape(x_sample.shape), c1[None], h1[None], k1[None],
            v1[None], c2[None], h2[None], k2[None], v2[None])
```

```python
import functools
import math

import jax
import jax.numpy as jnp
from jax import lax
from jax.experimental import pallas as pl
from jax.experimental.pallas import tpu as pltpu

D_MODEL = 1024
LRU_WIDTH = D_MODEL
LRU_BLOCKS = 16
LRU_BLOCK_DIM = LRU_WIDTH // LRU_BLOCKS
LRU_C = 8.0
CONV_W = 4
N_HEADS = 16
N_KV_HEADS = 4
GQA = N_HEADS // N_KV_HEADS
HEAD_DIM = D_MODEL // N_HEADS
KV_COLS = N_KV_HEADS * HEAD_DIM
WINDOW = 128
N_GROUPS = 4
EXPERTS_PER_GROUP = 8
N_EXPERTS = N_GROUPS * EXPERTS_PER_GROUP
D_EXPERT = D_MODEL // 4
RMS_EPS = 1e-6
NEG_INF = -1e30
IN_COLS = LRU_WIDTH + (N_HEADS + 2 * N_KV_HEADS) * HEAD_DIM + 2 * D_MODEL

SUBLANES = 8
LANES = 128
ROUTER_COLS = LANES
GATE_PACK = 256

MERGE_ROWS = 512
EXPERT_ROWS = 512
PIECE_ROWS = (32, SUBLANES)
ATTN_BLOCKS = 4
SAMPLE_SEQS = 8

F32 = jnp.float32
BF16 = jnp.bfloat16


def _sigmoid(z):
    return 1.0 / (1.0 + jnp.exp(-z))


def _expm1(z):
    u = jnp.exp(z)
    return jnp.where(u == 1.0, z, jnp.where(u == 0.0, -1.0, (u - 1.0) * z / jnp.log(u)))


def _rms_scale(x):
    return x * lax.rsqrt(jnp.mean(x * x, axis=-1, keepdims=True) + RMS_EPS)


def _inproj_kernel(x_ref, g_ref, w_ref, xl_ref, q_ref, k_ref, v_ref, gl_ref, ga_ref):
    xb = (_rms_scale(x_ref[...]) * g_ref[...]).astype(BF16)

    def proj(c0, c1):
        return jnp.dot(xb, w_ref[:, c0:c1], preferred_element_type=F32)

    c0 = LRU_WIDTH
    c1 = c0 + N_HEADS * HEAD_DIM
    c2 = c1 + KV_COLS
    c3 = c2 + KV_COLS
    c4 = c3 + D_MODEL
    xl_ref[...] = proj(0, c0)
    q_ref[...] = (proj(c0, c1) * (HEAD_DIM ** -0.5)).astype(BF16)
    k_ref[...] = proj(c1, c2)
    v_ref[...] = proj(c2, c3)
    gl_ref[...] = _sigmoid(proj(c3, c4)).astype(BF16)
    ga_ref[...] = _sigmoid(proj(c4, IN_COLS)).astype(BF16)


def _inproj(x2, g, w_bf, tm):
    n = x2.shape[0]
    row = lambda i: (i, 0)
    fixed = lambda i: (0, 0)
    outs = [
        jax.ShapeDtypeStruct((n, LRU_WIDTH), F32),
        jax.ShapeDtypeStruct((n, N_HEADS * HEAD_DIM), BF16),
        jax.ShapeDtypeStruct((n, KV_COLS), F32),
        jax.ShapeDtypeStruct((n, KV_COLS), F32),
        jax.ShapeDtypeStruct((n, D_MODEL), BF16),
        jax.ShapeDtypeStruct((n, D_MODEL), BF16),
    ]
    return pl.pallas_call(
        _inproj_kernel,
        out_shape=outs,
        grid=(n // tm,),
        in_specs=[
            pl.BlockSpec((tm, D_MODEL), row),
            pl.BlockSpec((1, D_MODEL), fixed),
            pl.BlockSpec((D_MODEL, IN_COLS), fixed),
        ],
        out_specs=[pl.BlockSpec((tm, o.shape[1]), row) for o in outs],
        compiler_params=pltpu.CompilerParams(dimension_semantics=("parallel",)),
        name="inproj",
    )(x2, g, w_bf)


def _lru_kernel(xl_ref, prev_ref, h0_ref, cw_ref, cb_ref, wbd_ref, ba_ref, bx_ref, lam_ref,
                y_ref, hl_ref, xs_ref, xc_ref, a_ref, b_ref, hc_ref, *, bb, tc):
    t = pl.program_id(1)

    @pl.when(t == 0)
    def _():
        xs_ref[:, 0:SUBLANES, :] = prev_ref[...]
        hc_ref[...] = h0_ref[...]

    @pl.when(t > 0)
    def _():
        xs_ref[:, 0:SUBLANES, :] = xs_ref[:, tc:tc + SUBLANES, :]

    for s in range(bb):
        xs_ref[s, SUBLANES:SUBLANES + tc, :] = xl_ref[s * tc:(s + 1) * tc, :]
    cw = cw_ref[...]
    for s in range(bb):
        acc = cb_ref[...] + xs_ref[s, SUBLANES:SUBLANES + tc, :] * cw[CONV_W - 1:CONV_W, :]
        for j in range(CONV_W - 1):
            off = SUBLANES - (CONV_W - 1) + j
            acc = acc + xs_ref[s, off:off + tc, :] * cw[j:j + 1, :]
        xc_ref[s * tc:(s + 1) * tc, :] = acc

    lam = lam_ref[...]
    softplus_neg_lam = jnp.maximum(-lam, 0.0) + jnp.log1p(jnp.exp(-jnp.abs(lam)))
    for g in range(LRU_WIDTH // GATE_PACK):
        cols = slice(g * GATE_PACK, (g + 1) * GATE_PACK)
        xc = xc_ref[:, cols]
        gates = jnp.dot(xc.astype(BF16), wbd_ref[g], preferred_element_type=F32)
        r = _sigmoid(gates[:, :GATE_PACK] + ba_ref[:, cols])
        ig = _sigmoid(gates[:, GATE_PACK:] + bx_ref[:, cols])
        log_a = (-LRU_C) * r * softplus_neg_lam[:, cols]
        a_ref[:, cols] = jnp.exp(log_a)
        b_ref[:, cols] = jnp.sqrt(-_expm1(2.0 * log_a)) * (ig * xc)

    row = lax.broadcasted_iota(jnp.int32, (SUBLANES, LRU_WIDTH), 0)
    for s in range(bb):
        def group(gi, h8, s=s):
            r0 = pl.multiple_of(s * tc + gi * SUBLANES, SUBLANES)
            a = a_ref[pl.ds(r0, SUBLANES), :]
            b = b_ref[pl.ds(r0, SUBLANES), :]
            for d in (1, 2, 4):
                a_up = jnp.where(row >= d, pltpu.roll(a, d, 0), 1.0)
                b_up = jnp.where(row >= d, pltpu.roll(b, d, 0), 0.0)
                b = a * b_up + b
                a = a * a_up
            h = a * h8 + b
            b_ref[pl.ds(r0, SUBLANES), :] = h
            return jnp.broadcast_to(h[SUBLANES - 1:SUBLANES, :], (SUBLANES, LRU_WIDTH))

        h8 = lax.fori_loop(0, tc // SUBLANES, group,
                           jnp.broadcast_to(hc_ref[s], (SUBLANES, LRU_WIDTH)))
        hc_ref[s] = h8[0:1, :]

    y_ref[...] = b_ref[...].astype(y_ref.dtype)

    @pl.when(t == pl.num_programs(1) - 1)
    def _():
        hl_ref[...] = hc_ref[...]


def _lru(xl2, prev8, h0, cw, cb, wbd, ba, bx, lam, *, nseq, seqlen, bb, tc):
    nt = seqlen // tc
    assert bb == 1 or nt == 1
    rows = bb * tc
    fixed2 = lambda b, t: (0, 0)
    fixed3 = lambda b, t: (0, 0, 0)
    return pl.pallas_call(
        functools.partial(_lru_kernel, bb=bb, tc=tc),
        out_shape=[
            jax.ShapeDtypeStruct((nseq * seqlen, LRU_WIDTH), BF16),
            jax.ShapeDtypeStruct((nseq, 1, LRU_WIDTH), F32),
        ],
        grid=(nseq // bb, nt),
        in_specs=[
            pl.BlockSpec((rows, LRU_WIDTH), lambda b, t: (b * nt + t, 0)),
            pl.BlockSpec((bb, SUBLANES, LRU_WIDTH), lambda b, t: (b, 0, 0)),
            pl.BlockSpec((bb, 1, LRU_WIDTH), lambda b, t: (b, 0, 0)),
            pl.BlockSpec((CONV_W, LRU_WIDTH), fixed2),
            pl.BlockSpec((1, LRU_WIDTH), fixed2),
            pl.BlockSpec((LRU_WIDTH // GATE_PACK, GATE_PACK, 2 * GATE_PACK), fixed3),
            pl.BlockSpec((1, LRU_WIDTH), fixed2),
            pl.BlockSpec((1, LRU_WIDTH), fixed2),
            pl.BlockSpec((1, LRU_WIDTH), fixed2),
        ],
        out_specs=[
            pl.BlockSpec((rows, LRU_WIDTH), lambda b, t: (b * nt + t, 0)),
            pl.BlockSpec((bb, 1, LRU_WIDTH), lambda b, t: (b, 0, 0)),
        ],
        scratch_shapes=[
            pltpu.VMEM((bb, tc + SUBLANES, LRU_WIDTH), F32),
            pltpu.VMEM((rows, LRU_WIDTH), F32),
            pltpu.VMEM((rows, LRU_WIDTH), F32),
            pltpu.VMEM((rows, LRU_WIDTH), F32),
            pltpu.VMEM((bb, 1, LRU_WIDTH), F32),
        ],
        compiler_params=pltpu.CompilerParams(dimension_semantics=("parallel", "arbitrary")),
        name="conv_rglru",
    )(xl2, prev8, h0, cw, cb, wbd, ba, bx, lam)


def _softmax_sink_pv(s, sink, v_bf):
    m = jnp.maximum(jnp.max(s, axis=-1, keepdims=True), sink)
    p = jnp.exp(s - m)
    denom = jnp.sum(p, axis=-1, keepdims=True) + jnp.exp(sink - m)
    o = jnp.dot(p.astype(BF16), v_bf, preferred_element_type=F32)
    return o / denom


def _attn_prompt_kernel(sink_ref, q_ref, kc_ref, kp_ref, vc_ref, vp_ref, bias_ref, o_ref,
                        k_ref, v_ref, *, nblk):
    k_ref[0:WINDOW, :] = kp_ref[0].astype(BF16)
    k_ref[WINDOW:, :] = kc_ref[0].astype(BF16)
    v_ref[0:WINDOW, :] = vp_ref[0].astype(BF16)
    v_ref[WINDOW:, :] = vc_ref[0].astype(BF16)

    def block(j, carry):
        bsel = jnp.where((pl.program_id(1) == 0) & (j == 0), 0, 1)
        r0 = pl.multiple_of(j * WINDOW, WINDOW)
        q = q_ref[0, pl.ds(r0, WINDOW), :]
        kcat = k_ref[pl.ds(r0, 2 * WINDOW), :]
        vcat = v_ref[pl.ds(r0, 2 * WINDOW), :]
        outs = []
        for h in range(N_HEADS):
            kv = slice((h // GQA) * HEAD_DIM, (h // GQA + 1) * HEAD_DIM)
            s = lax.dot_general(q[:, h * HEAD_DIM:(h + 1) * HEAD_DIM], kcat[:, kv],
                                (((1,), (1,)), ((), ())), preferred_element_type=F32)
            s = s + bias_ref[bsel, h]
            outs.append(_softmax_sink_pv(s, sink_ref[h], vcat[:, kv]))
        o_ref[0, pl.ds(r0, WINDOW), :] = jnp.concatenate(outs, axis=-1).astype(o_ref.dtype)
        return carry

    lax.fori_loop(0, nblk, block, 0)


def _attn_prompt(q3, k3, v3, sinks, bias, nblk):
    bsz, t, _ = q3.shape
    rows = nblk * WINDOW
    cur = lambda b, i: (b, i, 0)
    prev = lambda b, i: (b, jnp.maximum(i * nblk - 1, 0), 0)
    return pl.pallas_call(
        functools.partial(_attn_prompt_kernel, nblk=nblk),
        out_shape=jax.ShapeDtypeStruct((bsz, t, N_HEADS * HEAD_DIM), BF16),
        grid=(bsz, t // rows),
        in_specs=[
            pl.BlockSpec(memory_space=pltpu.SMEM),
            pl.BlockSpec((1, rows, N_HEADS * HEAD_DIM), cur),
            pl.BlockSpec((1, rows, KV_COLS), cur),
            pl.BlockSpec((1, WINDOW, KV_COLS), prev),
            pl.BlockSpec((1, rows, KV_COLS), cur),
            pl.BlockSpec((1, WINDOW, KV_COLS), prev),
            pl.BlockSpec((2, N_HEADS, WINDOW, 2 * WINDOW), lambda b, i: (0, 0, 0, 0)),
        ],
        out_specs=pl.BlockSpec((1, rows, N_HEADS * HEAD_DIM), cur),
        scratch_shapes=[pltpu.VMEM((rows + WINDOW, KV_COLS), BF16),
                        pltpu.VMEM((rows + WINDOW, KV_COLS), BF16)],
        compiler_params=pltpu.CompilerParams(dimension_semantics=("parallel", "arbitrary")),
        name="swa_prompt",
    )(sinks, q3, k3, k3, v3, v3, bias)


def _attn_sample_kernel(q_ref, kn_ref, vn_ref, ck_ref, cv_ref, bias_ref, sink_ref,
                        o_ref, nk_ref, nv_ref, *, bb, t):
    keep = WINDOW - t
    for s in range(bb):
        nk_ref[s, 0:keep, :] = ck_ref[s, t:WINDOW, :]
        nk_ref[s, keep:WINDOW, :] = kn_ref[s]
        nv_ref[s, 0:keep, :] = cv_ref[s, t:WINDOW, :]
        nv_ref[s, keep:WINDOW, :] = vn_ref[s]
    kall = jnp.concatenate([ck_ref[...].reshape(bb * WINDOW, KV_COLS),
                            kn_ref[...].reshape(bb * t, KV_COLS)], axis=0).astype(BF16)
    vall = jnp.concatenate([cv_ref[...].reshape(bb * WINDOW, KV_COLS),
                            vn_ref[...].reshape(bb * t, KV_COLS)], axis=0).astype(BF16)
    rows = q_ref.shape[2]
    for kh in range(N_KV_HEADS):
        kv = slice(kh * HEAD_DIM, (kh + 1) * HEAD_DIM)
        qh = q_ref[:, kh].reshape(bb * rows, HEAD_DIM)
        sc = lax.dot_general(qh, kall[:, kv], (((1,), (1,)), ((), ())),
                             preferred_element_type=F32)
        o = _softmax_sink_pv(sc + bias_ref[kh], sink_ref[kh], vall[:, kv])
        o_ref[:, kh] = o.reshape(bb, rows, HEAD_DIM).astype(o_ref.dtype)


def _attn_sample(q4, kn3, vn3, ck3, cv3, bias, sink_rows, bb):
    nseq, _, rows, _ = q4.shape
    t = kn3.shape[1]
    b3 = lambda i: (i, 0, 0)
    b4 = lambda i: (i, 0, 0, 0)
    return pl.pallas_call(
        functools.partial(_attn_sample_kernel, bb=bb, t=t),
        out_shape=[
            jax.ShapeDtypeStruct((nseq, N_KV_HEADS, rows, HEAD_DIM), BF16),
            jax.ShapeDtypeStruct((nseq, WINDOW, KV_COLS), F32),
            jax.ShapeDtypeStruct((nseq, WINDOW, KV_COLS), F32),
        ],
        grid=(nseq // bb,),
        in_specs=[
            pl.BlockSpec((bb, N_KV_HEADS, rows, HEAD_DIM), b4),
            pl.BlockSpec((bb, t, KV_COLS), b3),
            pl.BlockSpec((bb, t, KV_COLS), b3),
            pl.BlockSpec((bb, WINDOW, KV_COLS), b3),
            pl.BlockSpec((bb, WINDOW, KV_COLS), b3),
            pl.BlockSpec((N_KV_HEADS, bb * rows, bb * (WINDOW + t)), lambda i: (0, 0, 0)),
            pl.BlockSpec((N_KV_HEADS, bb * rows, 1), lambda i: (0, 0, 0)),
        ],
        out_specs=[
            pl.BlockSpec((bb, N_KV_HEADS, rows, HEAD_DIM), b4),
            pl.BlockSpec((bb, WINDOW, KV_COLS), b3),
            pl.BlockSpec((bb, WINDOW, KV_COLS), b3),
        ],
        compiler_params=pltpu.CompilerParams(dimension_semantics=("parallel",)),
        name="swa_sample",
    )(q4, kn3, vn3, ck3, cv3, bias, sink_rows)


R_E1, R_E2, R_D1, R_D2, R_W1, R_W2 = range(6)


def _route(logits):
    rows = logits.shape[0]
    lane = lax.broadcasted_iota(jnp.int32, logits.shape, 1)
    big = jnp.int32(ROUTER_COLS)
    gl = jnp.where(lane < N_GROUPS, logits, -jnp.inf)
    gmax = jnp.max(gl, axis=-1, keepdims=True)
    p_grp = 1.0 / jnp.sum(jnp.exp(gl - gmax), axis=-1, keepdims=True)
    g_idx = jnp.min(jnp.where(gl == gmax, lane, big), axis=-1, keepdims=True)
    e_lo = N_GROUPS + EXPERTS_PER_GROUP * g_idx
    el = jnp.where((lane >= e_lo) & (lane < e_lo + EXPERTS_PER_GROUP), logits, -jnp.inf)
    m1 = jnp.max(el, axis=-1, keepdims=True)
    i1 = jnp.min(jnp.where(el == m1, lane, big), axis=-1, keepdims=True)
    el2 = jnp.where(lane == i1, -jnp.inf, el)
    m2 = jnp.max(el2, axis=-1, keepdims=True)
    i2 = jnp.min(jnp.where(el2 == m2, lane, big), axis=-1, keepdims=True)
    e21 = jnp.exp(m2 - m1)
    w1 = p_grp / (1.0 + e21)
    w2 = e21 * w1
    hit1 = lane == i1
    hit2 = lane == i2
    onehot = jnp.where(hit1 | hit2, 1.0, 0.0).astype(BF16)
    r_i = lax.broadcasted_iota(jnp.int32, (rows, rows), 0)
    c_i = lax.broadcasted_iota(jnp.int32, (rows, rows), 1)
    tril = jnp.where(c_i <= r_i, 1.0, 0.0).astype(BF16)
    upto = jnp.dot(tril, onehot, preferred_element_type=F32)
    count = upto[rows - 1:rows, :]
    run8 = jnp.floor((count + (SUBLANES - 1)) * (1.0 / SUBLANES))
    k_i = lax.broadcasted_iota(jnp.int32, (ROUTER_COLS, ROUTER_COLS), 0)
    j_i = lax.broadcasted_iota(jnp.int32, (ROUTER_COLS, ROUTER_COLS), 1)
    before = jnp.where(k_i < j_i, 1.0, 0.0).astype(BF16)
    start8 = jnp.dot(jnp.broadcast_to(run8, (SUBLANES, ROUTER_COLS)).astype(BF16), before,
                     preferred_element_type=F32)[0:1, :]
    dest = SUBLANES * start8 + upto - 1.0
    d1 = jnp.sum(jnp.where(hit1, dest, 0.0), axis=-1, keepdims=True)
    d2 = jnp.sum(jnp.where(hit2, dest, 0.0), axis=-1, keepdims=True)
    rec = jnp.zeros(logits.shape, F32)
    for idx, val in ((R_E1, (i1 - N_GROUPS).astype(F32)), (R_E2, (i2 - N_GROUPS).astype(F32)),
                     (R_D1, d1), (R_D2, d2), (R_W1, w1), (R_W2, w2)):
        rec = jnp.where(lane == idx, val, rec)
    return rec, count


def _merge_kernel(*refs, a_tiles):
    acts_a, acts_b = refs[0:5], refs[5:10]
    wlo_ref, wao_ref, wo_ref, g_ref, wrh_ref, wrl_ref, x1_ref, xn_ref, rec_ref, cnt_ref = refs[10:]

    def tile(x_ref, yl_ref, ya_ref, gl_ref, ga_ref):
        a = jnp.dot(yl_ref[...], wlo_ref[...], preferred_element_type=F32)
        b = jnp.dot(ya_ref[...], wao_ref[...], preferred_element_type=F32)
        merged = gl_ref[...].astype(F32) * a + ga_ref[...].astype(F32) * b
        x1 = x_ref[...] + jnp.dot(merged.astype(BF16), wo_ref[...], preferred_element_type=F32)
        x1_ref[...] = x1
        xn = _rms_scale(x1) * g_ref[...]
        xh = xn.astype(BF16)
        xn_ref[...] = xh
        xlo = (xn - xh.astype(F32)).astype(BF16)
        logits = (jnp.dot(xh, wrh_ref[...], preferred_element_type=F32)
                  + jnp.dot(xlo, wrh_ref[...], preferred_element_type=F32)
                  + jnp.dot(xh, wrl_ref[...], preferred_element_type=F32))
        rec, count = _route(logits)
        rec_ref[...] = rec
        cnt_ref[0] = jnp.broadcast_to(count, (SUBLANES, ROUTER_COLS))

    pl.when(pl.program_id(0) < a_tiles)(lambda: tile(*acts_a))
    pl.when(pl.program_id(0) >= a_tiles)(lambda: tile(*acts_b))


def _merge(acts_a, acts_b, p, tm):
    n_a, n_b = acts_a[0].shape[0], acts_b[0].shape[0]
    n = n_a + n_b
    a_tiles = n_a // tm
    row = lambda i: (i, 0)
    fixed = lambda i: (0, 0)
    act_a = pl.BlockSpec((tm, D_MODEL), lambda i: (jnp.minimum(i, a_tiles - 1), 0))
    act_b = pl.BlockSpec((tm, D_MODEL), lambda i: (jnp.maximum(i - a_tiles, 0), 0))
    wsq = pl.BlockSpec((D_MODEL, D_MODEL), fixed)
    wr = pl.BlockSpec((D_MODEL, ROUTER_COLS), fixed)
    return pl.pallas_call(
        functools.partial(_merge_kernel, a_tiles=a_tiles),
        out_shape=[
            jax.ShapeDtypeStruct((n, D_MODEL), F32),
            jax.ShapeDtypeStruct((n, D_MODEL), BF16),
            jax.ShapeDtypeStruct((n, ROUTER_COLS), F32),
            jax.ShapeDtypeStruct((n // tm, SUBLANES, ROUTER_COLS), F32),
        ],
        grid=(n // tm,),
        in_specs=[act_a] * 5 + [act_b] * 5 + [wsq, wsq, wsq, pl.BlockSpec((1, D_MODEL), fixed),
                                              wr, wr],
        out_specs=[pl.BlockSpec((tm, D_MODEL), row),
                   pl.BlockSpec((tm, D_MODEL), row),
                   pl.BlockSpec((tm, ROUTER_COLS), row),
                   pl.BlockSpec((1, SUBLANES, ROUTER_COLS), lambda i: (i, 0, 0))],
        compiler_params=pltpu.CompilerParams(dimension_semantics=("parallel",)),
        name="merge_router",
    )(*acts_a, *acts_b, p['wlo'], p['wao'], p['wo'], p['norm_moe_g'], p['wr_hi'], p['wr_lo'])


def _tile_rows(tm):
    return -(-(2 * tm + N_EXPERTS * (SUBLANES - 1)) // LANES) * LANES


def _for_each_piece(i, start_ref, n8_ref, fn):
    big, small = PIECE_ROWS
    ratio = big // small

    def per_expert(e, row8):
        n8 = n8_ref[i * N_EXPERTS + e]
        src0 = row8 * SUBLANES
        dst0 = start_ref[i * N_EXPERTS + e]
        n_big = n8 // ratio

        def big_piece(k, carry):
            fn(pl.multiple_of(src0 + k * big, small), pl.multiple_of(dst0 + k * big, small), big)
            return carry

        def small_piece(k, carry):
            off = n_big * big + k * small
            fn(pl.multiple_of(src0 + off, small), pl.multiple_of(dst0 + off, small), small)
            return carry

        lax.fori_loop(0, n_big, big_piece, 0)
        lax.fori_loop(0, n8 - n_big * ratio, small_piece, 0)
        return row8 + n8

    lax.fori_loop(0, N_EXPERTS, per_expert, 0)


def _drain(copy_of, n_big, n_small):
    big, small = PIECE_ROWS
    lax.fori_loop(0, n_big, lambda k, c: (copy_of(big).wait(), c)[1], 0)
    lax.fori_loop(0, n_small, lambda k, c: (copy_of(small).wait(), c)[1], 0)


def _dispatch_kernel(start_ref, n8_ref, nbig_ref, nsmall_ref, pad_start_ref, pad_n8_ref,
                     pad_tot8_ref, used_ref, xn_ref, rec_ref, xs_ref, buf_ref, zero_ref, sem, zsem):
    i = pl.program_id(0)
    slot = i % 2
    rv = buf_ref.shape[1]
    rec_t = jnp.transpose(rec_ref[...])
    row = lax.broadcasted_iota(jnp.int32, (rv, rec_t.shape[1]), 0).astype(F32)
    place = jnp.where((row == rec_t[R_D1:R_D1 + 1, :]) | (row == rec_t[R_D2:R_D2 + 1, :]), 1.0, 0.0)
    buf_ref[slot] = jnp.dot(place.astype(BF16), xn_ref[...], preferred_element_type=F32)

    def piece_copy(s, src_row, dst_row, rows):
        return pltpu.make_async_copy(buf_ref.at[s, pl.ds(src_row, rows)],
                                     xs_ref.at[pl.ds(dst_row, rows)], sem.at[s])

    def drain(s, n_big, n_small):
        _drain(lambda rows: piece_copy(s, 0, 0, rows), n_big, n_small)

    @pl.when(i > 0)
    def _():
        drain(1 - slot, nbig_ref[i - 1], nsmall_ref[i - 1])

    _for_each_piece(i, start_ref, n8_ref,
                    lambda src_row, dst_row, rows: piece_copy(slot, src_row, dst_row, rows).start())

    @pl.when(i == pl.num_programs(0) - 1)
    def _():
        zero_ref[...] = jnp.zeros_like(zero_ref)

        def pad_expert(e, carry):
            def piece(k, c):
                dst_row = pl.multiple_of(pad_start_ref[e] + k * SUBLANES, SUBLANES)
                pltpu.make_async_copy(zero_ref.at[pl.ds(0, SUBLANES)],
                                      xs_ref.at[pl.ds(dst_row, SUBLANES)], sem.at[slot]).start()
                return c

            lax.fori_loop(0, pad_n8_ref[e], piece, 0)
            return carry

        lax.fori_loop(0, N_EXPERTS, pad_expert, 0)

        te_rows = zero_ref.shape[0]

        def tail_copy(t):
            return pltpu.make_async_copy(
                zero_ref, xs_ref.at[pl.ds(pl.multiple_of(t * te_rows, te_rows), te_rows)], zsem)

        n_tiles = xs_ref.shape[0] // te_rows
        lax.fori_loop(used_ref[0], n_tiles, lambda t, c: (tail_copy(t).start(), c)[1], 0)
        drain(slot, nbig_ref[i], nsmall_ref[i] + pad_tot8_ref[0])
        lax.fori_loop(used_ref[0], n_tiles, lambda t, c: (tail_copy(t).wait(), c)[1], 0)


def _dispatch(plan, pad_plan, n_used, xn, rec, n_rows, tm, te_rows):
    n = xn.shape[0]
    rv = _tile_rows(tm)
    row = lambda i, *_: (i, 0)
    return pl.pallas_call(
        _dispatch_kernel,
        out_shape=jax.ShapeDtypeStruct((n_rows, D_MODEL), F32),
        grid_spec=pltpu.PrefetchScalarGridSpec(
            num_scalar_prefetch=8,
            grid=(n // tm,),
            in_specs=[pl.BlockSpec((tm, D_MODEL), row), pl.BlockSpec((tm, ROUTER_COLS), row)],
            out_specs=pl.BlockSpec(memory_space=pl.ANY),
            scratch_shapes=[pltpu.VMEM((2, rv, D_MODEL), F32),
                            pltpu.VMEM((te_rows, D_MODEL), F32), pltpu.SemaphoreType.DMA((2,)),
                            pltpu.SemaphoreType.DMA],
        ),
        compiler_params=pltpu.CompilerParams(dimension_semantics=("arbitrary",)),
        name="moe_dispatch",
    )(*plan, *pad_plan, n_used, xn, rec)


def _experts_kernel(te_ref, nu_ref, xs_ref, wg_ref, wu_ref, wd_ref, ys_ref):
    del te_ref

    @pl.when(pl.program_id(0) < nu_ref[0])
    def _():
        x = xs_ref[...].astype(BF16)
        gate = jnp.dot(x, wg_ref[0].astype(BF16), preferred_element_type=F32)
        up = jnp.dot(x, wu_ref[0].astype(BF16), preferred_element_type=F32)
        h = gate * _sigmoid(gate) * up
        ys_ref[...] = jnp.dot(h.astype(BF16), wd_ref[0].astype(BF16), preferred_element_type=F32)

    @pl.when(pl.program_id(0) >= nu_ref[0])
    def _():
        ys_ref[...] = jnp.zeros_like(ys_ref)


def _experts(tile_expert, n_used, xs, wg, wu, wd, te_rows):
    n_rows, c = xs.shape
    last = lambda t, te, nu: jnp.minimum(t, nu[0] - 1)
    return pl.pallas_call(
        _experts_kernel,
        out_shape=jax.ShapeDtypeStruct((n_rows, c), xs.dtype),
        grid_spec=pltpu.PrefetchScalarGridSpec(
            num_scalar_prefetch=2,
            grid=(n_rows // te_rows,),
            in_specs=[
                pl.BlockSpec((te_rows, c), lambda t, te, nu: (last(t, te, nu), 0)),
                pl.BlockSpec((1, D_MODEL, D_EXPERT), lambda t, te, nu: (te[last(t, te, nu)], 0, 0)),
                pl.BlockSpec((1, D_MODEL, D_EXPERT), lambda t, te, nu: (te[last(t, te, nu)], 0, 0)),
                pl.BlockSpec((1, D_EXPERT, D_MODEL), lambda t, te, nu: (te[last(t, te, nu)], 0, 0)),
            ],
            out_specs=pl.BlockSpec((te_rows, c), lambda t, te, nu: (t, 0)),
        ),
        compiler_params=pltpu.CompilerParams(dimension_semantics=("arbitrary",)),
        name="moe_experts",
    )(tile_expert, n_used, xs, wg, wu, wd)


def _combine_kernel(start_ref, n8_ref, nbig_ref, nsmall_ref, ys_ref, x1_ref, rec_ref, g_ref, ya_ref, yb_ref,
                    buf_ref, sem, *, a_tiles):
    i = pl.program_id(0)
    last = pl.num_programs(0) - 1
    slot = i % 2
    rv = buf_ref.shape[1]

    def piece_copy(s, src_row, dst_row, rows):
        return pltpu.make_async_copy(ys_ref.at[pl.ds(src_row, rows)],
                                     buf_ref.at[s, pl.ds(dst_row, rows)], sem.at[s])

    def fetch(tile, s):
        _for_each_piece(tile, start_ref, n8_ref,
                        lambda row, src_row, rows: piece_copy(s, src_row, row, rows).start())

    @pl.when(i == 0)
    def _():
        buf_ref[...] = jnp.zeros_like(buf_ref)
        fetch(0, 0)

    @pl.when(i < last)
    def _():
        fetch(i + 1, 1 - slot)

    _drain(lambda rows: piece_copy(slot, 0, 0, rows), nbig_ref[i], nsmall_ref[i])
    rec = rec_ref[...]
    col = lax.broadcasted_iota(jnp.int32, (rec.shape[0], rv), 1).astype(F32)
    weigh = (jnp.where(col == rec[:, R_D1:R_D1 + 1], rec[:, R_W1:R_W1 + 1], 0.0)
             + jnp.where(col == rec[:, R_D2:R_D2 + 1], rec[:, R_W2:R_W2 + 1], 0.0))
    moe = jnp.dot(weigh.astype(BF16), buf_ref[slot].astype(BF16), preferred_element_type=F32)
    y = _rms_scale(x1_ref[...] + moe) * g_ref[...]

    @pl.when(i < a_tiles)
    def _():
        ya_ref[...] = y

    @pl.when(i >= a_tiles)
    def _():
        yb_ref[...] = y


def _combine(plan, ys, x1, rec, g, n_a, tm):
    n = x1.shape[0]
    a_tiles = n_a // tm
    rv = _tile_rows(tm)
    row = lambda i, *_: (i, 0)
    return pl.pallas_call(
        functools.partial(_combine_kernel, a_tiles=a_tiles),
        out_shape=[jax.ShapeDtypeStruct((n_a, D_MODEL), F32),
                   jax.ShapeDtypeStruct((n - n_a, D_MODEL), F32)],
        grid_spec=pltpu.PrefetchScalarGridSpec(
            num_scalar_prefetch=4,
            grid=(n // tm,),
            in_specs=[
                pl.BlockSpec(memory_space=pl.ANY),
                pl.BlockSpec((tm, D_MODEL), row),
                pl.BlockSpec((tm, ROUTER_COLS), row),
                pl.BlockSpec((1, D_MODEL), lambda i, *_: (0, 0)),
            ],
            out_specs=[
                pl.BlockSpec((tm, D_MODEL), lambda i, *_: (jnp.minimum(i, a_tiles - 1), 0)),
                pl.BlockSpec((tm, D_MODEL), lambda i, *_: (jnp.maximum(i - a_tiles, 0), 0))],
            scratch_shapes=[pltpu.VMEM((2, rv, D_MODEL), F32), pltpu.SemaphoreType.DMA((2,))],
        ),
        compiler_params=pltpu.CompilerParams(dimension_semantics=("arbitrary",)),
        name="moe_combine",
    )(*plan, ys, x1, rec, g)


def _moe_plan(counts, n, tm, te_rows):
    cnt = counts[:, 0, N_GROUPS:N_GROUPS + N_EXPERTS].astype(jnp.int32)
    n8 = (cnt + SUBLANES - 1) // SUBLANES
    run = n8 * SUBLANES
    tiles = (jnp.sum(run, axis=0) + te_rows - 1) // te_rows
    tile_end = jnp.cumsum(tiles)
    first_row = (tile_end - tiles) * te_rows
    start = first_row + jnp.cumsum(run, axis=0) - run
    pad_n8 = (tiles * te_rows - jnp.sum(run, axis=0)) // SUBLANES
    pad_plan = (first_row + jnp.sum(run, axis=0), pad_n8, jnp.sum(pad_n8, keepdims=True))
    n_tiles = (2 * n + (n // tm) * N_EXPERTS * (SUBLANES - 1) + te_rows - 1) // te_rows + N_EXPERTS
    tile_expert = jnp.minimum(
        jnp.sum(jnp.arange(n_tiles)[:, None] >= tile_end[None, :], axis=1), N_EXPERTS - 1)
    ratio = PIECE_ROWS[0] // PIECE_ROWS[1]
    plan = (start.reshape(-1), n8.reshape(-1), jnp.sum(n8 // ratio, axis=1),
            jnp.sum(n8 % ratio, axis=1))
    return (plan, pad_plan, tile_expert.astype(jnp.int32), tile_end[-1:].astype(jnp.int32),
            n_tiles * te_rows)


def _row_tile(n, want):
    tm = min(n, want)
    while n % tm or tm % SUBLANES:
        tm -= 1
    return tm


def _divisor_at_most(n, want):
    d = min(n, want)
    while n % d:
        d -= 1
    return d


def _alibi_slopes():
    return jnp.exp2(-8.0 * jnp.arange(1, N_HEADS + 1, dtype=F32) / N_HEADS)


def _prompt_bias():
    qi = jnp.arange(WINDOW)[:, None]
    kj = jnp.arange(2 * WINDOW)[None, :]
    dist = WINDOW + qi - kj
    in_window = (dist >= 0) & (dist < WINDOW)
    valid = jnp.stack([in_window & (kj >= WINDOW), in_window])
    score = -_alibi_slopes()[None, :, None, None] * dist.astype(F32)[None, None]
    return jnp.where(valid[:, None], score, NEG_INF)


def _sample_bias(t, bb):
    q_pos = jnp.tile(jnp.arange(t), bb * GQA)[:, None]
    q_seq = jnp.repeat(jnp.arange(bb), GQA * t)[:, None]
    k_pos = jnp.concatenate([jnp.tile(jnp.arange(WINDOW) - WINDOW, bb),
                             jnp.tile(jnp.arange(t), bb)])[None, :]
    k_seq = jnp.concatenate([jnp.repeat(jnp.arange(bb), WINDOW),
                             jnp.repeat(jnp.arange(bb), t)])[None, :]
    dist = q_pos - k_pos
    valid = (dist >= 0) & (dist < WINDOW) & (q_seq == k_seq)
    slopes = jnp.tile(jnp.repeat(_alibi_slopes().reshape(N_KV_HEADS, GQA), t, axis=1), (1, bb))
    score = -slopes[:, :, None] * dist.astype(F32)[None]
    return jnp.where(valid[None], score, NEG_INF)


def _layer_params(norm_mix_g, w_in, conv_w, conv_b, lru_wa, lru_ba, lru_wx, lru_bx, lru_lambda,
                  attn_sinks, w_lru_out, w_attn_out, w_o, norm_moe_g, w_router_group,
                  w_router_expert, moe_w_gate, moe_w_up, moe_w_down):
    per_pack = GATE_PACK // LRU_BLOCK_DIM

    def pack_diag(w):
        w4 = w.reshape(LRU_WIDTH // GATE_PACK, per_pack, LRU_BLOCK_DIM, LRU_BLOCK_DIM)
        eye = jnp.eye(per_pack, dtype=w.dtype)
        return jnp.einsum('pbij,bc->pbicj', w4, eye).reshape(-1, GATE_PACK, GATE_PACK)

    wbd = jnp.concatenate([pack_diag(lru_wa), pack_diag(lru_wx)], axis=-1).astype(BF16)
    w_router = jnp.concatenate(
        [w_router_group, jnp.transpose(w_router_expert, (1, 0, 2)).reshape(D_MODEL, N_EXPERTS)],
        axis=1)
    w_router = jnp.pad(w_router, ((0, 0), (0, ROUTER_COLS - w_router.shape[1])))
    wr_hi = w_router.astype(BF16)
    wr_lo = (w_router - wr_hi.astype(F32)).astype(BF16)
    wg = moe_w_gate.reshape(N_EXPERTS, D_MODEL, D_EXPERT)
    wu = moe_w_up.reshape(N_EXPERTS, D_MODEL, D_EXPERT)
    wd = moe_w_down.reshape(N_EXPERTS, D_EXPERT, D_MODEL)
    row = lambda v: v.reshape(1, -1)
    return dict(
        norm_mix_g=row(norm_mix_g), w_in=w_in.astype(BF16), conv_w=conv_w, conv_b=row(conv_b),
        wbd=wbd, ba=row(lru_ba), bx=row(lru_bx), lam=row(lru_lambda), sinks=attn_sinks,
        wlo=w_lru_out.astype(BF16), wao=w_attn_out.astype(BF16), wo=w_o.astype(BF16),
        norm_moe_g=row(norm_moe_g), wr_hi=wr_hi, wr_lo=wr_lo, wg=wg, wu=wu, wd=wd)


def _prompt_mixers(x, p):
    bsz, t, _ = x.shape
    assert t % WINDOW == 0 and t >= CONV_W - 1
    n = bsz * t
    x2 = x.reshape(n, D_MODEL)
    xl, q, k, v, gl, ga = _inproj(x2, p['norm_mix_g'], p['w_in'], _row_tile(n, 512))
    tc = _row_tile(t, 512)
    yl, h_last = _lru(xl, jnp.zeros((bsz, SUBLANES, LRU_WIDTH), F32),
                      jnp.zeros((bsz, 1, LRU_WIDTH), F32), p['conv_w'], p['conv_b'], p['wbd'],
                      p['ba'], p['bx'], p['lam'], nseq=bsz, seqlen=t, bb=1, tc=tc)
    k3 = k.reshape(bsz, t, KV_COLS)
    v3 = v.reshape(bsz, t, KV_COLS)
    nblk = _divisor_at_most(t // WINDOW, ATTN_BLOCKS)
    ya = _attn_prompt(q.reshape(bsz, t, -1), k3, v3, p['sinks'], _prompt_bias(), nblk)
    new_conv = xl.reshape(bsz, t, LRU_WIDTH)[:, t - (CONV_W - 1):]
    new_k = k3[:, t - WINDOW:].reshape(bsz, WINDOW, N_KV_HEADS, HEAD_DIM)
    new_v = v3[:, t - WINDOW:].reshape(bsz, WINDOW, N_KV_HEADS, HEAD_DIM)
    return ((x2, yl, ya.reshape(n, -1), gl, ga),
            (new_conv, h_last.reshape(bsz, LRU_WIDTH), new_k, new_v))


def _sample_mixers(x, conv_buf, h0, k_buf, v_buf, p):
    bsz, t, _ = x.shape
    assert t % SUBLANES == 0 and CONV_W - 1 <= t <= WINDOW
    n = bsz * t
    x2 = x.reshape(n, D_MODEL)
    xl, q, k, v, gl, ga = _inproj(x2, p['norm_mix_g'], p['w_in'], _row_tile(n, 512))
    bb = _row_tile(bsz, 16)
    prev8 = jnp.pad(conv_buf, ((0, 0), (SUBLANES - (CONV_W - 1), 0), (0, 0)))
    yl, h_last = _lru(xl, prev8, h0.reshape(bsz, 1, LRU_WIDTH), p['conv_w'], p['conv_b'],
                      p['wbd'], p['ba'], p['bx'], p['lam'], nseq=bsz, seqlen=t, bb=bb, tc=t)
    q4 = q.reshape(bsz, t, N_KV_HEADS, GQA, HEAD_DIM).transpose(0, 2, 3, 1, 4).reshape(
        bsz, N_KV_HEADS, GQA * t, HEAD_DIM)
    bb_attn = _row_tile(bsz, SAMPLE_SEQS)
    sink_rows = jnp.tile(jnp.repeat(p['sinks'].reshape(N_KV_HEADS, GQA), t, axis=1),
                         (1, bb_attn))[:, :, None]
    o4, new_k, new_v = _attn_sample(
        q4, k.reshape(bsz, t, KV_COLS), v.reshape(bsz, t, KV_COLS),
        k_buf.reshape(bsz, WINDOW, KV_COLS), v_buf.reshape(bsz, WINDOW, KV_COLS),
        _sample_bias(t, bb_attn), sink_rows, bb_attn)
    ya = o4.reshape(bsz, N_KV_HEADS, GQA, t, HEAD_DIM).transpose(0, 3, 1, 2, 4).reshape(n, -1)
    new_conv = xl.reshape(bsz, t, LRU_WIDTH)[:, t - (CONV_W - 1):]
    shape = (bsz, WINDOW, N_KV_HEADS, HEAD_DIM)
    return ((x2, yl, ya, gl, ga),
            (new_conv, h_last.reshape(bsz, LRU_WIDTH), new_k.reshape(shape), new_v.reshape(shape)))


def _merge_and_moe(acts_a, acts_b, p, norm_final_g):
    n_a, n_b = acts_a[0].shape[0], acts_b[0].shape[0]
    n = n_a + n_b
    tm = _row_tile(math.gcd(n_a, n_b), MERGE_ROWS)
    x1, xn, rec, cnt = _merge(acts_a, acts_b, p, tm)
    plan, pad_plan, tile_expert, n_used, n_rows = _moe_plan(cnt, n, tm, EXPERT_ROWS)
    xs = _dispatch(plan, pad_plan, n_used, xn, rec, n_rows, tm, EXPERT_ROWS)
    ys = _experts(tile_expert, n_used, xs, p['wg'], p['wu'], p['wd'], EXPERT_ROWS)
    return _combine(plan, ys, x1, rec, norm_final_g.reshape(1, -1), n_a, tm)


def kernel(x_prompt, x_sample, state_conv, state_lru_h, cache_win_k, cache_win_v, norm_mix_g, w_in, conv_w, conv_b, lru_wa, lru_ba, lru_wx, lru_bx, lru_lambda, attn_sinks, w_lru_out, w_attn_out, w_o, norm_moe_g, w_router_group, w_router_expert, moe_w_gate, moe_w_up, moe_w_down, norm_final_g):
    depth = w_in.shape[0]
    assert depth == 1, "the final norm is fused into the single layer's MoE kernel"
    p = _layer_params(norm_mix_g[0], w_in[0], conv_w[0], conv_b[0], lru_wa[0], lru_ba[0],
                      lru_wx[0], lru_bx[0], lru_lambda[0], attn_sinks[0], w_lru_out[0],
                      w_attn_out[0], w_o[0], norm_moe_g[0], w_router_group[0],
                      w_router_expert[0], moe_w_gate[0], moe_w_up[0], moe_w_down[0])
    acts_p, (c1, h1, k1, v1) = _prompt_mixers(x_prompt, p)
    acts_s, (c2, h2, k2, v2) = _sample_mixers(x_sample, state_conv[0], state_lru_h[0],
                                              cache_win_k[0], cache_win_v[0], p)
    yp, ys = _merge_and_moe(acts_p, acts_s, p, norm_final_g)
    return (yp.reshape(x_prompt.shape), ys.reshape(x_sample.shape), c1[None], h1[None], k1[None],
            v1[None], c2[None], h2[None], k2[None], v2[None])
```

```python
import functools
import math

import jax
import jax.numpy as jnp
from jax import lax
from jax.experimental import pallas as pl
from jax.experimental.pallas import tpu as pltpu

D_MODEL = 1024
LRU_WIDTH = D_MODEL
LRU_BLOCKS = 16
LRU_BLOCK_DIM = LRU_WIDTH // LRU_BLOCKS
LRU_C = 8.0
CONV_W = 4
N_HEADS = 16
N_KV_HEADS = 4
GQA = N_HEADS // N_KV_HEADS
HEAD_DIM = D_MODEL // N_HEADS
KV_COLS = N_KV_HEADS * HEAD_DIM
WINDOW = 128
N_GROUPS = 4
EXPERTS_PER_GROUP = 8
N_EXPERTS = N_GROUPS * EXPERTS_PER_GROUP
D_EXPERT = D_MODEL // 4
RMS_EPS = 1e-6
NEG_INF = -1e30
IN_COLS = LRU_WIDTH + (N_HEADS + 2 * N_KV_HEADS) * HEAD_DIM + 2 * D_MODEL

SUBLANES = 8
LANES = 128
ROUTER_COLS = LANES
GATE_PACK = 256

MERGE_ROWS = 512
EXPERT_ROWS = 512
PIECE_ROWS = (32, SUBLANES)
ATTN_BLOCKS = 4
DISPATCH_SLOTS = 3
SAMPLE_SEQS = 8

F32 = jnp.float32
BF16 = jnp.bfloat16


def _sigmoid(z):
    return 1.0 / (1.0 + jnp.exp(-z))


def _one_minus_exp2(x, ex):
    return -jnp.tanh(x) * (ex * ex + 1.0)


def _rms_scale(x):
    return x * lax.rsqrt(jnp.mean(x * x, axis=-1, keepdims=True) + RMS_EPS)


def _inproj_kernel(x_ref, g_ref, w_ref, xl_ref, q_ref, k_ref, v_ref, gl_ref, ga_ref):
    xb = (_rms_scale(x_ref[...]) * g_ref[...]).astype(BF16)

    def proj(c0, c1):
        return jnp.dot(xb, w_ref[:, c0:c1], preferred_element_type=F32)

    c0 = LRU_WIDTH
    c1 = c0 + N_HEADS * HEAD_DIM
    c2 = c1 + KV_COLS
    c3 = c2 + KV_COLS
    c4 = c3 + D_MODEL
    xl_ref[...] = proj(0, c0)
    q_ref[...] = (proj(c0, c1) * (HEAD_DIM ** -0.5)).astype(BF16)
    k_ref[...] = proj(c1, c2)
    v_ref[...] = proj(c2, c3)
    gl_ref[...] = _sigmoid(proj(c3, c4)).astype(BF16)
    ga_ref[...] = _sigmoid(proj(c4, IN_COLS)).astype(BF16)


def _inproj(x2, g, w_bf, tm):
    n = x2.shape[0]
    row = lambda i: (i, 0)
    fixed = lambda i: (0, 0)
    outs = [
        jax.ShapeDtypeStruct((n, LRU_WIDTH), F32),
        jax.ShapeDtypeStruct((n, N_HEADS * HEAD_DIM), BF16),
        jax.ShapeDtypeStruct((n, KV_COLS), F32),
        jax.ShapeDtypeStruct((n, KV_COLS), F32),
        jax.ShapeDtypeStruct((n, D_MODEL), BF16),
        jax.ShapeDtypeStruct((n, D_MODEL), BF16),
    ]
    return pl.pallas_call(
        _inproj_kernel,
        out_shape=outs,
        grid=(n // tm,),
        in_specs=[
            pl.BlockSpec((tm, D_MODEL), row),
            pl.BlockSpec((1, D_MODEL), fixed),
            pl.BlockSpec((D_MODEL, IN_COLS), fixed),
        ],
        out_specs=[pl.BlockSpec((tm, o.shape[1]), row) for o in outs],
        compiler_params=pltpu.CompilerParams(dimension_semantics=("parallel",)),
        name="inproj",
    )(x2, g, w_bf)


def _lru_kernel(xl_ref, prev_ref, h0_ref, cw_ref, cb_ref, wbd_ref, ba_ref, bx_ref, lam_ref,
                y_ref, hl_ref, xs_ref, xc_ref, a_ref, b_ref, hc_ref, *, bb, tc):
    t = pl.program_id(1)

    @pl.when(t == 0)
    def _():
        xs_ref[:, 0:SUBLANES, :] = prev_ref[...]
        hc_ref[...] = h0_ref[...]

    @pl.when(t > 0)
    def _():
        xs_ref[:, 0:SUBLANES, :] = xs_ref[:, tc:tc + SUBLANES, :]

    for s in range(bb):
        xs_ref[s, SUBLANES:SUBLANES + tc, :] = xl_ref[s * tc:(s + 1) * tc, :]
    cw = cw_ref[...]
    for s in range(bb):
        acc = cb_ref[...] + xs_ref[s, SUBLANES:SUBLANES + tc, :] * cw[CONV_W - 1:CONV_W, :]
        for j in range(CONV_W - 1):
            off = SUBLANES - (CONV_W - 1) + j
            acc = acc + xs_ref[s, off:off + tc, :] * cw[j:j + 1, :]
        xc_ref[s * tc:(s + 1) * tc, :] = acc

    lam = lam_ref[...]
    softplus_neg_lam = jnp.maximum(-lam, 0.0) + jnp.log1p(jnp.exp(-jnp.abs(lam)))
    for g in range(LRU_WIDTH // GATE_PACK):
        cols = slice(g * GATE_PACK, (g + 1) * GATE_PACK)
        xc = xc_ref[:, cols]
        gates = jnp.dot(xc.astype(BF16), wbd_ref[g], preferred_element_type=F32)
        r = _sigmoid(gates[:, :GATE_PACK] + ba_ref[:, cols])
        ig = _sigmoid(gates[:, GATE_PACK:] + bx_ref[:, cols])
        log_a = (-LRU_C) * r * softplus_neg_lam[:, cols]
        a = jnp.exp(log_a)
        a_ref[:, cols] = a
        b_ref[:, cols] = jnp.sqrt(_one_minus_exp2(log_a, a)) * (ig * xc)

    row = lax.broadcasted_iota(jnp.int32, (SUBLANES, LRU_WIDTH), 0)
    for s in range(bb):
        def group(gi, h8, s=s):
            r0 = pl.multiple_of(s * tc + gi * SUBLANES, SUBLANES)
            a = a_ref[pl.ds(r0, SUBLANES), :]
            b = b_ref[pl.ds(r0, SUBLANES), :]
            for d in (1, 2, 4):
                a_up = jnp.where(row >= d, pltpu.roll(a, d, 0), 1.0)
                b_up = jnp.where(row >= d, pltpu.roll(b, d, 0), 0.0)
                b = a * b_up + b
                a = a * a_up
            h = a * h8 + b
            b_ref[pl.ds(r0, SUBLANES), :] = h
            return jnp.broadcast_to(h[SUBLANES - 1:SUBLANES, :], (SUBLANES, LRU_WIDTH))

        h8 = lax.fori_loop(0, tc // SUBLANES, group,
                           jnp.broadcast_to(hc_ref[s], (SUBLANES, LRU_WIDTH)))
        hc_ref[s] = h8[0:1, :]

    y_ref[...] = b_ref[...].astype(y_ref.dtype)

    @pl.when(t == pl.num_programs(1) - 1)
    def _():
        hl_ref[...] = hc_ref[...]


def _lru(xl2, prev8, h0, cw, cb, wbd, ba, bx, lam, *, nseq, seqlen, bb, tc):
    nt = seqlen // tc
    assert bb == 1 or nt == 1
    rows = bb * tc
    fixed2 = lambda b, t: (0, 0)
    fixed3 = lambda b, t: (0, 0, 0)
    return pl.pallas_call(
        functools.partial(_lru_kernel, bb=bb, tc=tc),
        out_shape=[
            jax.ShapeDtypeStruct((nseq * seqlen, LRU_WIDTH), BF16),
            jax.ShapeDtypeStruct((nseq, 1, LRU_WIDTH), F32),
        ],
        grid=(nseq // bb, nt),
        in_specs=[
            pl.BlockSpec((rows, LRU_WIDTH), lambda b, t: (b * nt + t, 0)),
            pl.BlockSpec((bb, SUBLANES, LRU_WIDTH), lambda b, t: (b, 0, 0)),
            pl.BlockSpec((bb, 1, LRU_WIDTH), lambda b, t: (b, 0, 0)),
            pl.BlockSpec((CONV_W, LRU_WIDTH), fixed2),
            pl.BlockSpec((1, LRU_WIDTH), fixed2),
            pl.BlockSpec((LRU_WIDTH // GATE_PACK, GATE_PACK, 2 * GATE_PACK), fixed3),
            pl.BlockSpec((1, LRU_WIDTH), fixed2),
            pl.BlockSpec((1, LRU_WIDTH), fixed2),
            pl.BlockSpec((1, LRU_WIDTH), fixed2),
        ],
        out_specs=[
            pl.BlockSpec((rows, LRU_WIDTH), lambda b, t: (b * nt + t, 0)),
            pl.BlockSpec((bb, 1, LRU_WIDTH), lambda b, t: (b, 0, 0)),
        ],
        scratch_shapes=[
            pltpu.VMEM((bb, tc + SUBLANES, LRU_WIDTH), F32),
            pltpu.VMEM((rows, LRU_WIDTH), F32),
            pltpu.VMEM((rows, LRU_WIDTH), F32),
            pltpu.VMEM((rows, LRU_WIDTH), F32),
            pltpu.VMEM((bb, 1, LRU_WIDTH), F32),
        ],
        compiler_params=pltpu.CompilerParams(dimension_semantics=("parallel", "arbitrary")),
        name="conv_rglru",
    )(xl2, prev8, h0, cw, cb, wbd, ba, bx, lam)


def _softmax_sink_pv(s, sink, v_bf):
    m = jnp.maximum(jnp.max(s, axis=-1, keepdims=True), sink)
    p = jnp.exp(s - m)
    denom = jnp.sum(p, axis=-1, keepdims=True) + jnp.exp(sink - m)
    o = jnp.dot(p.astype(BF16), v_bf, preferred_element_type=F32)
    return o / denom


def _attn_prompt_kernel(sink_ref, q_ref, kc_ref, kp_ref, vc_ref, vp_ref, bias_ref, o_ref,
                        k_ref, v_ref, *, nblk):
    k_ref[0:WINDOW, :] = kp_ref[0].astype(BF16)
    k_ref[WINDOW:, :] = kc_ref[0].astype(BF16)
    v_ref[0:WINDOW, :] = vp_ref[0].astype(BF16)
    v_ref[WINDOW:, :] = vc_ref[0].astype(BF16)

    def block(j, carry):
        bsel = jnp.where((pl.program_id(1) == 0) & (j == 0), 0, 1)
        r0 = pl.multiple_of(j * WINDOW, WINDOW)
        q = q_ref[0, pl.ds(r0, WINDOW), :]
        kcat = k_ref[pl.ds(r0, 2 * WINDOW), :]
        vcat = v_ref[pl.ds(r0, 2 * WINDOW), :]
        outs = []
        for h in range(N_HEADS):
            kv = slice((h // GQA) * HEAD_DIM, (h // GQA + 1) * HEAD_DIM)
            s = lax.dot_general(q[:, h * HEAD_DIM:(h + 1) * HEAD_DIM], kcat[:, kv],
                                (((1,), (1,)), ((), ())), preferred_element_type=F32)
            s = s + bias_ref[bsel, h]
            outs.append(_softmax_sink_pv(s, sink_ref[h], vcat[:, kv]))
        o_ref[0, pl.ds(r0, WINDOW), :] = jnp.concatenate(outs, axis=-1).astype(o_ref.dtype)
        return carry

    lax.fori_loop(0, nblk, block, 0)


def _attn_prompt(q3, k3, v3, sinks, bias, nblk):
    bsz, t, _ = q3.shape
    rows = nblk * WINDOW
    cur = lambda b, i: (b, i, 0)
    prev = lambda b, i: (b, jnp.maximum(i * nblk - 1, 0), 0)
    return pl.pallas_call(
        functools.partial(_attn_prompt_kernel, nblk=nblk),
        out_shape=jax.ShapeDtypeStruct((bsz, t, N_HEADS * HEAD_DIM), BF16),
        grid=(bsz, t // rows),
        in_specs=[
            pl.BlockSpec(memory_space=pltpu.SMEM),
            pl.BlockSpec((1, rows, N_HEADS * HEAD_DIM), cur),
            pl.BlockSpec((1, rows, KV_COLS), cur),
            pl.BlockSpec((1, WINDOW, KV_COLS), prev),
            pl.BlockSpec((1, rows, KV_COLS), cur),
            pl.BlockSpec((1, WINDOW, KV_COLS), prev),
            pl.BlockSpec((2, N_HEADS, WINDOW, 2 * WINDOW), lambda b, i: (0, 0, 0, 0)),
        ],
        out_specs=pl.BlockSpec((1, rows, N_HEADS * HEAD_DIM), cur),
        scratch_shapes=[pltpu.VMEM((rows + WINDOW, KV_COLS), BF16),
                        pltpu.VMEM((rows + WINDOW, KV_COLS), BF16)],
        compiler_params=pltpu.CompilerParams(dimension_semantics=("parallel", "arbitrary")),
        name="swa_prompt",
    )(sinks, q3, k3, k3, v3, v3, bias)


def _attn_sample_kernel(q_ref, kn_ref, vn_ref, ck_ref, cv_ref, bias_ref, sink_ref,
                        o_ref, nk_ref, nv_ref, *, bb, t):
    keep = WINDOW - t
    for s in range(bb):
        nk_ref[s, 0:keep, :] = ck_ref[s, t:WINDOW, :]
        nk_ref[s, keep:WINDOW, :] = kn_ref[s]
        nv_ref[s, 0:keep, :] = cv_ref[s, t:WINDOW, :]
        nv_ref[s, keep:WINDOW, :] = vn_ref[s]
    kall = jnp.concatenate([ck_ref[...].reshape(bb * WINDOW, KV_COLS),
                            kn_ref[...].reshape(bb * t, KV_COLS)], axis=0).astype(BF16)
    vall = jnp.concatenate([cv_ref[...].reshape(bb * WINDOW, KV_COLS),
                            vn_ref[...].reshape(bb * t, KV_COLS)], axis=0).astype(BF16)
    rows = q_ref.shape[2]
    for kh in range(N_KV_HEADS):
        kv = slice(kh * HEAD_DIM, (kh + 1) * HEAD_DIM)
        qh = q_ref[:, kh].reshape(bb * rows, HEAD_DIM)
        sc = lax.dot_general(qh, kall[:, kv], (((1,), (1,)), ((), ())),
                             preferred_element_type=F32)
        o = _softmax_sink_pv(sc + bias_ref[kh], sink_ref[kh], vall[:, kv])
        o_ref[:, kh] = o.reshape(bb, rows, HEAD_DIM).astype(o_ref.dtype)


def _attn_sample(q4, kn3, vn3, ck3, cv3, bias, sink_rows, bb):
    nseq, _, rows, _ = q4.shape
    t = kn3.shape[1]
    b3 = lambda i: (i, 0, 0)
    b4 = lambda i: (i, 0, 0, 0)
    return pl.pallas_call(
        functools.partial(_attn_sample_kernel, bb=bb, t=t),
        out_shape=[
            jax.ShapeDtypeStruct((nseq, N_KV_HEADS, rows, HEAD_DIM), BF16),
            jax.ShapeDtypeStruct((nseq, WINDOW, KV_COLS), F32),
            jax.ShapeDtypeStruct((nseq, WINDOW, KV_COLS), F32),
        ],
        grid=(nseq // bb,),
        in_specs=[
            pl.BlockSpec((bb, N_KV_HEADS, rows, HEAD_DIM), b4),
            pl.BlockSpec((bb, t, KV_COLS), b3),
            pl.BlockSpec((bb, t, KV_COLS), b3),
            pl.BlockSpec((bb, WINDOW, KV_COLS), b3),
            pl.BlockSpec((bb, WINDOW, KV_COLS), b3),
            pl.BlockSpec((N_KV_HEADS, bb * rows, bb * (WINDOW + t)), lambda i: (0, 0, 0)),
            pl.BlockSpec((N_KV_HEADS, bb * rows, 1), lambda i: (0, 0, 0)),
        ],
        out_specs=[
            pl.BlockSpec((bb, N_KV_HEADS, rows, HEAD_DIM), b4),
            pl.BlockSpec((bb, WINDOW, KV_COLS), b3),
            pl.BlockSpec((bb, WINDOW, KV_COLS), b3),
        ],
        compiler_params=pltpu.CompilerParams(dimension_semantics=("parallel",)),
        name="swa_sample",
    )(q4, kn3, vn3, ck3, cv3, bias, sink_rows)


R_E1, R_E2, R_D1, R_D2, R_W1, R_W2 = range(6)


def _route(logits):
    rows = logits.shape[0]
    lane = lax.broadcasted_iota(jnp.int32, logits.shape, 1)
    big = jnp.int32(ROUTER_COLS)
    gl = jnp.where(lane < N_GROUPS, logits, -jnp.inf)
    gmax = jnp.max(gl, axis=-1, keepdims=True)
    p_grp = 1.0 / jnp.sum(jnp.exp(gl - gmax), axis=-1, keepdims=True)
    g_idx = jnp.min(jnp.where(gl == gmax, lane, big), axis=-1, keepdims=True)
    e_lo = N_GROUPS + EXPERTS_PER_GROUP * g_idx
    el = jnp.where((lane >= e_lo) & (lane < e_lo + EXPERTS_PER_GROUP), logits, -jnp.inf)
    m1 = jnp.max(el, axis=-1, keepdims=True)
    i1 = jnp.min(jnp.where(el == m1, lane, big), axis=-1, keepdims=True)
    el2 = jnp.where(lane == i1, -jnp.inf, el)
    m2 = jnp.max(el2, axis=-1, keepdims=True)
    i2 = jnp.min(jnp.where(el2 == m2, lane, big), axis=-1, keepdims=True)
    e21 = jnp.exp(m2 - m1)
    w1 = p_grp / (1.0 + e21)
    w2 = e21 * w1
    hit1 = lane == i1
    hit2 = lane == i2
    onehot = jnp.where(hit1 | hit2, 1.0, 0.0).astype(BF16)
    r_i = lax.broadcasted_iota(jnp.int32, (rows, rows), 0)
    c_i = lax.broadcasted_iota(jnp.int32, (rows, rows), 1)
    tril = jnp.where(c_i <= r_i, 1.0, 0.0).astype(BF16)
    upto = jnp.dot(tril, onehot, preferred_element_type=F32)
    count = upto[rows - 1:rows, :]
    run8 = jnp.floor((count + (SUBLANES - 1)) * (1.0 / SUBLANES))
    k_i = lax.broadcasted_iota(jnp.int32, (ROUTER_COLS, ROUTER_COLS), 0)
    j_i = lax.broadcasted_iota(jnp.int32, (ROUTER_COLS, ROUTER_COLS), 1)
    before = jnp.where(k_i < j_i, 1.0, 0.0).astype(BF16)
    start8 = jnp.dot(jnp.broadcast_to(run8, (SUBLANES, ROUTER_COLS)).astype(BF16), before,
                     preferred_element_type=F32)[0:1, :]
    dest = SUBLANES * start8 + upto - 1.0
    d1 = jnp.sum(jnp.where(hit1, dest, 0.0), axis=-1, keepdims=True)
    d2 = jnp.sum(jnp.where(hit2, dest, 0.0), axis=-1, keepdims=True)
    rec = jnp.zeros(logits.shape, F32)
    for idx, val in ((R_E1, (i1 - N_GROUPS).astype(F32)), (R_E2, (i2 - N_GROUPS).astype(F32)),
                     (R_D1, d1), (R_D2, d2), (R_W1, w1), (R_W2, w2)):
        rec = jnp.where(lane == idx, val, rec)
    return rec, count


def _merge_kernel(*refs, a_tiles):
    acts_a, acts_b = refs[0:5], refs[5:10]
    wlo_ref, wao_ref, wo_ref, g_ref, wr_ref, x1_ref, xn_ref, rec_ref, cnt_ref = refs[10:]

    def tile(x_ref, yl_ref, ya_ref, gl_ref, ga_ref):
        a = jnp.dot(yl_ref[...], wlo_ref[...], preferred_element_type=F32)
        b = jnp.dot(ya_ref[...], wao_ref[...], preferred_element_type=F32)
        merged = gl_ref[...].astype(F32) * a + ga_ref[...].astype(F32) * b
        x1 = x_ref[...] + jnp.dot(merged.astype(BF16), wo_ref[...], preferred_element_type=F32)
        x1_ref[...] = x1
        xn = _rms_scale(x1) * g_ref[...]
        xh = xn.astype(BF16)
        xn_ref[...] = xh
        xlo = (xn - xh.astype(F32)).astype(BF16)
        both = jnp.dot(xh, wr_ref[...], preferred_element_type=F32)
        logits = (both[:, :ROUTER_COLS] + both[:, ROUTER_COLS:]
                  + jnp.dot(xlo, wr_ref[:, :ROUTER_COLS], preferred_element_type=F32))
        rec, count = _route(logits)
        rec_ref[...] = rec
        cnt_ref[0] = jnp.broadcast_to(count, (SUBLANES, ROUTER_COLS))

    pl.when(pl.program_id(0) < a_tiles)(lambda: tile(*acts_a))
    pl.when(pl.program_id(0) >= a_tiles)(lambda: tile(*acts_b))


def _merge(acts_a, acts_b, p, tm):
    n_a, n_b = acts_a[0].shape[0], acts_b[0].shape[0]
    n = n_a + n_b
    a_tiles = n_a // tm
    row = lambda i: (i, 0)
    fixed = lambda i: (0, 0)
    act_a = pl.BlockSpec((tm, D_MODEL), lambda i: (jnp.minimum(i, a_tiles - 1), 0))
    act_b = pl.BlockSpec((tm, D_MODEL), lambda i: (jnp.maximum(i - a_tiles, 0), 0))
    wsq = pl.BlockSpec((D_MODEL, D_MODEL), fixed)
    wr = pl.BlockSpec((D_MODEL, 2 * ROUTER_COLS), fixed)
    return pl.pallas_call(
        functools.partial(_merge_kernel, a_tiles=a_tiles),
        out_shape=[
            jax.ShapeDtypeStruct((n, D_MODEL), F32),
            jax.ShapeDtypeStruct((n, D_MODEL), BF16),
            jax.ShapeDtypeStruct((n, ROUTER_COLS), F32),
            jax.ShapeDtypeStruct((n // tm, SUBLANES, ROUTER_COLS), F32),
        ],
        grid=(n // tm,),
        in_specs=[act_a] * 5 + [act_b] * 5 + [wsq, wsq, wsq, pl.BlockSpec((1, D_MODEL), fixed), wr],
        out_specs=[pl.BlockSpec((tm, D_MODEL), row),
                   pl.BlockSpec((tm, D_MODEL), row),
                   pl.BlockSpec((tm, ROUTER_COLS), row),
                   pl.BlockSpec((1, SUBLANES, ROUTER_COLS), lambda i: (i, 0, 0))],
        compiler_params=pltpu.CompilerParams(dimension_semantics=("parallel",)),
        name="merge_router",
    )(*acts_a, *acts_b, p['wlo'], p['wao'], p['wo'], p['norm_moe_g'], p['wr'])


def _tile_rows(tm):
    return -(-(2 * tm + N_EXPERTS * (SUBLANES - 1)) // LANES) * LANES


def _for_each_piece(i, start_ref, n8_ref, fn):
    big, small = PIECE_ROWS
    ratio = big // small

    def per_expert(e, row8):
        n8 = n8_ref[i * N_EXPERTS + e]
        src0 = row8 * SUBLANES
        dst0 = start_ref[i * N_EXPERTS + e]
        n_big = n8 // ratio

        def big_piece(k, carry):
            fn(pl.multiple_of(src0 + k * big, small), pl.multiple_of(dst0 + k * big, small), big)
            return carry

        def small_piece(k, carry):
            off = n_big * big + k * small
            fn(pl.multiple_of(src0 + off, small), pl.multiple_of(dst0 + off, small), small)
            return carry

        lax.fori_loop(0, n_big, big_piece, 0)
        lax.fori_loop(0, n8 - n_big * ratio, small_piece, 0)
        return row8 + n8

    lax.fori_loop(0, N_EXPERTS, per_expert, 0)


def _drain(copy_of, n_big, n_small):
    big, small = PIECE_ROWS
    lax.fori_loop(0, n_big, lambda k, c: (copy_of(big).wait(), c)[1], 0)
    lax.fori_loop(0, n_small, lambda k, c: (copy_of(small).wait(), c)[1], 0)


def _dispatch_kernel(start_ref, n8_ref, nbig_ref, nsmall_ref, pad_start_ref, pad_n8_ref,
                     pad_tot8_ref, used_ref, xn_ref, rec_ref, xs_ref, buf_ref, zero_ref, sem, zsem):
    i = pl.program_id(0)
    nslot = buf_ref.shape[0]
    slot = i % nslot
    rv = buf_ref.shape[1]
    rec_t = jnp.transpose(rec_ref[...])
    row = lax.broadcasted_iota(jnp.int32, (rv, rec_t.shape[1]), 0).astype(F32)
    place = jnp.where((row == rec_t[R_D1:R_D1 + 1, :]) | (row == rec_t[R_D2:R_D2 + 1, :]), 1.0, 0.0)
    buf_ref[slot] = jnp.dot(place.astype(BF16), xn_ref[...], preferred_element_type=F32)

    def piece_copy(s, src_row, dst_row, rows):
        return pltpu.make_async_copy(buf_ref.at[s, pl.ds(src_row, rows)],
                                     xs_ref.at[pl.ds(dst_row, rows)], sem.at[s])

    def drain(s, n_big, n_small):
        _drain(lambda rows: piece_copy(s, 0, 0, rows), n_big, n_small)

    @pl.when(i >= nslot - 1)
    def _():
        drain((i + 1) % nslot, nbig_ref[i - (nslot - 1)], nsmall_ref[i - (nslot - 1)])

    _for_each_piece(i, start_ref, n8_ref,
                    lambda src_row, dst_row, rows: piece_copy(slot, src_row, dst_row, rows).start())

    @pl.when(i == pl.num_programs(0) - 1)
    def _():
        zero_ref[...] = jnp.zeros_like(zero_ref)

        def pad_expert(e, carry):
            def piece(k, c):
                dst_row = pl.multiple_of(pad_start_ref[e] + k * SUBLANES, SUBLANES)
                pltpu.make_async_copy(zero_ref.at[pl.ds(0, SUBLANES)],
                                      xs_ref.at[pl.ds(dst_row, SUBLANES)], sem.at[slot]).start()
                return c

            lax.fori_loop(0, pad_n8_ref[e], piece, 0)
            return carry

        lax.fori_loop(0, N_EXPERTS, pad_expert, 0)

        te_rows = zero_ref.shape[0]

        def tail_copy(t):
            return pltpu.make_async_copy(
                zero_ref, xs_ref.at[pl.ds(pl.multiple_of(t * te_rows, te_rows), te_rows)], zsem)

        n_tiles = xs_ref.shape[0] // te_rows
        lax.fori_loop(used_ref[0], n_tiles, lambda t, c: (tail_copy(t).start(), c)[1], 0)
        for back in range(1, nslot - 1):
            @pl.when(i >= back)
            def _(back=back):
                drain((i - back) % nslot, nbig_ref[i - back], nsmall_ref[i - back])
        drain(slot, nbig_ref[i], nsmall_ref[i] + pad_tot8_ref[0])
        lax.fori_loop(used_ref[0], n_tiles, lambda t, c: (tail_copy(t).wait(), c)[1], 0)


def _dispatch(plan, pad_plan, n_used, xn, rec, n_rows, tm, te_rows):
    n = xn.shape[0]
    rv = _tile_rows(tm)
    row = lambda i, *_: (i, 0)
    return pl.pallas_call(
        _dispatch_kernel,
        out_shape=jax.ShapeDtypeStruct((n_rows, D_MODEL), F32),
        grid_spec=pltpu.PrefetchScalarGridSpec(
            num_scalar_prefetch=8,
            grid=(n // tm,),
            in_specs=[pl.BlockSpec((tm, D_MODEL), row), pl.BlockSpec((tm, ROUTER_COLS), row)],
            out_specs=pl.BlockSpec(memory_space=pl.ANY),
            scratch_shapes=[pltpu.VMEM((DISPATCH_SLOTS, rv, D_MODEL), F32),
                            pltpu.VMEM((te_rows, D_MODEL), F32),
                            pltpu.SemaphoreType.DMA((DISPATCH_SLOTS,)), pltpu.SemaphoreType.DMA],
        ),
        compiler_params=pltpu.CompilerParams(dimension_semantics=("arbitrary",)),
        name="moe_dispatch",
    )(*plan, *pad_plan, n_used, xn, rec)


def _experts_kernel(te_ref, nu_ref, xs_ref, wg_ref, wu_ref, wd_ref, ys_ref):
    del te_ref

    @pl.when(pl.program_id(0) < nu_ref[0])
    def _():
        x = xs_ref[...].astype(BF16)
        gate = jnp.dot(x, wg_ref[0].astype(BF16), preferred_element_type=F32)
        up = jnp.dot(x, wu_ref[0].astype(BF16), preferred_element_type=F32)
        h = gate * _sigmoid(gate) * up
        ys_ref[...] = jnp.dot(h.astype(BF16), wd_ref[0].astype(BF16), preferred_element_type=F32)

    @pl.when(pl.program_id(0) >= nu_ref[0])
    def _():
        ys_ref[...] = jnp.zeros_like(ys_ref)


def _experts(tile_expert, n_used, xs, wg, wu, wd, te_rows):
    n_rows, c = xs.shape
    last = lambda t, te, nu: jnp.minimum(t, nu[0] - 1)
    return pl.pallas_call(
        _experts_kernel,
        out_shape=jax.ShapeDtypeStruct((n_rows, c), xs.dtype),
        grid_spec=pltpu.PrefetchScalarGridSpec(
            num_scalar_prefetch=2,
            grid=(n_rows // te_rows,),
            in_specs=[
                pl.BlockSpec((te_rows, c), lambda t, te, nu: (last(t, te, nu), 0)),
                pl.BlockSpec((1, D_MODEL, D_EXPERT), lambda t, te, nu: (te[last(t, te, nu)], 0, 0)),
                pl.BlockSpec((1, D_MODEL, D_EXPERT), lambda t, te, nu: (te[last(t, te, nu)], 0, 0)),
                pl.BlockSpec((1, D_EXPERT, D_MODEL), lambda t, te, nu: (te[last(t, te, nu)], 0, 0)),
            ],
            out_specs=pl.BlockSpec((te_rows, c), lambda t, te, nu: (t, 0)),
        ),
        compiler_params=pltpu.CompilerParams(dimension_semantics=("arbitrary",)),
        name="moe_experts",
    )(tile_expert, n_used, xs, wg, wu, wd)


def _combine_kernel(start_ref, n8_ref, nbig_ref, nsmall_ref, ys_ref, x1_ref, rec_ref, g_ref, ya_ref, yb_ref,
                    buf_ref, sem, *, a_tiles):
    i = pl.program_id(0)
    last = pl.num_programs(0) - 1
    slot = i % 2
    rv = buf_ref.shape[1]

    def piece_copy(s, src_row, dst_row, rows):
        return pltpu.make_async_copy(ys_ref.at[pl.ds(src_row, rows)],
                                     buf_ref.at[s, pl.ds(dst_row, rows)], sem.at[s])

    def fetch(tile, s):
        _for_each_piece(tile, start_ref, n8_ref,
                        lambda row, src_row, rows: piece_copy(s, src_row, row, rows).start())

    @pl.when(i == 0)
    def _():
        buf_ref[...] = jnp.zeros_like(buf_ref)
        fetch(0, 0)

    @pl.when(i < last)
    def _():
        fetch(i + 1, 1 - slot)

    _drain(lambda rows: piece_copy(slot, 0, 0, rows), nbig_ref[i], nsmall_ref[i])
    rec = rec_ref[...]
    col = lax.broadcasted_iota(jnp.int32, (rec.shape[0], rv), 1).astype(F32)
    weigh = (jnp.where(col == rec[:, R_D1:R_D1 + 1], rec[:, R_W1:R_W1 + 1], 0.0)
             + jnp.where(col == rec[:, R_D2:R_D2 + 1], rec[:, R_W2:R_W2 + 1], 0.0))
    moe = jnp.dot(weigh.astype(BF16), buf_ref[slot].astype(BF16), preferred_element_type=F32)
    y = _rms_scale(x1_ref[...] + moe) * g_ref[...]

    @pl.when(i < a_tiles)
    def _():
        ya_ref[...] = y

    @pl.when(i >= a_tiles)
    def _():
        yb_ref[...] = y


def _combine(plan, ys, x1, rec, g, n_a, tm):
    n = x1.shape[0]
    a_tiles = n_a // tm
    rv = _tile_rows(tm)
    row = lambda i, *_: (i, 0)
    return pl.pallas_call(
        functools.partial(_combine_kernel, a_tiles=a_tiles),
        out_shape=[jax.ShapeDtypeStruct((n_a, D_MODEL), F32),
                   jax.ShapeDtypeStruct((n - n_a, D_MODEL), F32)],
        grid_spec=pltpu.PrefetchScalarGridSpec(
            num_scalar_prefetch=4,
            grid=(n // tm,),
            in_specs=[
                pl.BlockSpec(memory_space=pl.ANY),
                pl.BlockSpec((tm, D_MODEL), row),
                pl.BlockSpec((tm, ROUTER_COLS), row),
                pl.BlockSpec((1, D_MODEL), lambda i, *_: (0, 0)),
            ],
            out_specs=[
                pl.BlockSpec((tm, D_MODEL), lambda i, *_: (jnp.minimum(i, a_tiles - 1), 0)),
                pl.BlockSpec((tm, D_MODEL), lambda i, *_: (jnp.maximum(i - a_tiles, 0), 0))],
            scratch_shapes=[pltpu.VMEM((2, rv, D_MODEL), F32), pltpu.SemaphoreType.DMA((2,))],
        ),
        compiler_params=pltpu.CompilerParams(dimension_semantics=("arbitrary",)),
        name="moe_combine",
    )(*plan, ys, x1, rec, g)


def _moe_plan(counts, n, tm, te_rows):
    cnt = counts[:, 0, N_GROUPS:N_GROUPS + N_EXPERTS].astype(jnp.int32)
    n8 = (cnt + SUBLANES - 1) // SUBLANES
    run = n8 * SUBLANES
    tiles = (jnp.sum(run, axis=0) + te_rows - 1) // te_rows
    tile_end = jnp.cumsum(tiles)
    first_row = (tile_end - tiles) * te_rows
    start = first_row + jnp.cumsum(run, axis=0) - run
    pad_n8 = (tiles * te_rows - jnp.sum(run, axis=0)) // SUBLANES
    pad_plan = (first_row + jnp.sum(run, axis=0), pad_n8, jnp.sum(pad_n8, keepdims=True))
    n_tiles = (2 * n + (n // tm) * N_EXPERTS * (SUBLANES - 1) + te_rows - 1) // te_rows + N_EXPERTS
    tile_expert = jnp.minimum(
        jnp.sum(jnp.arange(n_tiles)[:, None] >= tile_end[None, :], axis=1), N_EXPERTS - 1)
    ratio = PIECE_ROWS[0] // PIECE_ROWS[1]
    plan = (start.reshape(-1), n8.reshape(-1), jnp.sum(n8 // ratio, axis=1),
            jnp.sum(n8 % ratio, axis=1))
    return (plan, pad_plan, tile_expert.astype(jnp.int32), tile_end[-1:].astype(jnp.int32),
            n_tiles * te_rows)


def _row_tile(n, want):
    tm = min(n, want)
    while n % tm or tm % SUBLANES:
        tm -= 1
    return tm


def _divisor_at_most(n, want):
    d = min(n, want)
    while n % d:
        d -= 1
    return d


def _alibi_slopes():
    return jnp.exp2(-8.0 * jnp.arange(1, N_HEADS + 1, dtype=F32) / N_HEADS)


def _prompt_bias():
    qi = jnp.arange(WINDOW)[:, None]
    kj = jnp.arange(2 * WINDOW)[None, :]
    dist = WINDOW + qi - kj
    in_window = (dist >= 0) & (dist < WINDOW)
    valid = jnp.stack([in_window & (kj >= WINDOW), in_window])
    score = -_alibi_slopes()[None, :, None, None] * dist.astype(F32)[None, None]
    return jnp.where(valid[:, None], score, NEG_INF)


def _sample_bias(t, bb):
    q_pos = jnp.tile(jnp.arange(t), bb * GQA)[:, None]
    q_seq = jnp.repeat(jnp.arange(bb), GQA * t)[:, None]
    k_pos = jnp.concatenate([jnp.tile(jnp.arange(WINDOW) - WINDOW, bb),
                             jnp.tile(jnp.arange(t), bb)])[None, :]
    k_seq = jnp.concatenate([jnp.repeat(jnp.arange(bb), WINDOW),
                             jnp.repeat(jnp.arange(bb), t)])[None, :]
    dist = q_pos - k_pos
    valid = (dist >= 0) & (dist < WINDOW) & (q_seq == k_seq)
    slopes = jnp.tile(jnp.repeat(_alibi_slopes().reshape(N_KV_HEADS, GQA), t, axis=1), (1, bb))
    score = -slopes[:, :, None] * dist.astype(F32)[None]
    return jnp.where(valid[None], score, NEG_INF)


def _layer_params(norm_mix_g, w_in, conv_w, conv_b, lru_wa, lru_ba, lru_wx, lru_bx, lru_lambda,
                  attn_sinks, w_lru_out, w_attn_out, w_o, norm_moe_g, w_router_group,
                  w_router_expert, moe_w_gate, moe_w_up, moe_w_down):
    per_pack = GATE_PACK // LRU_BLOCK_DIM

    def pack_diag(w):
        w4 = w.reshape(LRU_WIDTH // GATE_PACK, per_pack, LRU_BLOCK_DIM, LRU_BLOCK_DIM)
        eye = jnp.eye(per_pack, dtype=w.dtype)
        return jnp.einsum('pbij,bc->pbicj', w4, eye).reshape(-1, GATE_PACK, GATE_PACK)

    wbd = jnp.concatenate([pack_diag(lru_wa), pack_diag(lru_wx)], axis=-1).astype(BF16)
    w_router = jnp.concatenate(
        [w_router_group, jnp.transpose(w_router_expert, (1, 0, 2)).reshape(D_MODEL, N_EXPERTS)],
        axis=1)
    w_router = jnp.pad(w_router, ((0, 0), (0, ROUTER_COLS - w_router.shape[1])))
    wr_hi = w_router.astype(BF16)
    wr = jnp.concatenate([wr_hi, (w_router - wr_hi.astype(F32)).astype(BF16)], axis=1)
    wg = moe_w_gate.reshape(N_EXPERTS, D_MODEL, D_EXPERT)
    wu = moe_w_up.reshape(N_EXPERTS, D_MODEL, D_EXPERT)
    wd = moe_w_down.reshape(N_EXPERTS, D_EXPERT, D_MODEL)
    row = lambda v: v.reshape(1, -1)
    return dict(
        norm_mix_g=row(norm_mix_g), w_in=w_in.astype(BF16), conv_w=conv_w, conv_b=row(conv_b),
        wbd=wbd, ba=row(lru_ba), bx=row(lru_bx), lam=row(lru_lambda), sinks=attn_sinks,
        wlo=w_lru_out.astype(BF16), wao=w_attn_out.astype(BF16), wo=w_o.astype(BF16),
        norm_moe_g=row(norm_moe_g), wr=wr, wg=wg, wu=wu, wd=wd)


def _prompt_mixers(x, p):
    bsz, t, _ = x.shape
    assert t % WINDOW == 0 and t >= CONV_W - 1
    n = bsz * t
    x2 = x.reshape(n, D_MODEL)
    xl, q, k, v, gl, ga = _inproj(x2, p['norm_mix_g'], p['w_in'], _row_tile(n, 512))
    tc = _row_tile(t, 512)
    yl, h_last = _lru(xl, jnp.zeros((bsz, SUBLANES, LRU_WIDTH), F32),
                      jnp.zeros((bsz, 1, LRU_WIDTH), F32), p['conv_w'], p['conv_b'], p['wbd'],
                      p['ba'], p['bx'], p['lam'], nseq=bsz, seqlen=t, bb=1, tc=tc)
    k3 = k.reshape(bsz, t, KV_COLS)
    v3 = v.reshape(bsz, t, KV_COLS)
    nblk = _divisor_at_most(t // WINDOW, ATTN_BLOCKS)
    ya = _attn_prompt(q.reshape(bsz, t, -1), k3, v3, p['sinks'], _prompt_bias(), nblk)
    new_conv = xl.reshape(bsz, t, LRU_WIDTH)[:, t - (CONV_W - 1):]
    new_k = k3[:, t - WINDOW:].reshape(bsz, WINDOW, N_KV_HEADS, HEAD_DIM)
    new_v = v3[:, t - WINDOW:].reshape(bsz, WINDOW, N_KV_HEADS, HEAD_DIM)
    return ((x2, yl, ya.reshape(n, -1), gl, ga),
            (new_conv, h_last.reshape(bsz, LRU_WIDTH), new_k, new_v))


def _sample_mixers(x, conv_buf, h0, k_buf, v_buf, p):
    bsz, t, _ = x.shape
    assert t % SUBLANES == 0 and CONV_W - 1 <= t <= WINDOW
    n = bsz * t
    x2 = x.reshape(n, D_MODEL)
    xl, q, k, v, gl, ga = _inproj(x2, p['norm_mix_g'], p['w_in'], _row_tile(n, 512))
    bb = _row_tile(bsz, 16)
    prev8 = jnp.pad(conv_buf, ((0, 0), (SUBLANES - (CONV_W - 1), 0), (0, 0)))
    yl, h_last = _lru(xl, prev8, h0.reshape(bsz, 1, LRU_WIDTH), p['conv_w'], p['conv_b'],
                      p['wbd'], p['ba'], p['bx'], p['lam'], nseq=bsz, seqlen=t, bb=bb, tc=t)
    q4 = q.reshape(bsz, t, N_KV_HEADS, GQA, HEAD_DIM).transpose(0, 2, 3, 1, 4).reshape(
        bsz, N_KV_HEADS, GQA * t, HEAD_DIM)
    bb_attn = _row_tile(bsz, SAMPLE_SEQS)
    sink_rows = jnp.tile(jnp.repeat(p['sinks'].reshape(N_KV_HEADS, GQA), t, axis=1),
                         (1, bb_attn))[:, :, None]
    o4, new_k, new_v = _attn_sample(
        q4, k.reshape(bsz, t, KV_COLS), v.reshape(bsz, t, KV_COLS),
        k_buf.reshape(bsz, WINDOW, KV_COLS), v_buf.reshape(bsz, WINDOW, KV_COLS),
        _sample_bias(t, bb_attn), sink_rows, bb_attn)
    ya = o4.reshape(bsz, N_KV_HEADS, GQA, t, HEAD_DIM).transpose(0, 3, 1, 2, 4).reshape(n, -1)
    new_conv = xl.reshape(bsz, t, LRU_WIDTH)[:, t - (CONV_W - 1):]
    shape = (bsz, WINDOW, N_KV_HEADS, HEAD_DIM)
    return ((x2, yl, ya, gl, ga),
            (new_conv, h_last.reshape(bsz, LRU_WIDTH), new_k.reshape(shape), new_v.reshape(shape)))


def _merge_and_moe(acts_a, acts_b, p, norm_final_g):
    n_a, n_b = acts_a[0].shape[0], acts_b[0].shape[0]
    n = n_a + n_b
    tm = _row_tile(math.gcd(n_a, n_b), MERGE_ROWS)
    x1, xn, rec, cnt = _merge(acts_a, acts_b, p, tm)
    plan, pad_plan, tile_expert, n_used, n_rows = _moe_plan(cnt, n, tm, EXPERT_ROWS)
    xs = _dispatch(plan, pad_plan, n_used, xn, rec, n_rows, tm, EXPERT_ROWS)
    ys = _experts(tile_expert, n_used, xs, p['wg'], p['wu'], p['wd'], EXPERT_ROWS)
    return _combine(plan, ys, x1, rec, norm_final_g.reshape(1, -1), n_a, tm)


def kernel(x_prompt, x_sample, state_conv, state_lru_h, cache_win_k, cache_win_v, norm_mix_g, w_in, conv_w, conv_b, lru_wa, lru_ba, lru_wx, lru_bx, lru_lambda, attn_sinks, w_lru_out, w_attn_out, w_o, norm_moe_g, w_router_group, w_router_expert, moe_w_gate, moe_w_up, moe_w_down, norm_final_g):
    depth = w_in.shape[0]
    assert depth == 1, "the final norm is fused into the single layer's MoE kernel"
    p = _layer_params(norm_mix_g[0], w_in[0], conv_w[0], conv_b[0], lru_wa[0], lru_ba[0],
                      lru_wx[0], lru_bx[0], lru_lambda[0], attn_sinks[0], w_lru_out[0],
                      w_attn_out[0], w_o[0], norm_moe_g[0], w_router_group[0],
                      w_router_expert[0], moe_w_gate[0], moe_w_up[0], moe_w_down[0])
    acts_p, (c1, h1, k1, v1) = _prompt_mixers(x_prompt, p)
    acts_s, (c2, h2, k2, v2) = _sample_mixers(x_sample, state_conv[0], state_lru_h[0],
                                              cache_win_k[0], cache_win_v[0], p)
    yp, ys = _merge_and_moe(acts_p, acts_s, p, norm_final_g)
    return (yp.reshape(x_prompt.shape), ys.reshape(x_sample.shape), c1[None], h1[None], k1[None],
            v1[None], c2[None], h2[None], k2[None], v2[None])
```

```python
import functools
import math

import jax
import jax.numpy as jnp
from jax import lax
from jax.experimental import pallas as pl
from jax.experimental.pallas import tpu as pltpu

D_MODEL = 1024
LRU_WIDTH = D_MODEL
LRU_BLOCKS = 16
LRU_BLOCK_DIM = LRU_WIDTH // LRU_BLOCKS
LRU_C = 8.0
CONV_W = 4
N_HEADS = 16
N_KV_HEADS = 4
GQA = N_HEADS // N_KV_HEADS
HEAD_DIM = D_MODEL // N_HEADS
KV_COLS = N_KV_HEADS * HEAD_DIM
WINDOW = 128
N_GROUPS = 4
EXPERTS_PER_GROUP = 8
N_EXPERTS = N_GROUPS * EXPERTS_PER_GROUP
D_EXPERT = D_MODEL // 4
RMS_EPS = 1e-6
NEG_INF = -1e30
IN_COLS = LRU_WIDTH + (N_HEADS + 2 * N_KV_HEADS) * HEAD_DIM + 2 * D_MODEL

SUBLANES = 8
LANES = 128
ROUTER_COLS = LANES
GATE_PACK = 256

MERGE_ROWS = 512
EXPERT_ROWS = 512
RUN_ALIGN = 16
PIECE_ROWS = (32, RUN_ALIGN)
ATTN_BLOCKS = 4
DISPATCH_SLOTS = 3
SAMPLE_SEQS = 8

F32 = jnp.float32
BF16 = jnp.bfloat16


def _sigmoid(z):
    return 1.0 / (1.0 + jnp.exp(-z))


def _one_minus_exp2(x, ex):
    return -jnp.tanh(x) * (ex * ex + 1.0)


def _rms_scale(x):
    return x * lax.rsqrt(jnp.mean(x * x, axis=-1, keepdims=True) + RMS_EPS)


def _inproj_kernel(x_ref, g_ref, w_ref, xl_ref, q_ref, k_ref, v_ref, gl_ref, ga_ref):
    xb = (_rms_scale(x_ref[...]) * g_ref[...]).astype(BF16)

    def proj(c0, c1):
        return jnp.dot(xb, w_ref[:, c0:c1], preferred_element_type=F32)

    c0 = LRU_WIDTH
    c1 = c0 + N_HEADS * HEAD_DIM
    c2 = c1 + KV_COLS
    c3 = c2 + KV_COLS
    c4 = c3 + D_MODEL
    xl_ref[...] = proj(0, c0)
    q_ref[...] = (proj(c0, c1) * (HEAD_DIM ** -0.5)).astype(BF16)
    k_ref[...] = proj(c1, c2)
    v_ref[...] = proj(c2, c3)
    gl_ref[...] = _sigmoid(proj(c3, c4)).astype(BF16)
    ga_ref[...] = _sigmoid(proj(c4, IN_COLS)).astype(BF16)


def _inproj(x2, g, w_bf, tm):
    n = x2.shape[0]
    row = lambda i: (i, 0)
    fixed = lambda i: (0, 0)
    outs = [
        jax.ShapeDtypeStruct((n, LRU_WIDTH), F32),
        jax.ShapeDtypeStruct((n, N_HEADS * HEAD_DIM), BF16),
        jax.ShapeDtypeStruct((n, KV_COLS), F32),
        jax.ShapeDtypeStruct((n, KV_COLS), F32),
        jax.ShapeDtypeStruct((n, D_MODEL), BF16),
        jax.ShapeDtypeStruct((n, D_MODEL), BF16),
    ]
    return pl.pallas_call(
        _inproj_kernel,
        out_shape=outs,
        grid=(n // tm,),
        in_specs=[
            pl.BlockSpec((tm, D_MODEL), row),
            pl.BlockSpec((1, D_MODEL), fixed),
            pl.BlockSpec((D_MODEL, IN_COLS), fixed),
        ],
        out_specs=[pl.BlockSpec((tm, o.shape[1]), row) for o in outs],
        compiler_params=pltpu.CompilerParams(dimension_semantics=("parallel",)),
        name="inproj",
    )(x2, g, w_bf)


def _lru_kernel(xl_ref, prev_ref, h0_ref, cw_ref, cb_ref, wbd_ref, ba_ref, bx_ref, lam_ref,
                y_ref, hl_ref, xs_ref, xc_ref, a_ref, b_ref, hc_ref, *, bb, tc):
    t = pl.program_id(1)

    @pl.when(t == 0)
    def _():
        xs_ref[:, 0:SUBLANES, :] = prev_ref[...]
        hc_ref[...] = h0_ref[...]

    @pl.when(t > 0)
    def _():
        xs_ref[:, 0:SUBLANES, :] = xs_ref[:, tc:tc + SUBLANES, :]

    for s in range(bb):
        xs_ref[s, SUBLANES:SUBLANES + tc, :] = xl_ref[s * tc:(s + 1) * tc, :]
    cw = cw_ref[...]
    for s in range(bb):
        acc = cb_ref[...] + xs_ref[s, SUBLANES:SUBLANES + tc, :] * cw[CONV_W - 1:CONV_W, :]
        for j in range(CONV_W - 1):
            off = SUBLANES - (CONV_W - 1) + j
            acc = acc + xs_ref[s, off:off + tc, :] * cw[j:j + 1, :]
        xc_ref[s * tc:(s + 1) * tc, :] = acc

    lam = lam_ref[...]
    softplus_neg_lam = jnp.maximum(-lam, 0.0) + jnp.log1p(jnp.exp(-jnp.abs(lam)))
    for g in range(LRU_WIDTH // GATE_PACK):
        cols = slice(g * GATE_PACK, (g + 1) * GATE_PACK)
        xc = xc_ref[:, cols]
        gates = jnp.dot(xc.astype(BF16), wbd_ref[g], preferred_element_type=F32)
        r = _sigmoid(gates[:, :GATE_PACK] + ba_ref[:, cols])
        ig = _sigmoid(gates[:, GATE_PACK:] + bx_ref[:, cols])
        log_a = (-LRU_C) * r * softplus_neg_lam[:, cols]
        a = jnp.exp(log_a)
        a_ref[:, cols] = a
        b_ref[:, cols] = jnp.sqrt(_one_minus_exp2(log_a, a)) * (ig * xc)

    row = lax.broadcasted_iota(jnp.int32, (SUBLANES, LRU_WIDTH), 0)
    for s in range(bb):
        def group(gi, h8, s=s):
            r0 = pl.multiple_of(s * tc + gi * SUBLANES, SUBLANES)
            a = a_ref[pl.ds(r0, SUBLANES), :]
            b = b_ref[pl.ds(r0, SUBLANES), :]
            for d in (1, 2, 4):
                a_up = jnp.where(row >= d, pltpu.roll(a, d, 0), 1.0)
                b_up = jnp.where(row >= d, pltpu.roll(b, d, 0), 0.0)
                b = a * b_up + b
                a = a * a_up
            h = a * h8 + b
            b_ref[pl.ds(r0, SUBLANES), :] = h
            return jnp.broadcast_to(h[SUBLANES - 1:SUBLANES, :], (SUBLANES, LRU_WIDTH))

        h8 = lax.fori_loop(0, tc // SUBLANES, group,
                           jnp.broadcast_to(hc_ref[s], (SUBLANES, LRU_WIDTH)))
        hc_ref[s] = h8[0:1, :]

    y_ref[...] = b_ref[...].astype(y_ref.dtype)

    @pl.when(t == pl.num_programs(1) - 1)
    def _():
        hl_ref[...] = hc_ref[...]


def _lru(xl2, prev8, h0, cw, cb, wbd, ba, bx, lam, *, nseq, seqlen, bb, tc):
    nt = seqlen // tc
    assert bb == 1 or nt == 1
    rows = bb * tc
    fixed2 = lambda b, t: (0, 0)
    fixed3 = lambda b, t: (0, 0, 0)
    return pl.pallas_call(
        functools.partial(_lru_kernel, bb=bb, tc=tc),
        out_shape=[
            jax.ShapeDtypeStruct((nseq * seqlen, LRU_WIDTH), BF16),
            jax.ShapeDtypeStruct((nseq, 1, LRU_WIDTH), F32),
        ],
        grid=(nseq // bb, nt),
        in_specs=[
            pl.BlockSpec((rows, LRU_WIDTH), lambda b, t: (b * nt + t, 0)),
            pl.BlockSpec((bb, SUBLANES, LRU_WIDTH), lambda b, t: (b, 0, 0)),
            pl.BlockSpec((bb, 1, LRU_WIDTH), lambda b, t: (b, 0, 0)),
            pl.BlockSpec((CONV_W, LRU_WIDTH), fixed2),
            pl.BlockSpec((1, LRU_WIDTH), fixed2),
            pl.BlockSpec((LRU_WIDTH // GATE_PACK, GATE_PACK, 2 * GATE_PACK), fixed3),
            pl.BlockSpec((1, LRU_WIDTH), fixed2),
            pl.BlockSpec((1, LRU_WIDTH), fixed2),
            pl.BlockSpec((1, LRU_WIDTH), fixed2),
        ],
        out_specs=[
            pl.BlockSpec((rows, LRU_WIDTH), lambda b, t: (b * nt + t, 0)),
            pl.BlockSpec((bb, 1, LRU_WIDTH), lambda b, t: (b, 0, 0)),
        ],
        scratch_shapes=[
            pltpu.VMEM((bb, tc + SUBLANES, LRU_WIDTH), F32),
            pltpu.VMEM((rows, LRU_WIDTH), F32),
            pltpu.VMEM((rows, LRU_WIDTH), F32),
            pltpu.VMEM((rows, LRU_WIDTH), F32),
            pltpu.VMEM((bb, 1, LRU_WIDTH), F32),
        ],
        compiler_params=pltpu.CompilerParams(dimension_semantics=("parallel", "arbitrary")),
        name="conv_rglru",
    )(xl2, prev8, h0, cw, cb, wbd, ba, bx, lam)


def _softmax_sink_pv(s, sink, v_bf):
    m = jnp.maximum(jnp.max(s, axis=-1, keepdims=True), sink)
    p = jnp.exp(s - m)
    denom = jnp.sum(p, axis=-1, keepdims=True) + jnp.exp(sink - m)
    o = jnp.dot(p.astype(BF16), v_bf, preferred_element_type=F32)
    return o / denom


def _attn_prompt_kernel(sink_ref, q_ref, kc_ref, kp_ref, vc_ref, vp_ref, bias_ref, o_ref,
                        k_ref, v_ref, *, nblk):
    k_ref[0:WINDOW, :] = kp_ref[0].astype(BF16)
    k_ref[WINDOW:, :] = kc_ref[0].astype(BF16)
    v_ref[0:WINDOW, :] = vp_ref[0].astype(BF16)
    v_ref[WINDOW:, :] = vc_ref[0].astype(BF16)

    def block(j, carry):
        bsel = jnp.where((pl.program_id(1) == 0) & (j == 0), 0, 1)
        r0 = pl.multiple_of(j * WINDOW, WINDOW)
        q = q_ref[0, pl.ds(r0, WINDOW), :]
        kcat = k_ref[pl.ds(r0, 2 * WINDOW), :]
        vcat = v_ref[pl.ds(r0, 2 * WINDOW), :]
        outs = []
        for h in range(N_HEADS):
            kv = slice((h // GQA) * HEAD_DIM, (h // GQA + 1) * HEAD_DIM)
            s = lax.dot_general(q[:, h * HEAD_DIM:(h + 1) * HEAD_DIM], kcat[:, kv],
                                (((1,), (1,)), ((), ())), preferred_element_type=F32)
            s = s + bias_ref[bsel, h]
            outs.append(_softmax_sink_pv(s, sink_ref[h], vcat[:, kv]))
        o_ref[0, pl.ds(r0, WINDOW), :] = jnp.concatenate(outs, axis=-1).astype(o_ref.dtype)
        return carry

    lax.fori_loop(0, nblk, block, 0)


def _attn_prompt(q3, k3, v3, sinks, bias, nblk):
    bsz, t, _ = q3.shape
    rows = nblk * WINDOW
    cur = lambda b, i: (b, i, 0)
    prev = lambda b, i: (b, jnp.maximum(i * nblk - 1, 0), 0)
    return pl.pallas_call(
        functools.partial(_attn_prompt_kernel, nblk=nblk),
        out_shape=jax.ShapeDtypeStruct((bsz, t, N_HEADS * HEAD_DIM), BF16),
        grid=(bsz, t // rows),
        in_specs=[
            pl.BlockSpec(memory_space=pltpu.SMEM),
            pl.BlockSpec((1, rows, N_HEADS * HEAD_DIM), cur),
            pl.BlockSpec((1, rows, KV_COLS), cur),
            pl.BlockSpec((1, WINDOW, KV_COLS), prev),
            pl.BlockSpec((1, rows, KV_COLS), cur),
            pl.BlockSpec((1, WINDOW, KV_COLS), prev),
            pl.BlockSpec((2, N_HEADS, WINDOW, 2 * WINDOW), lambda b, i: (0, 0, 0, 0)),
        ],
        out_specs=pl.BlockSpec((1, rows, N_HEADS * HEAD_DIM), cur),
        scratch_shapes=[pltpu.VMEM((rows + WINDOW, KV_COLS), BF16),
                        pltpu.VMEM((rows + WINDOW, KV_COLS), BF16)],
        compiler_params=pltpu.CompilerParams(dimension_semantics=("parallel", "arbitrary")),
        name="swa_prompt",
    )(sinks, q3, k3, k3, v3, v3, bias)


def _attn_sample_kernel(q_ref, kn_ref, vn_ref, ck_ref, cv_ref, bias_ref, sink_ref,
                        o_ref, nk_ref, nv_ref, *, bb, t):
    keep = WINDOW - t
    for s in range(bb):
        nk_ref[s, 0:keep, :] = ck_ref[s, t:WINDOW, :]
        nk_ref[s, keep:WINDOW, :] = kn_ref[s]
        nv_ref[s, 0:keep, :] = cv_ref[s, t:WINDOW, :]
        nv_ref[s, keep:WINDOW, :] = vn_ref[s]
    kall = jnp.concatenate([ck_ref[...].reshape(bb * WINDOW, KV_COLS),
                            kn_ref[...].reshape(bb * t, KV_COLS)], axis=0).astype(BF16)
    vall = jnp.concatenate([cv_ref[...].reshape(bb * WINDOW, KV_COLS),
                            vn_ref[...].reshape(bb * t, KV_COLS)], axis=0).astype(BF16)
    rows = q_ref.shape[2]
    for kh in range(N_KV_HEADS):
        kv = slice(kh * HEAD_DIM, (kh + 1) * HEAD_DIM)
        qh = q_ref[:, kh].reshape(bb * rows, HEAD_DIM)
        sc = lax.dot_general(qh, kall[:, kv], (((1,), (1,)), ((), ())),
                             preferred_element_type=F32)
        o = _softmax_sink_pv(sc + bias_ref[kh], sink_ref[kh], vall[:, kv])
        o_ref[:, kh] = o.reshape(bb, rows, HEAD_DIM).astype(o_ref.dtype)


def _attn_sample(q4, kn3, vn3, ck3, cv3, bias, sink_rows, bb):
    nseq, _, rows, _ = q4.shape
    t = kn3.shape[1]
    b3 = lambda i: (i, 0, 0)
    b4 = lambda i: (i, 0, 0, 0)
    return pl.pallas_call(
        functools.partial(_attn_sample_kernel, bb=bb, t=t),
        out_shape=[
            jax.ShapeDtypeStruct((nseq, N_KV_HEADS, rows, HEAD_DIM), BF16),
            jax.ShapeDtypeStruct((nseq, WINDOW, KV_COLS), F32),
            jax.ShapeDtypeStruct((nseq, WINDOW, KV_COLS), F32),
        ],
        grid=(nseq // bb,),
        in_specs=[
            pl.BlockSpec((bb, N_KV_HEADS, rows, HEAD_DIM), b4),
            pl.BlockSpec((bb, t, KV_COLS), b3),
            pl.BlockSpec((bb, t, KV_COLS), b3),
            pl.BlockSpec((bb, WINDOW, KV_COLS), b3),
            pl.BlockSpec((bb, WINDOW, KV_COLS), b3),
            pl.BlockSpec((N_KV_HEADS, bb * rows, bb * (WINDOW + t)), lambda i: (0, 0, 0)),
            pl.BlockSpec((N_KV_HEADS, bb * rows, 1), lambda i: (0, 0, 0)),
        ],
        out_specs=[
            pl.BlockSpec((bb, N_KV_HEADS, rows, HEAD_DIM), b4),
            pl.BlockSpec((bb, WINDOW, KV_COLS), b3),
            pl.BlockSpec((bb, WINDOW, KV_COLS), b3),
        ],
        compiler_params=pltpu.CompilerParams(dimension_semantics=("parallel",)),
        name="swa_sample",
    )(q4, kn3, vn3, ck3, cv3, bias, sink_rows)


R_E1, R_E2, R_D1, R_D2, R_W1, R_W2 = range(6)


def _route(logits):
    rows = logits.shape[0]
    lane = lax.broadcasted_iota(jnp.int32, logits.shape, 1)
    big = jnp.int32(ROUTER_COLS)
    gl = jnp.where(lane < N_GROUPS, logits, -jnp.inf)
    gmax = jnp.max(gl, axis=-1, keepdims=True)
    p_grp = 1.0 / jnp.sum(jnp.exp(gl - gmax), axis=-1, keepdims=True)
    g_idx = jnp.min(jnp.where(gl == gmax, lane, big), axis=-1, keepdims=True)
    e_lo = N_GROUPS + EXPERTS_PER_GROUP * g_idx
    el = jnp.where((lane >= e_lo) & (lane < e_lo + EXPERTS_PER_GROUP), logits, -jnp.inf)
    m1 = jnp.max(el, axis=-1, keepdims=True)
    i1 = jnp.min(jnp.where(el == m1, lane, big), axis=-1, keepdims=True)
    el2 = jnp.where(lane == i1, -jnp.inf, el)
    m2 = jnp.max(el2, axis=-1, keepdims=True)
    i2 = jnp.min(jnp.where(el2 == m2, lane, big), axis=-1, keepdims=True)
    e21 = jnp.exp(m2 - m1)
    w1 = p_grp / (1.0 + e21)
    w2 = e21 * w1
    hit1 = lane == i1
    hit2 = lane == i2
    onehot = jnp.where(hit1 | hit2, 1.0, 0.0).astype(BF16)
    r_i = lax.broadcasted_iota(jnp.int32, (rows, rows), 0)
    c_i = lax.broadcasted_iota(jnp.int32, (rows, rows), 1)
    tril = jnp.where(c_i <= r_i, 1.0, 0.0).astype(BF16)
    upto = jnp.dot(tril, onehot, preferred_element_type=F32)
    count = upto[rows - 1:rows, :]
    run_u = jnp.floor((count + (RUN_ALIGN - 1)) * (1.0 / RUN_ALIGN))
    k_i = lax.broadcasted_iota(jnp.int32, (ROUTER_COLS, ROUTER_COLS), 0)
    j_i = lax.broadcasted_iota(jnp.int32, (ROUTER_COLS, ROUTER_COLS), 1)
    before = jnp.where(k_i < j_i, 1.0, 0.0).astype(BF16)
    start_u = jnp.dot(jnp.broadcast_to(run_u, (SUBLANES, ROUTER_COLS)).astype(BF16), before,
                     preferred_element_type=F32)[0:1, :]
    dest = RUN_ALIGN * start_u + upto - 1.0
    d1 = jnp.sum(jnp.where(hit1, dest, 0.0), axis=-1, keepdims=True)
    d2 = jnp.sum(jnp.where(hit2, dest, 0.0), axis=-1, keepdims=True)
    rec = jnp.zeros(logits.shape, F32)
    for idx, val in ((R_E1, (i1 - N_GROUPS).astype(F32)), (R_E2, (i2 - N_GROUPS).astype(F32)),
                     (R_D1, d1), (R_D2, d2), (R_W1, w1), (R_W2, w2)):
        rec = jnp.where(lane == idx, val, rec)
    return rec, count


def _merge_kernel(*refs, a_tiles):
    acts_a, acts_b = refs[0:5], refs[5:10]
    wlo_ref, wao_ref, wo_ref, g_ref, wr_ref, x1_ref, xn_ref, rec_ref, cnt_ref = refs[10:]

    def tile(x_ref, yl_ref, ya_ref, gl_ref, ga_ref):
        a = jnp.dot(yl_ref[...], wlo_ref[...], preferred_element_type=F32)
        b = jnp.dot(ya_ref[...], wao_ref[...], preferred_element_type=F32)
        merged = gl_ref[...].astype(F32) * a + ga_ref[...].astype(F32) * b
        x1 = x_ref[...] + jnp.dot(merged.astype(BF16), wo_ref[...], preferred_element_type=F32)
        x1_ref[...] = x1
        xn = _rms_scale(x1) * g_ref[...]
        xh = xn.astype(BF16)
        xn_ref[...] = xh
        xlo = (xn - xh.astype(F32)).astype(BF16)
        both = jnp.dot(xh, wr_ref[...], preferred_element_type=F32)
        logits = (both[:, :ROUTER_COLS] + both[:, ROUTER_COLS:]
                  + jnp.dot(xlo, wr_ref[:, :ROUTER_COLS], preferred_element_type=F32))
        rec, count = _route(logits)
        rec_ref[...] = rec
        cnt_ref[0] = jnp.broadcast_to(count, (SUBLANES, ROUTER_COLS))

    pl.when(pl.program_id(0) < a_tiles)(lambda: tile(*acts_a))
    pl.when(pl.program_id(0) >= a_tiles)(lambda: tile(*acts_b))


def _merge(acts_a, acts_b, p, tm):
    n_a, n_b = acts_a[0].shape[0], acts_b[0].shape[0]
    n = n_a + n_b
    a_tiles = n_a // tm
    row = lambda i: (i, 0)
    fixed = lambda i: (0, 0)
    act_a = pl.BlockSpec((tm, D_MODEL), lambda i: (jnp.minimum(i, a_tiles - 1), 0))
    act_b = pl.BlockSpec((tm, D_MODEL), lambda i: (jnp.maximum(i - a_tiles, 0), 0))
    wsq = pl.BlockSpec((D_MODEL, D_MODEL), fixed)
    wr = pl.BlockSpec((D_MODEL, 2 * ROUTER_COLS), fixed)
    return pl.pallas_call(
        functools.partial(_merge_kernel, a_tiles=a_tiles),
        out_shape=[
            jax.ShapeDtypeStruct((n, D_MODEL), F32),
            jax.ShapeDtypeStruct((n, D_MODEL), BF16),
            jax.ShapeDtypeStruct((n, ROUTER_COLS), F32),
            jax.ShapeDtypeStruct((n // tm, SUBLANES, ROUTER_COLS), F32),
        ],
        grid=(n // tm,),
        in_specs=[act_a] * 5 + [act_b] * 5 + [wsq, wsq, wsq, pl.BlockSpec((1, D_MODEL), fixed), wr],
        out_specs=[pl.BlockSpec((tm, D_MODEL), row),
                   pl.BlockSpec((tm, D_MODEL), row),
                   pl.BlockSpec((tm, ROUTER_COLS), row),
                   pl.BlockSpec((1, SUBLANES, ROUTER_COLS), lambda i: (i, 0, 0))],
        compiler_params=pltpu.CompilerParams(dimension_semantics=("parallel",)),
        name="merge_router",
    )(*acts_a, *acts_b, p['wlo'], p['wao'], p['wo'], p['norm_moe_g'], p['wr'])


def _tile_rows(tm):
    return -(-(2 * tm + N_EXPERTS * (RUN_ALIGN - 1)) // LANES) * LANES


def _for_each_piece(i, start_ref, nrun_ref, fn):
    big, small = PIECE_ROWS
    ratio = big // small

    def per_expert(e, row_u):
        nu = nrun_ref[i * N_EXPERTS + e]
        src0 = row_u * RUN_ALIGN
        dst0 = start_ref[i * N_EXPERTS + e]
        n_big = nu // ratio

        def big_piece(k, carry):
            fn(pl.multiple_of(src0 + k * big, small), pl.multiple_of(dst0 + k * big, small), big)
            return carry

        def small_piece(k, carry):
            off = n_big * big + k * small
            fn(pl.multiple_of(src0 + off, small), pl.multiple_of(dst0 + off, small), small)
            return carry

        lax.fori_loop(0, n_big, big_piece, 0)
        lax.fori_loop(0, nu - n_big * ratio, small_piece, 0)
        return row_u + nu

    lax.fori_loop(0, N_EXPERTS, per_expert, 0)


def _drain(copy_of, n_big, n_small):
    big, small = PIECE_ROWS
    lax.fori_loop(0, n_big, lambda k, c: (copy_of(big).wait(), c)[1], 0)
    lax.fori_loop(0, n_small, lambda k, c: (copy_of(small).wait(), c)[1], 0)


def _dispatch_kernel(start_ref, nrun_ref, nbig_ref, nsmall_ref, pad_start_ref, pad_nu_ref,
                     pad_tot_ref, used_ref, xn_ref, rec_ref, xs_ref, buf_ref, zero_ref, sem, zsem):
    i = pl.program_id(0)
    nslot = buf_ref.shape[0]
    slot = i % nslot
    rv = buf_ref.shape[1]
    rec_t = jnp.transpose(rec_ref[...])
    row = lax.broadcasted_iota(jnp.int32, (rv, rec_t.shape[1]), 0).astype(F32)
    place = jnp.where((row == rec_t[R_D1:R_D1 + 1, :]) | (row == rec_t[R_D2:R_D2 + 1, :]), 1.0, 0.0)
    buf_ref[slot] = jnp.dot(place.astype(BF16), xn_ref[...],
                            preferred_element_type=F32).astype(buf_ref.dtype)

    def piece_copy(s, src_row, dst_row, rows):
        return pltpu.make_async_copy(buf_ref.at[s, pl.ds(src_row, rows)],
                                     xs_ref.at[pl.ds(dst_row, rows)], sem.at[s])

    def drain(s, n_big, n_small):
        _drain(lambda rows: piece_copy(s, 0, 0, rows), n_big, n_small)

    @pl.when(i >= nslot - 1)
    def _():
        drain((i + 1) % nslot, nbig_ref[i - (nslot - 1)], nsmall_ref[i - (nslot - 1)])

    _for_each_piece(i, start_ref, nrun_ref,
                    lambda src_row, dst_row, rows: piece_copy(slot, src_row, dst_row, rows).start())

    @pl.when(i == pl.num_programs(0) - 1)
    def _():
        zero_ref[...] = jnp.zeros_like(zero_ref)

        def pad_expert(e, carry):
            def piece(k, c):
                dst_row = pl.multiple_of(pad_start_ref[e] + k * RUN_ALIGN, RUN_ALIGN)
                pltpu.make_async_copy(zero_ref.at[pl.ds(0, RUN_ALIGN)],
                                      xs_ref.at[pl.ds(dst_row, RUN_ALIGN)], sem.at[slot]).start()
                return c

            lax.fori_loop(0, pad_nu_ref[e], piece, 0)
            return carry

        lax.fori_loop(0, N_EXPERTS, pad_expert, 0)

        te_rows = zero_ref.shape[0]

        def tail_copy(t):
            return pltpu.make_async_copy(
                zero_ref, xs_ref.at[pl.ds(pl.multiple_of(t * te_rows, te_rows), te_rows)], zsem)

        n_tiles = xs_ref.shape[0] // te_rows
        lax.fori_loop(used_ref[0], n_tiles, lambda t, c: (tail_copy(t).start(), c)[1], 0)
        for back in range(1, nslot - 1):
            @pl.when(i >= back)
            def _(back=back):
                drain((i - back) % nslot, nbig_ref[i - back], nsmall_ref[i - back])
        drain(slot, nbig_ref[i], nsmall_ref[i] + pad_tot_ref[0])
        lax.fori_loop(used_ref[0], n_tiles, lambda t, c: (tail_copy(t).wait(), c)[1], 0)


def _dispatch(plan, pad_plan, n_used, xn, rec, n_rows, tm, te_rows):
    n = xn.shape[0]
    rv = _tile_rows(tm)
    row = lambda i, *_: (i, 0)
    return pl.pallas_call(
        _dispatch_kernel,
        out_shape=jax.ShapeDtypeStruct((n_rows, D_MODEL), BF16),
        grid_spec=pltpu.PrefetchScalarGridSpec(
            num_scalar_prefetch=8,
            grid=(n // tm,),
            in_specs=[pl.BlockSpec((tm, D_MODEL), row), pl.BlockSpec((tm, ROUTER_COLS), row)],
            out_specs=pl.BlockSpec(memory_space=pl.ANY),
            scratch_shapes=[pltpu.VMEM((DISPATCH_SLOTS, rv, D_MODEL), BF16),
                            pltpu.VMEM((te_rows, D_MODEL), BF16),
                            pltpu.SemaphoreType.DMA((DISPATCH_SLOTS,)), pltpu.SemaphoreType.DMA],
        ),
        compiler_params=pltpu.CompilerParams(dimension_semantics=("arbitrary",)),
        name="moe_dispatch",
    )(*plan, *pad_plan, n_used, xn, rec)


def _experts_kernel(te_ref, nu_ref, xs_ref, wg_ref, wu_ref, wd_ref, ys_ref):
    del te_ref

    @pl.when(pl.program_id(0) < nu_ref[0])
    def _():
        x = xs_ref[...]
        gate = jnp.dot(x, wg_ref[0].astype(BF16), preferred_element_type=F32)
        up = jnp.dot(x, wu_ref[0].astype(BF16), preferred_element_type=F32)
        h = gate * _sigmoid(gate) * up
        ys_ref[...] = jnp.dot(h.astype(BF16), wd_ref[0].astype(BF16),
                              preferred_element_type=F32).astype(ys_ref.dtype)

    @pl.when(pl.program_id(0) >= nu_ref[0])
    def _():
        ys_ref[...] = jnp.zeros_like(ys_ref)


def _experts(tile_expert, n_used, xs, wg, wu, wd, te_rows):
    n_rows, c = xs.shape
    last = lambda t, te, nu: jnp.minimum(t, nu[0] - 1)
    return pl.pallas_call(
        _experts_kernel,
        out_shape=jax.ShapeDtypeStruct((n_rows, c), xs.dtype),
        grid_spec=pltpu.PrefetchScalarGridSpec(
            num_scalar_prefetch=2,
            grid=(n_rows // te_rows,),
            in_specs=[
                pl.BlockSpec((te_rows, c), lambda t, te, nu: (last(t, te, nu), 0)),
                pl.BlockSpec((1, D_MODEL, D_EXPERT), lambda t, te, nu: (te[last(t, te, nu)], 0, 0)),
                pl.BlockSpec((1, D_MODEL, D_EXPERT), lambda t, te, nu: (te[last(t, te, nu)], 0, 0)),
                pl.BlockSpec((1, D_EXPERT, D_MODEL), lambda t, te, nu: (te[last(t, te, nu)], 0, 0)),
            ],
            out_specs=pl.BlockSpec((te_rows, c), lambda t, te, nu: (t, 0)),
        ),
        compiler_params=pltpu.CompilerParams(dimension_semantics=("arbitrary",)),
        name="moe_experts",
    )(tile_expert, n_used, xs, wg, wu, wd)


def _combine_kernel(start_ref, nrun_ref, nbig_ref, nsmall_ref, ys_ref, x1_ref, rec_ref, g_ref, ya_ref, yb_ref,
                    buf_ref, sem, *, a_tiles):
    i = pl.program_id(0)
    last = pl.num_programs(0) - 1
    slot = i % 2
    rv = buf_ref.shape[1]

    def piece_copy(s, src_row, dst_row, rows):
        return pltpu.make_async_copy(ys_ref.at[pl.ds(src_row, rows)],
                                     buf_ref.at[s, pl.ds(dst_row, rows)], sem.at[s])

    def fetch(tile, s):
        _for_each_piece(tile, start_ref, nrun_ref,
                        lambda row, src_row, rows: piece_copy(s, src_row, row, rows).start())

    @pl.when(i == 0)
    def _():
        buf_ref[...] = jnp.zeros_like(buf_ref)
        fetch(0, 0)

    @pl.when(i < last)
    def _():
        fetch(i + 1, 1 - slot)

    _drain(lambda rows: piece_copy(slot, 0, 0, rows), nbig_ref[i], nsmall_ref[i])
    rec = rec_ref[...]
    col = lax.broadcasted_iota(jnp.int32, (rec.shape[0], rv), 1).astype(F32)
    weigh = (jnp.where(col == rec[:, R_D1:R_D1 + 1], rec[:, R_W1:R_W1 + 1], 0.0)
             + jnp.where(col == rec[:, R_D2:R_D2 + 1], rec[:, R_W2:R_W2 + 1], 0.0))
    moe = jnp.dot(weigh.astype(BF16), buf_ref[slot], preferred_element_type=F32)
    y = _rms_scale(x1_ref[...] + moe) * g_ref[...]

    @pl.when(i < a_tiles)
    def _():
        ya_ref[...] = y

    @pl.when(i >= a_tiles)
    def _():
        yb_ref[...] = y


def _combine(plan, ys, x1, rec, g, n_a, tm):
    n = x1.shape[0]
    a_tiles = n_a // tm
    rv = _tile_rows(tm)
    row = lambda i, *_: (i, 0)
    return pl.pallas_call(
        functools.partial(_combine_kernel, a_tiles=a_tiles),
        out_shape=[jax.ShapeDtypeStruct((n_a, D_MODEL), F32),
                   jax.ShapeDtypeStruct((n - n_a, D_MODEL), F32)],
        grid_spec=pltpu.PrefetchScalarGridSpec(
            num_scalar_prefetch=4,
            grid=(n // tm,),
            in_specs=[
                pl.BlockSpec(memory_space=pl.ANY),
                pl.BlockSpec((tm, D_MODEL), row),
                pl.BlockSpec((tm, ROUTER_COLS), row),
                pl.BlockSpec((1, D_MODEL), lambda i, *_: (0, 0)),
            ],
            out_specs=[
                pl.BlockSpec((tm, D_MODEL), lambda i, *_: (jnp.minimum(i, a_tiles - 1), 0)),
                pl.BlockSpec((tm, D_MODEL), lambda i, *_: (jnp.maximum(i - a_tiles, 0), 0))],
            scratch_shapes=[pltpu.VMEM((2, rv, D_MODEL), BF16), pltpu.SemaphoreType.DMA((2,))],
        ),
        compiler_params=pltpu.CompilerParams(dimension_semantics=("arbitrary",)),
        name="moe_combine",
    )(*plan, ys, x1, rec, g)


def _moe_plan(counts, n, tm, te_rows):
    cnt = counts[:, 0, N_GROUPS:N_GROUPS + N_EXPERTS].astype(jnp.int32)
    nu = (cnt + RUN_ALIGN - 1) // RUN_ALIGN
    run = nu * RUN_ALIGN
    tiles = (jnp.sum(run, axis=0) + te_rows - 1) // te_rows
    tile_end = jnp.cumsum(tiles)
    first_row = (tile_end - tiles) * te_rows
    start = first_row + jnp.cumsum(run, axis=0) - run
    pad_nu = (tiles * te_rows - jnp.sum(run, axis=0)) // RUN_ALIGN
    pad_plan = (first_row + jnp.sum(run, axis=0), pad_nu, jnp.sum(pad_nu, keepdims=True))
    n_tiles = (2 * n + (n // tm) * N_EXPERTS * (RUN_ALIGN - 1) + te_rows - 1) // te_rows + N_EXPERTS
    tile_expert = jnp.minimum(
        jnp.sum(jnp.arange(n_tiles)[:, None] >= tile_end[None, :], axis=1), N_EXPERTS - 1)
    ratio = PIECE_ROWS[0] // PIECE_ROWS[1]
    plan = (start.reshape(-1), nu.reshape(-1), jnp.sum(nu // ratio, axis=1),
            jnp.sum(nu % ratio, axis=1))
    return (plan, pad_plan, tile_expert.astype(jnp.int32), tile_end[-1:].astype(jnp.int32),
            n_tiles * te_rows)


def _row_tile(n, want):
    tm = min(n, want)
    while n % tm or tm % SUBLANES:
        tm -= 1
    return tm


def _divisor_at_most(n, want):
    d = min(n, want)
    while n % d:
        d -= 1
    return d


def _alibi_slopes():
    return jnp.exp2(-8.0 * jnp.arange(1, N_HEADS + 1, dtype=F32) / N_HEADS)


def _prompt_bias():
    qi = jnp.arange(WINDOW)[:, None]
    kj = jnp.arange(2 * WINDOW)[None, :]
    dist = WINDOW + qi - kj
    in_window = (dist >= 0) & (dist < WINDOW)
    valid = jnp.stack([in_window & (kj >= WINDOW), in_window])
    score = -_alibi_slopes()[None, :, None, None] * dist.astype(F32)[None, None]
    return jnp.where(valid[:, None], score, NEG_INF)


def _sample_bias(t, bb):
    q_pos = jnp.tile(jnp.arange(t), bb * GQA)[:, None]
    q_seq = jnp.repeat(jnp.arange(bb), GQA * t)[:, None]
    k_pos = jnp.concatenate([jnp.tile(jnp.arange(WINDOW) - WINDOW, bb),
                             jnp.tile(jnp.arange(t), bb)])[None, :]
    k_seq = jnp.concatenate([jnp.repeat(jnp.arange(bb), WINDOW),
                             jnp.repeat(jnp.arange(bb), t)])[None, :]
    dist = q_pos - k_pos
    valid = (dist >= 0) & (dist < WINDOW) & (q_seq == k_seq)
    slopes = jnp.tile(jnp.repeat(_alibi_slopes().reshape(N_KV_HEADS, GQA), t, axis=1), (1, bb))
    score = -slopes[:, :, None] * dist.astype(F32)[None]
    return jnp.where(valid[None], score, NEG_INF)


def _layer_params(norm_mix_g, w_in, conv_w, conv_b, lru_wa, lru_ba, lru_wx, lru_bx, lru_lambda,
                  attn_sinks, w_lru_out, w_attn_out, w_o, norm_moe_g, w_router_group,
                  w_router_expert, moe_w_gate, moe_w_up, moe_w_down):
    per_pack = GATE_PACK // LRU_BLOCK_DIM

    def pack_diag(w):
        w4 = w.reshape(LRU_WIDTH // GATE_PACK, per_pack, LRU_BLOCK_DIM, LRU_BLOCK_DIM)
        eye = jnp.eye(per_pack, dtype=w.dtype)
        return jnp.einsum('pbij,bc->pbicj', w4, eye).reshape(-1, GATE_PACK, GATE_PACK)

    wbd = jnp.concatenate([pack_diag(lru_wa), pack_diag(lru_wx)], axis=-1).astype(BF16)
    w_router = jnp.concatenate(
        [w_router_group, jnp.transpose(w_router_expert, (1, 0, 2)).reshape(D_MODEL, N_EXPERTS)],
        axis=1)
    w_router = jnp.pad(w_router, ((0, 0), (0, ROUTER_COLS - w_router.shape[1])))
    wr_hi = w_router.astype(BF16)
    wr = jnp.concatenate([wr_hi, (w_router - wr_hi.astype(F32)).astype(BF16)], axis=1)
    wg = moe_w_gate.reshape(N_EXPERTS, D_MODEL, D_EXPERT)
    wu = moe_w_up.reshape(N_EXPERTS, D_MODEL, D_EXPERT)
    wd = moe_w_down.reshape(N_EXPERTS, D_EXPERT, D_MODEL)
    row = lambda v: v.reshape(1, -1)
    return dict(
        norm_mix_g=row(norm_mix_g), w_in=w_in.astype(BF16), conv_w=conv_w, conv_b=row(conv_b),
        wbd=wbd, ba=row(lru_ba), bx=row(lru_bx), lam=row(lru_lambda), sinks=attn_sinks,
        wlo=w_lru_out.astype(BF16), wao=w_attn_out.astype(BF16), wo=w_o.astype(BF16),
        norm_moe_g=row(norm_moe_g), wr=wr, wg=wg, wu=wu, wd=wd)


def _prompt_mixers(x, p):
    bsz, t, _ = x.shape
    assert t % WINDOW == 0 and t >= CONV_W - 1
    n = bsz * t
    x2 = x.reshape(n, D_MODEL)
    xl, q, k, v, gl, ga = _inproj(x2, p['norm_mix_g'], p['w_in'], _row_tile(n, 512))
    tc = _row_tile(t, 512)
    yl, h_last = _lru(xl, jnp.zeros((bsz, SUBLANES, LRU_WIDTH), F32),
                      jnp.zeros((bsz, 1, LRU_WIDTH), F32), p['conv_w'], p['conv_b'], p['wbd'],
                      p['ba'], p['bx'], p['lam'], nseq=bsz, seqlen=t, bb=1, tc=tc)
    k3 = k.reshape(bsz, t, KV_COLS)
    v3 = v.reshape(bsz, t, KV_COLS)
    nblk = _divisor_at_most(t // WINDOW, ATTN_BLOCKS)
    ya = _attn_prompt(q.reshape(bsz, t, -1), k3, v3, p['sinks'], _prompt_bias(), nblk)
    new_conv = xl.reshape(bsz, t, LRU_WIDTH)[:, t - (CONV_W - 1):]
    new_k = k3[:, t - WINDOW:].reshape(bsz, WINDOW, N_KV_HEADS, HEAD_DIM)
    new_v = v3[:, t - WINDOW:].reshape(bsz, WINDOW, N_KV_HEADS, HEAD_DIM)
    return ((x2, yl, ya.reshape(n, -1), gl, ga),
            (new_conv, h_last.reshape(bsz, LRU_WIDTH), new_k, new_v))


def _sample_mixers(x, conv_buf, h0, k_buf, v_buf, p):
    bsz, t, _ = x.shape
    assert t % SUBLANES == 0 and CONV_W - 1 <= t <= WINDOW
    n = bsz * t
    x2 = x.reshape(n, D_MODEL)
    xl, q, k, v, gl, ga = _inproj(x2, p['norm_mix_g'], p['w_in'], _row_tile(n, 512))
    bb = _row_tile(bsz, 16)
    prev8 = jnp.pad(conv_buf, ((0, 0), (SUBLANES - (CONV_W - 1), 0), (0, 0)))
    yl, h_last = _lru(xl, prev8, h0.reshape(bsz, 1, LRU_WIDTH), p['conv_w'], p['conv_b'],
                      p['wbd'], p['ba'], p['bx'], p['lam'], nseq=bsz, seqlen=t, bb=bb, tc=t)
    q4 = q.reshape(bsz, t, N_KV_HEADS, GQA, HEAD_DIM).transpose(0, 2, 3, 1, 4).reshape(
        bsz, N_KV_HEADS, GQA * t, HEAD_DIM)
    bb_attn = _row_tile(bsz, SAMPLE_SEQS)
    sink_rows = jnp.tile(jnp.repeat(p['sinks'].reshape(N_KV_HEADS, GQA), t, axis=1),
                         (1, bb_attn))[:, :, None]
    o4, new_k, new_v = _attn_sample(
        q4, k.reshape(bsz, t, KV_COLS), v.reshape(bsz, t, KV_COLS),
        k_buf.reshape(bsz, WINDOW, KV_COLS), v_buf.reshape(bsz, WINDOW, KV_COLS),
        _sample_bias(t, bb_attn), sink_rows, bb_attn)
    ya = o4.reshape(bsz, N_KV_HEADS, GQA, t, HEAD_DIM).transpose(0, 3, 1, 2, 4).reshape(n, -1)
    new_conv = xl.reshape(bsz, t, LRU_WIDTH)[:, t - (CONV_W - 1):]
    shape = (bsz, WINDOW, N_KV_HEADS, HEAD_DIM)
    return ((x2, yl, ya, gl, ga),
            (new_conv, h_last.reshape(bsz, LRU_WIDTH), new_k.reshape(shape), new_v.reshape(shape)))


def _merge_and_moe(acts_a, acts_b, p, norm_final_g):
    n_a, n_b = acts_a[0].shape[0], acts_b[0].shape[0]
    n = n_a + n_b
    tm = _row_tile(math.gcd(n_a, n_b), MERGE_ROWS)
    x1, xn, rec, cnt = _merge(acts_a, acts_b, p, tm)
    plan, pad_plan, tile_expert, n_used, n_rows = _moe_plan(cnt, n, tm, EXPERT_ROWS)
    xs = _dispatch(plan, pad_plan, n_used, xn, rec, n_rows, tm, EXPERT_ROWS)
    ys = _experts(tile_expert, n_used, xs, p['wg'], p['wu'], p['wd'], EXPERT_ROWS)
    return _combine(plan, ys, x1, rec, norm_final_g.reshape(1, -1), n_a, tm)


def kernel(x_prompt, x_sample, state_conv, state_lru_h, cache_win_k, cache_win_v, norm_mix_g, w_in, conv_w, conv_b, lru_wa, lru_ba, lru_wx, lru_bx, lru_lambda, attn_sinks, w_lru_out, w_attn_out, w_o, norm_moe_g, w_router_group, w_router_expert, moe_w_gate, moe_w_up, moe_w_down, norm_final_g):
    depth = w_in.shape[0]
    assert depth == 1, "the final norm is fused into the single layer's MoE kernel"
    p = _layer_params(norm_mix_g[0], w_in[0], conv_w[0], conv_b[0], lru_wa[0], lru_ba[0],
                      lru_wx[0], lru_bx[0], lru_lambda[0], attn_sinks[0], w_lru_out[0],
                      w_attn_out[0], w_o[0], norm_moe_g[0], w_router_group[0],
                      w_router_expert[0], moe_w_gate[0], moe_w_up[0], moe_w_down[0])
    acts_p, (c1, h1, k1, v1) = _prompt_mixers(x_prompt, p)
    acts_s, (c2, h2, k2, v2) = _sample_mixers(x_sample, state_conv[0], state_lru_h[0],
                                              cache_win_k[0], cache_win_v[0], p)
    yp, ys = _merge_and_moe(acts_p, acts_s, p, norm_final_g)
    return (yp.reshape(x_prompt.shape), ys.reshape(x_sample.shape), c1[None], h1[None], k1[None],
            v1[None], c2[None], h2[None], k2[None], v2[None])
```

```python
import functools
import math

import jax
import jax.numpy as jnp
from jax import lax
from jax.experimental import pallas as pl
from jax.experimental.pallas import tpu as pltpu

D_MODEL = 1024
LRU_WIDTH = D_MODEL
LRU_BLOCKS = 16
LRU_BLOCK_DIM = LRU_WIDTH // LRU_BLOCKS
LRU_C = 8.0
CONV_W = 4
N_HEADS = 16
N_KV_HEADS = 4
GQA = N_HEADS // N_KV_HEADS
HEAD_DIM = D_MODEL // N_HEADS
KV_COLS = N_KV_HEADS * HEAD_DIM
WINDOW = 128
N_GROUPS = 4
EXPERTS_PER_GROUP = 8
N_EXPERTS = N_GROUPS * EXPERTS_PER_GROUP
D_EXPERT = D_MODEL // 4
RMS_EPS = 1e-6
NEG_INF = -1e30
IN_COLS = LRU_WIDTH + (N_HEADS + 2 * N_KV_HEADS) * HEAD_DIM + 2 * D_MODEL

SUBLANES = 8
LANES = 128
ROUTER_COLS = LANES
GATE_PACK = 256

MERGE_ROWS = 512
EXPERT_ROWS = 512
RUN_ALIGN = 16
PIECE_ROWS = (32, RUN_ALIGN)
ATTN_BLOCKS = 4
DISPATCH_SLOTS = 3
SAMPLE_SEQS = 8

F32 = jnp.float32
BF16 = jnp.bfloat16


def _sigmoid(z):
    return 1.0 / (1.0 + jnp.exp(-z))


def _one_minus_exp2(x, ex):
    return -jnp.tanh(x) * (ex * ex + 1.0)


def _rms_scale(x):
    return x * lax.rsqrt(jnp.mean(x * x, axis=-1, keepdims=True) + RMS_EPS)


def _inproj_kernel(x_ref, g_ref, w_ref, xl_ref, q_ref, k_ref, v_ref, gl_ref, ga_ref):
    xb = (_rms_scale(x_ref[...]) * g_ref[...]).astype(BF16)

    def proj(c0, c1):
        return jnp.dot(xb, w_ref[:, c0:c1], preferred_element_type=F32)

    c0 = LRU_WIDTH
    c1 = c0 + N_HEADS * HEAD_DIM
    c2 = c1 + KV_COLS
    c3 = c2 + KV_COLS
    c4 = c3 + D_MODEL
    xl_ref[...] = proj(0, c0)
    q_ref[...] = (proj(c0, c1) * (HEAD_DIM ** -0.5)).astype(BF16)
    k_ref[...] = proj(c1, c2)
    v_ref[...] = proj(c2, c3)
    gl_ref[...] = _sigmoid(proj(c3, c4)).astype(BF16)
    ga_ref[...] = _sigmoid(proj(c4, IN_COLS)).astype(BF16)


def _inproj(x2, g, w_bf, tm):
    n = x2.shape[0]
    row = lambda i: (i, 0)
    fixed = lambda i: (0, 0)
    outs = [
        jax.ShapeDtypeStruct((n, LRU_WIDTH), F32),
        jax.ShapeDtypeStruct((n, N_HEADS * HEAD_DIM), BF16),
        jax.ShapeDtypeStruct((n, KV_COLS), F32),
        jax.ShapeDtypeStruct((n, KV_COLS), F32),
        jax.ShapeDtypeStruct((n, D_MODEL), BF16),
        jax.ShapeDtypeStruct((n, D_MODEL), BF16),
    ]
    return pl.pallas_call(
        _inproj_kernel,
        out_shape=outs,
        grid=(n // tm,),
        in_specs=[
            pl.BlockSpec((tm, D_MODEL), row),
            pl.BlockSpec((1, D_MODEL), fixed),
            pl.BlockSpec((D_MODEL, IN_COLS), fixed),
        ],
        out_specs=[pl.BlockSpec((tm, o.shape[1]), row) for o in outs],
        compiler_params=pltpu.CompilerParams(dimension_semantics=("parallel",)),
        name="inproj",
    )(x2, g, w_bf)


def _lru_kernel(xl_ref, prev_ref, h0_ref, cw_ref, cb_ref, wbd_ref, ba_ref, bx_ref, lam_ref,
                y_ref, hl_ref, xs_ref, xc_ref, a_ref, b_ref, hc_ref, *, bb, tc):
    t = pl.program_id(1)

    @pl.when(t == 0)
    def _():
        xs_ref[:, 0:SUBLANES, :] = prev_ref[...]
        hc_ref[...] = h0_ref[...]

    @pl.when(t > 0)
    def _():
        xs_ref[:, 0:SUBLANES, :] = xs_ref[:, tc:tc + SUBLANES, :]

    for s in range(bb):
        xs_ref[s, SUBLANES:SUBLANES + tc, :] = xl_ref[s * tc:(s + 1) * tc, :]
    cw = cw_ref[...]
    for s in range(bb):
        acc = cb_ref[...] + xs_ref[s, SUBLANES:SUBLANES + tc, :] * cw[CONV_W - 1:CONV_W, :]
        for j in range(CONV_W - 1):
            off = SUBLANES - (CONV_W - 1) + j
            acc = acc + xs_ref[s, off:off + tc, :] * cw[j:j + 1, :]
        xc_ref[s * tc:(s + 1) * tc, :] = acc

    lam = lam_ref[...]
    softplus_neg_lam = jnp.maximum(-lam, 0.0) + jnp.log1p(jnp.exp(-jnp.abs(lam)))
    for g in range(LRU_WIDTH // GATE_PACK):
        cols = slice(g * GATE_PACK, (g + 1) * GATE_PACK)
        xc = xc_ref[:, cols]
        gates = jnp.dot(xc.astype(BF16), wbd_ref[g], preferred_element_type=F32)
        r = _sigmoid(gates[:, :GATE_PACK] + ba_ref[:, cols])
        ig = _sigmoid(gates[:, GATE_PACK:] + bx_ref[:, cols])
        log_a = (-LRU_C) * r * softplus_neg_lam[:, cols]
        a = jnp.exp(log_a)
        a_ref[:, cols] = a
        b_ref[:, cols] = jnp.sqrt(_one_minus_exp2(log_a, a)) * (ig * xc)

    row = lax.broadcasted_iota(jnp.int32, (SUBLANES, LRU_WIDTH), 0)
    for s in range(bb):
        def group(gi, h8, s=s):
            r0 = pl.multiple_of(s * tc + gi * SUBLANES, SUBLANES)
            a = a_ref[pl.ds(r0, SUBLANES), :]
            b = b_ref[pl.ds(r0, SUBLANES), :]
            for d in (1, 2, 4):
                a_up = jnp.where(row >= d, pltpu.roll(a, d, 0), 1.0)
                b_up = jnp.where(row >= d, pltpu.roll(b, d, 0), 0.0)
                b = a * b_up + b
                a = a * a_up
            h = a * h8 + b
            b_ref[pl.ds(r0, SUBLANES), :] = h
            return jnp.broadcast_to(h[SUBLANES - 1:SUBLANES, :], (SUBLANES, LRU_WIDTH))

        h8 = lax.fori_loop(0, tc // SUBLANES, group,
                           jnp.broadcast_to(hc_ref[s], (SUBLANES, LRU_WIDTH)))
        hc_ref[s] = h8[0:1, :]

    y_ref[...] = b_ref[...].astype(y_ref.dtype)

    @pl.when(t == pl.num_programs(1) - 1)
    def _():
        hl_ref[...] = hc_ref[...]


def _lru(xl2, prev8, h0, cw, cb, wbd, ba, bx, lam, *, nseq, seqlen, bb, tc):
    nt = seqlen // tc
    assert bb == 1 or nt == 1
    rows = bb * tc
    fixed2 = lambda b, t: (0, 0)
    fixed3 = lambda b, t: (0, 0, 0)
    return pl.pallas_call(
        functools.partial(_lru_kernel, bb=bb, tc=tc),
        out_shape=[
            jax.ShapeDtypeStruct((nseq * seqlen, LRU_WIDTH), BF16),
            jax.ShapeDtypeStruct((nseq, 1, LRU_WIDTH), F32),
        ],
        grid=(nseq // bb, nt),
        in_specs=[
            pl.BlockSpec((rows, LRU_WIDTH), lambda b, t: (b * nt + t, 0)),
            pl.BlockSpec((bb, SUBLANES, LRU_WIDTH), lambda b, t: (b, 0, 0)),
            pl.BlockSpec((bb, 1, LRU_WIDTH), lambda b, t: (b, 0, 0)),
            pl.BlockSpec((CONV_W, LRU_WIDTH), fixed2),
            pl.BlockSpec((1, LRU_WIDTH), fixed2),
            pl.BlockSpec((LRU_WIDTH // GATE_PACK, GATE_PACK, 2 * GATE_PACK), fixed3),
            pl.BlockSpec((1, LRU_WIDTH), fixed2),
            pl.BlockSpec((1, LRU_WIDTH), fixed2),
            pl.BlockSpec((1, LRU_WIDTH), fixed2),
        ],
        out_specs=[
            pl.BlockSpec((rows, LRU_WIDTH), lambda b, t: (b * nt + t, 0)),
            pl.BlockSpec((bb, 1, LRU_WIDTH), lambda b, t: (b, 0, 0)),
        ],
        scratch_shapes=[
            pltpu.VMEM((bb, tc + SUBLANES, LRU_WIDTH), F32),
            pltpu.VMEM((rows, LRU_WIDTH), F32),
            pltpu.VMEM((rows, LRU_WIDTH), F32),
            pltpu.VMEM((rows, LRU_WIDTH), F32),
            pltpu.VMEM((bb, 1, LRU_WIDTH), F32),
        ],
        compiler_params=pltpu.CompilerParams(dimension_semantics=("parallel", "arbitrary")),
        name="conv_rglru",
    )(xl2, prev8, h0, cw, cb, wbd, ba, bx, lam)


def _softmax_sink_pv(s, sink, v_bf):
    m = jnp.maximum(jnp.max(s, axis=-1, keepdims=True), sink)
    p = jnp.exp(s - m)
    denom = jnp.sum(p, axis=-1, keepdims=True) + jnp.exp(sink - m)
    o = jnp.dot(p.astype(BF16), v_bf, preferred_element_type=F32)
    return o / denom


def _attn_prompt_kernel(sink_ref, q_ref, kc_ref, kp_ref, vc_ref, vp_ref, bias_ref, o_ref,
                        k_ref, v_ref, *, nblk):
    k_ref[0:WINDOW, :] = kp_ref[0].astype(BF16)
    k_ref[WINDOW:, :] = kc_ref[0].astype(BF16)
    v_ref[0:WINDOW, :] = vp_ref[0].astype(BF16)
    v_ref[WINDOW:, :] = vc_ref[0].astype(BF16)

    def block(j, carry):
        bsel = jnp.where((pl.program_id(1) == 0) & (j == 0), 0, 1)
        r0 = pl.multiple_of(j * WINDOW, WINDOW)
        q = q_ref[0, pl.ds(r0, WINDOW), :]
        kcat = k_ref[pl.ds(r0, 2 * WINDOW), :]
        vcat = v_ref[pl.ds(r0, 2 * WINDOW), :]
        outs = []
        for h in range(N_HEADS):
            kv = slice((h // GQA) * HEAD_DIM, (h // GQA + 1) * HEAD_DIM)
            s = lax.dot_general(q[:, h * HEAD_DIM:(h + 1) * HEAD_DIM], kcat[:, kv],
                                (((1,), (1,)), ((), ())), preferred_element_type=F32)
            s = s + bias_ref[bsel, h]
            outs.append(_softmax_sink_pv(s, sink_ref[h], vcat[:, kv]))
        o_ref[0, pl.ds(r0, WINDOW), :] = jnp.concatenate(outs, axis=-1).astype(o_ref.dtype)
        return carry

    lax.fori_loop(0, nblk, block, 0)


def _attn_prompt(q3, k3, v3, sinks, bias, nblk):
    bsz, t, _ = q3.shape
    rows = nblk * WINDOW
    cur = lambda b, i: (b, i, 0)
    prev = lambda b, i: (b, jnp.maximum(i * nblk - 1, 0), 0)
    return pl.pallas_call(
        functools.partial(_attn_prompt_kernel, nblk=nblk),
        out_shape=jax.ShapeDtypeStruct((bsz, t, N_HEADS * HEAD_DIM), BF16),
        grid=(bsz, t // rows),
        in_specs=[
            pl.BlockSpec(memory_space=pltpu.SMEM),
            pl.BlockSpec((1, rows, N_HEADS * HEAD_DIM), cur),
            pl.BlockSpec((1, rows, KV_COLS), cur),
            pl.BlockSpec((1, WINDOW, KV_COLS), prev),
            pl.BlockSpec((1, rows, KV_COLS), cur),
            pl.BlockSpec((1, WINDOW, KV_COLS), prev),
            pl.BlockSpec((2, N_HEADS, WINDOW, 2 * WINDOW), lambda b, i: (0, 0, 0, 0)),
        ],
        out_specs=pl.BlockSpec((1, rows, N_HEADS * HEAD_DIM), cur),
        scratch_shapes=[pltpu.VMEM((rows + WINDOW, KV_COLS), BF16),
                        pltpu.VMEM((rows + WINDOW, KV_COLS), BF16)],
        compiler_params=pltpu.CompilerParams(dimension_semantics=("parallel", "arbitrary")),
        name="swa_prompt",
    )(sinks, q3, k3, k3, v3, v3, bias)


def _attn_sample_kernel(q_ref, kn_ref, vn_ref, ck_ref, cv_ref, bias_ref, sink_ref,
                        o_ref, nk_ref, nv_ref, *, bb, t):
    keep = WINDOW - t
    for s in range(bb):
        nk_ref[s, 0:keep, :] = ck_ref[s, t:WINDOW, :]
        nk_ref[s, keep:WINDOW, :] = kn_ref[s]
        nv_ref[s, 0:keep, :] = cv_ref[s, t:WINDOW, :]
        nv_ref[s, keep:WINDOW, :] = vn_ref[s]
    kall = jnp.concatenate([ck_ref[...].reshape(bb * WINDOW, KV_COLS),
                            kn_ref[...].reshape(bb * t, KV_COLS)], axis=0).astype(BF16)
    vall = jnp.concatenate([cv_ref[...].reshape(bb * WINDOW, KV_COLS),
                            vn_ref[...].reshape(bb * t, KV_COLS)], axis=0).astype(BF16)
    rows = q_ref.shape[2]
    for kh in range(N_KV_HEADS):
        kv = slice(kh * HEAD_DIM, (kh + 1) * HEAD_DIM)
        qh = q_ref[:, kh].reshape(bb * rows, HEAD_DIM)
        sc = lax.dot_general(qh, kall[:, kv], (((1,), (1,)), ((), ())),
                             preferred_element_type=F32)
        o = _softmax_sink_pv(sc + bias_ref[kh], sink_ref[kh], vall[:, kv])
        o_ref[:, kh] = o.reshape(bb, rows, HEAD_DIM).astype(o_ref.dtype)


def _attn_sample(q4, kn3, vn3, ck3, cv3, bias, sink_rows, bb):
    nseq, _, rows, _ = q4.shape
    t = kn3.shape[1]
    b3 = lambda i: (i, 0, 0)
    b4 = lambda i: (i, 0, 0, 0)
    return pl.pallas_call(
        functools.partial(_attn_sample_kernel, bb=bb, t=t),
        out_shape=[
            jax.ShapeDtypeStruct((nseq, N_KV_HEADS, rows, HEAD_DIM), BF16),
            jax.ShapeDtypeStruct((nseq, WINDOW, KV_COLS), F32),
            jax.ShapeDtypeStruct((nseq, WINDOW, KV_COLS), F32),
        ],
        grid=(nseq // bb,),
        in_specs=[
            pl.BlockSpec((bb, N_KV_HEADS, rows, HEAD_DIM), b4),
            pl.BlockSpec((bb, t, KV_COLS), b3),
            pl.BlockSpec((bb, t, KV_COLS), b3),
            pl.BlockSpec((bb, WINDOW, KV_COLS), b3),
            pl.BlockSpec((bb, WINDOW, KV_COLS), b3),
            pl.BlockSpec((N_KV_HEADS, bb * rows, bb * (WINDOW + t)), lambda i: (0, 0, 0)),
            pl.BlockSpec((N_KV_HEADS, bb * rows, 1), lambda i: (0, 0, 0)),
        ],
        out_specs=[
            pl.BlockSpec((bb, N_KV_HEADS, rows, HEAD_DIM), b4),
            pl.BlockSpec((bb, WINDOW, KV_COLS), b3),
            pl.BlockSpec((bb, WINDOW, KV_COLS), b3),
        ],
        compiler_params=pltpu.CompilerParams(dimension_semantics=("parallel",)),
        name="swa_sample",
    )(q4, kn3, vn3, ck3, cv3, bias, sink_rows)


R_E1, R_E2, R_D1, R_D2, R_W1, R_W2 = range(6)


def _route(logits):
    rows = logits.shape[0]
    lane = lax.broadcasted_iota(jnp.int32, logits.shape, 1)
    big = jnp.int32(ROUTER_COLS)
    gl = jnp.where(lane < N_GROUPS, logits, -jnp.inf)
    gmax = jnp.max(gl, axis=-1, keepdims=True)
    p_grp = 1.0 / jnp.sum(jnp.exp(gl - gmax), axis=-1, keepdims=True)
    g_idx = jnp.min(jnp.where(gl == gmax, lane, big), axis=-1, keepdims=True)
    e_lo = N_GROUPS + EXPERTS_PER_GROUP * g_idx
    el = jnp.where((lane >= e_lo) & (lane < e_lo + EXPERTS_PER_GROUP), logits, -jnp.inf)
    m1 = jnp.max(el, axis=-1, keepdims=True)
    i1 = jnp.min(jnp.where(el == m1, lane, big), axis=-1, keepdims=True)
    el2 = jnp.where(lane == i1, -jnp.inf, el)
    m2 = jnp.max(el2, axis=-1, keepdims=True)
    i2 = jnp.min(jnp.where(el2 == m2, lane, big), axis=-1, keepdims=True)
    e21 = jnp.exp(m2 - m1)
    w1 = p_grp / (1.0 + e21)
    w2 = e21 * w1
    hit1 = lane == i1
    hit2 = lane == i2
    onehot = jnp.where(hit1 | hit2, 1.0, 0.0).astype(BF16)
    r_i = lax.broadcasted_iota(jnp.int32, (rows, rows), 0)
    c_i = lax.broadcasted_iota(jnp.int32, (rows, rows), 1)
    tril = jnp.where(c_i <= r_i, 1.0, 0.0).astype(BF16)
    upto = jnp.dot(tril, onehot, preferred_element_type=F32)
    count = upto[rows - 1:rows, :]
    run_u = jnp.floor((count + (RUN_ALIGN - 1)) * (1.0 / RUN_ALIGN))
    k_i = lax.broadcasted_iota(jnp.int32, (ROUTER_COLS, ROUTER_COLS), 0)
    j_i = lax.broadcasted_iota(jnp.int32, (ROUTER_COLS, ROUTER_COLS), 1)
    before = jnp.where(k_i < j_i, 1.0, 0.0).astype(BF16)
    start_u = jnp.dot(jnp.broadcast_to(run_u, (SUBLANES, ROUTER_COLS)).astype(BF16), before,
                     preferred_element_type=F32)[0:1, :]
    dest = RUN_ALIGN * start_u + upto - 1.0
    d1 = jnp.sum(jnp.where(hit1, dest, 0.0), axis=-1, keepdims=True)
    d2 = jnp.sum(jnp.where(hit2, dest, 0.0), axis=-1, keepdims=True)
    rec = jnp.zeros(logits.shape, F32)
    for idx, val in ((R_E1, (i1 - N_GROUPS).astype(F32)), (R_E2, (i2 - N_GROUPS).astype(F32)),
                     (R_D1, d1), (R_D2, d2), (R_W1, w1), (R_W2, w2)):
        rec = jnp.where(lane == idx, val, rec)
    return rec, count


def _merge_kernel(*refs, a_tiles):
    acts_a, acts_b = refs[0:5], refs[5:10]
    wlo_ref, wao_ref, wo_ref, g_ref, wr_ref, x1_ref, xn_ref, rec_ref, cnt_ref = refs[10:]

    def tile(x_ref, yl_ref, ya_ref, gl_ref, ga_ref):
        a = jnp.dot(yl_ref[...], wlo_ref[...], preferred_element_type=F32)
        b = jnp.dot(ya_ref[...], wao_ref[...], preferred_element_type=F32)
        merged = gl_ref[...].astype(F32) * a + ga_ref[...].astype(F32) * b
        x1 = x_ref[...] + jnp.dot(merged.astype(BF16), wo_ref[...], preferred_element_type=F32)
        x1_ref[...] = x1
        xn = _rms_scale(x1) * g_ref[...]
        xh = xn.astype(BF16)
        xn_ref[...] = xh
        xlo = (xn - xh.astype(F32)).astype(BF16)
        both = jnp.dot(xh, wr_ref[...], preferred_element_type=F32)
        logits = (both[:, :ROUTER_COLS] + both[:, ROUTER_COLS:]
                  + jnp.dot(xlo, wr_ref[:, :ROUTER_COLS], preferred_element_type=F32))
        rec, count = _route(logits)
        rec_ref[...] = rec
        cnt_ref[0] = jnp.broadcast_to(count, (SUBLANES, ROUTER_COLS))

    pl.when(pl.program_id(0) < a_tiles)(lambda: tile(*acts_a))
    pl.when(pl.program_id(0) >= a_tiles)(lambda: tile(*acts_b))


def _merge(acts_a, acts_b, p, tm):
    n_a, n_b = acts_a[0].shape[0], acts_b[0].shape[0]
    n = n_a + n_b
    a_tiles = n_a // tm
    row = lambda i: (i, 0)
    fixed = lambda i: (0, 0)
    act_a = pl.BlockSpec((tm, D_MODEL), lambda i: (jnp.minimum(i, a_tiles - 1), 0))
    act_b = pl.BlockSpec((tm, D_MODEL), lambda i: (jnp.maximum(i - a_tiles, 0), 0))
    wsq = pl.BlockSpec((D_MODEL, D_MODEL), fixed)
    wr = pl.BlockSpec((D_MODEL, 2 * ROUTER_COLS), fixed)
    return pl.pallas_call(
        functools.partial(_merge_kernel, a_tiles=a_tiles),
        out_shape=[
            jax.ShapeDtypeStruct((n, D_MODEL), F32),
            jax.ShapeDtypeStruct((n, D_MODEL), BF16),
            jax.ShapeDtypeStruct((n, ROUTER_COLS), F32),
            jax.ShapeDtypeStruct((n // tm, SUBLANES, ROUTER_COLS), F32),
        ],
        grid=(n // tm,),
        in_specs=[act_a] * 5 + [act_b] * 5 + [wsq, wsq, wsq, pl.BlockSpec((1, D_MODEL), fixed), wr],
        out_specs=[pl.BlockSpec((tm, D_MODEL), row),
                   pl.BlockSpec((tm, D_MODEL), row),
                   pl.BlockSpec((tm, ROUTER_COLS), row),
                   pl.BlockSpec((1, SUBLANES, ROUTER_COLS), lambda i: (i, 0, 0))],
        compiler_params=pltpu.CompilerParams(dimension_semantics=("parallel",)),
        name="merge_router",
    )(*acts_a, *acts_b, p['wlo'], p['wao'], p['wo'], p['norm_moe_g'], p['wr'])


def _tile_rows(tm):
    return -(-(2 * tm + N_EXPERTS * (RUN_ALIGN - 1)) // LANES) * LANES


def _max_pieces(rv):
    big, small = PIECE_ROWS
    return rv // big, N_EXPERTS * (big // small - 1)


def _for_each_piece(i, plan_refs, rv, fn):
    big_src, big_dst, small_src, small_dst, nbig_ref, nsmall_ref = plan_refs
    max_big, max_small = _max_pieces(rv)
    for src_ref, dst_ref, n_ref, stride, rows in ((big_src, big_dst, nbig_ref, max_big, PIECE_ROWS[0]),
                                                  (small_src, small_dst, nsmall_ref, max_small,
                                                   PIECE_ROWS[1])):
        def piece(k, carry, src_ref=src_ref, dst_ref=dst_ref, stride=stride, rows=rows):
            fn(pl.multiple_of(src_ref[i * stride + k], RUN_ALIGN),
               pl.multiple_of(dst_ref[i * stride + k], RUN_ALIGN), rows)
            return carry

        lax.fori_loop(0, n_ref[i], piece, 0)


def _drain(copy_of, n_big, n_small):
    big, small = PIECE_ROWS
    lax.fori_loop(0, n_big, lambda k, c: (copy_of(big).wait(), c)[1], 0)
    lax.fori_loop(0, n_small, lambda k, c: (copy_of(small).wait(), c)[1], 0)


def _dispatch_kernel(*refs):
    plan_refs = refs[:6]
    nbig_ref, nsmall_ref = plan_refs[4:]
    (pad_start_ref, pad_nu_ref, pad_tot_ref, used_ref, xn_ref, rec_ref, xs_ref, buf_ref, zero_ref,
     sem, zsem) = refs[6:]
    i = pl.program_id(0)
    nslot = buf_ref.shape[0]
    slot = i % nslot
    rv = buf_ref.shape[1]
    rec_t = jnp.transpose(rec_ref[...])
    row = lax.broadcasted_iota(jnp.int32, (rv, rec_t.shape[1]), 0).astype(F32)
    place = jnp.where((row == rec_t[R_D1:R_D1 + 1, :]) | (row == rec_t[R_D2:R_D2 + 1, :]), 1.0, 0.0)
    buf_ref[slot] = jnp.dot(place.astype(BF16), xn_ref[...],
                            preferred_element_type=F32).astype(buf_ref.dtype)

    def piece_copy(s, src_row, dst_row, rows):
        return pltpu.make_async_copy(buf_ref.at[s, pl.ds(src_row, rows)],
                                     xs_ref.at[pl.ds(dst_row, rows)], sem.at[s])

    def drain(s, n_big, n_small):
        _drain(lambda rows: piece_copy(s, 0, 0, rows), n_big, n_small)

    @pl.when(i >= nslot - 1)
    def _():
        drain((i + 1) % nslot, nbig_ref[i - (nslot - 1)], nsmall_ref[i - (nslot - 1)])

    _for_each_piece(i, plan_refs, rv,
                    lambda src_row, dst_row, rows: piece_copy(slot, src_row, dst_row, rows).start())

    @pl.when(i == pl.num_programs(0) - 1)
    def _():
        zero_ref[...] = jnp.zeros_like(zero_ref)

        def pad_expert(e, carry):
            def piece(k, c):
                dst_row = pl.multiple_of(pad_start_ref[e] + k * RUN_ALIGN, RUN_ALIGN)
                pltpu.make_async_copy(zero_ref.at[pl.ds(0, RUN_ALIGN)],
                                      xs_ref.at[pl.ds(dst_row, RUN_ALIGN)], sem.at[slot]).start()
                return c

            lax.fori_loop(0, pad_nu_ref[e], piece, 0)
            return carry

        lax.fori_loop(0, N_EXPERTS, pad_expert, 0)

        te_rows = zero_ref.shape[0]

        def tail_copy(t):
            return pltpu.make_async_copy(
                zero_ref, xs_ref.at[pl.ds(pl.multiple_of(t * te_rows, te_rows), te_rows)], zsem)

        n_tiles = xs_ref.shape[0] // te_rows
        lax.fori_loop(used_ref[0], n_tiles, lambda t, c: (tail_copy(t).start(), c)[1], 0)
        for back in range(1, nslot - 1):
            @pl.when(i >= back)
            def _(back=back):
                drain((i - back) % nslot, nbig_ref[i - back], nsmall_ref[i - back])
        drain(slot, nbig_ref[i], nsmall_ref[i] + pad_tot_ref[0])
        lax.fori_loop(used_ref[0], n_tiles, lambda t, c: (tail_copy(t).wait(), c)[1], 0)


def _dispatch(plan, pad_plan, n_used, xn, rec, n_rows, tm, te_rows):
    n = xn.shape[0]
    rv = _tile_rows(tm)
    row = lambda i, *_: (i, 0)
    return pl.pallas_call(
        _dispatch_kernel,
        out_shape=jax.ShapeDtypeStruct((n_rows, D_MODEL), BF16),
        grid_spec=pltpu.PrefetchScalarGridSpec(
            num_scalar_prefetch=10,
            grid=(n // tm,),
            in_specs=[pl.BlockSpec((tm, D_MODEL), row), pl.BlockSpec((tm, ROUTER_COLS), row)],
            out_specs=pl.BlockSpec(memory_space=pl.ANY),
            scratch_shapes=[pltpu.VMEM((DISPATCH_SLOTS, rv, D_MODEL), BF16),
                            pltpu.VMEM((te_rows, D_MODEL), BF16),
                            pltpu.SemaphoreType.DMA((DISPATCH_SLOTS,)), pltpu.SemaphoreType.DMA],
        ),
        compiler_params=pltpu.CompilerParams(dimension_semantics=("arbitrary",)),
        name="moe_dispatch",
    )(*plan, *pad_plan, n_used, xn, rec)


def _experts_kernel(te_ref, nu_ref, xs_ref, wg_ref, wu_ref, wd_ref, ys_ref):
    del te_ref

    @pl.when(pl.program_id(0) < nu_ref[0])
    def _():
        x = xs_ref[...]
        gate = jnp.dot(x, wg_ref[0].astype(BF16), preferred_element_type=F32)
        up = jnp.dot(x, wu_ref[0].astype(BF16), preferred_element_type=F32)
        h = gate * _sigmoid(gate) * up
        ys_ref[...] = jnp.dot(h.astype(BF16), wd_ref[0].astype(BF16),
                              preferred_element_type=F32).astype(ys_ref.dtype)

    @pl.when(pl.program_id(0) >= nu_ref[0])
    def _():
        ys_ref[...] = jnp.zeros_like(ys_ref)


def _experts(tile_expert, n_used, xs, wg, wu, wd, te_rows):
    n_rows, c = xs.shape
    last = lambda t, te, nu: jnp.minimum(t, nu[0] - 1)
    return pl.pallas_call(
        _experts_kernel,
        out_shape=jax.ShapeDtypeStruct((n_rows, c), xs.dtype),
        grid_spec=pltpu.PrefetchScalarGridSpec(
            num_scalar_prefetch=2,
            grid=(n_rows // te_rows,),
            in_specs=[
                pl.BlockSpec((te_rows, c), lambda t, te, nu: (last(t, te, nu), 0)),
                pl.BlockSpec((1, D_MODEL, D_EXPERT), lambda t, te, nu: (te[last(t, te, nu)], 0, 0)),
                pl.BlockSpec((1, D_MODEL, D_EXPERT), lambda t, te, nu: (te[last(t, te, nu)], 0, 0)),
                pl.BlockSpec((1, D_EXPERT, D_MODEL), lambda t, te, nu: (te[last(t, te, nu)], 0, 0)),
            ],
            out_specs=pl.BlockSpec((te_rows, c), lambda t, te, nu: (t, 0)),
        ),
        compiler_params=pltpu.CompilerParams(dimension_semantics=("arbitrary",)),
        name="moe_experts",
    )(tile_expert, n_used, xs, wg, wu, wd)


def _combine_kernel(*refs, a_tiles):
    plan_refs = refs[:6]
    nbig_ref, nsmall_ref = plan_refs[4:]
    ys_ref, x1_ref, rec_ref, g_ref, ya_ref, yb_ref, buf_ref, sem = refs[6:]
    i = pl.program_id(0)
    last = pl.num_programs(0) - 1
    slot = i % 2
    rv = buf_ref.shape[1]

    def piece_copy(s, src_row, dst_row, rows):
        return pltpu.make_async_copy(ys_ref.at[pl.ds(src_row, rows)],
                                     buf_ref.at[s, pl.ds(dst_row, rows)], sem.at[s])

    def fetch(tile, s):
        _for_each_piece(tile, plan_refs, rv,
                        lambda row, src_row, rows: piece_copy(s, src_row, row, rows).start())

    @pl.when(i == 0)
    def _():
        buf_ref[...] = jnp.zeros_like(buf_ref)
        fetch(0, 0)

    @pl.when(i < last)
    def _():
        fetch(i + 1, 1 - slot)

    _drain(lambda rows: piece_copy(slot, 0, 0, rows), nbig_ref[i], nsmall_ref[i])
    rec = rec_ref[...]
    col = lax.broadcasted_iota(jnp.int32, (rec.shape[0], rv), 1).astype(F32)
    weigh = (jnp.where(col == rec[:, R_D1:R_D1 + 1], rec[:, R_W1:R_W1 + 1], 0.0)
             + jnp.where(col == rec[:, R_D2:R_D2 + 1], rec[:, R_W2:R_W2 + 1], 0.0))
    moe = jnp.dot(weigh.astype(BF16), buf_ref[slot], preferred_element_type=F32)
    y = _rms_scale(x1_ref[...] + moe) * g_ref[...]

    @pl.when(i < a_tiles)
    def _():
        ya_ref[...] = y

    @pl.when(i >= a_tiles)
    def _():
        yb_ref[...] = y


def _combine(plan, ys, x1, rec, g, n_a, tm):
    n = x1.shape[0]
    a_tiles = n_a // tm
    rv = _tile_rows(tm)
    row = lambda i, *_: (i, 0)
    return pl.pallas_call(
        functools.partial(_combine_kernel, a_tiles=a_tiles),
        out_shape=[jax.ShapeDtypeStruct((n_a, D_MODEL), F32),
                   jax.ShapeDtypeStruct((n - n_a, D_MODEL), F32)],
        grid_spec=pltpu.PrefetchScalarGridSpec(
            num_scalar_prefetch=6,
            grid=(n // tm,),
            in_specs=[
                pl.BlockSpec(memory_space=pl.ANY),
                pl.BlockSpec((tm, D_MODEL), row),
                pl.BlockSpec((tm, ROUTER_COLS), row),
                pl.BlockSpec((1, D_MODEL), lambda i, *_: (0, 0)),
            ],
            out_specs=[
                pl.BlockSpec((tm, D_MODEL), lambda i, *_: (jnp.minimum(i, a_tiles - 1), 0)),
                pl.BlockSpec((tm, D_MODEL), lambda i, *_: (jnp.maximum(i - a_tiles, 0), 0))],
            scratch_shapes=[pltpu.VMEM((2, rv, D_MODEL), BF16), pltpu.SemaphoreType.DMA((2,))],
        ),
        compiler_params=pltpu.CompilerParams(dimension_semantics=("arbitrary",)),
        name="moe_combine",
    )(*plan, ys, x1, rec, g)


def _moe_plan(counts, n, tm, te_rows):
    cnt = counts[:, 0, N_GROUPS:N_GROUPS + N_EXPERTS].astype(jnp.int32)
    nu = (cnt + RUN_ALIGN - 1) // RUN_ALIGN
    run = nu * RUN_ALIGN
    tiles = (jnp.sum(run, axis=0) + te_rows - 1) // te_rows
    tile_end = jnp.cumsum(tiles)
    first_row = (tile_end - tiles) * te_rows
    start = first_row + jnp.cumsum(run, axis=0) - run
    pad_nu = (tiles * te_rows - jnp.sum(run, axis=0)) // RUN_ALIGN
    pad_plan = (first_row + jnp.sum(run, axis=0), pad_nu, jnp.sum(pad_nu, keepdims=True))
    n_tiles = (2 * n + (n // tm) * N_EXPERTS * (RUN_ALIGN - 1) + te_rows - 1) // te_rows + N_EXPERTS
    tile_expert = jnp.minimum(
        jnp.sum(jnp.arange(n_tiles)[:, None] >= tile_end[None, :], axis=1), N_EXPERTS - 1)
    big, small = PIECE_ROWS
    ratio = big // small
    layout_start = jnp.cumsum(run, axis=1) - run
    n_big, n_small = nu // ratio, nu % ratio
    max_big, max_small = _max_pieces(_tile_rows(tm))

    def piece_list(per_run, limit, rows, first):
        ends = jnp.cumsum(per_run, axis=1)
        p = jnp.arange(limit)
        e = jnp.minimum(jnp.sum(p[None, :, None] >= ends[:, None, :], axis=-1), N_EXPERTS - 1)
        at = lambda a: jnp.take_along_axis(a, e, axis=1)
        off = at(first) + (p[None, :] - at(ends - per_run)) * rows
        return [(at(layout_start) + off).reshape(-1).astype(jnp.int32),
                (at(start) + off).reshape(-1).astype(jnp.int32)]

    plan = tuple(piece_list(n_big, max_big, big, jnp.zeros_like(nu))
                 + piece_list(n_small, max_small, small, n_big * big)
                 + [jnp.sum(n_big, axis=1), jnp.sum(n_small, axis=1)])
    return (plan, pad_plan, tile_expert.astype(jnp.int32), tile_end[-1:].astype(jnp.int32),
            n_tiles * te_rows)


def _row_tile(n, want):
    tm = min(n, want)
    while n % tm or tm % SUBLANES:
        tm -= 1
    return tm


def _divisor_at_most(n, want):
    d = min(n, want)
    while n % d:
        d -= 1
    return d


def _alibi_slopes():
    return jnp.exp2(-8.0 * jnp.arange(1, N_HEADS + 1, dtype=F32) / N_HEADS)


def _prompt_bias():
    qi = jnp.arange(WINDOW)[:, None]
    kj = jnp.arange(2 * WINDOW)[None, :]
    dist = WINDOW + qi - kj
    in_window = (dist >= 0) & (dist < WINDOW)
    valid = jnp.stack([in_window & (kj >= WINDOW), in_window])
    score = -_alibi_slopes()[None, :, None, None] * dist.astype(F32)[None, None]
    return jnp.where(valid[:, None], score, NEG_INF)


def _sample_bias(t, bb):
    q_pos = jnp.tile(jnp.arange(t), bb * GQA)[:, None]
    q_seq = jnp.repeat(jnp.arange(bb), GQA * t)[:, None]
    k_pos = jnp.concatenate([jnp.tile(jnp.arange(WINDOW) - WINDOW, bb),
                             jnp.tile(jnp.arange(t), bb)])[None, :]
    k_seq = jnp.concatenate([jnp.repeat(jnp.arange(bb), WINDOW),
                             jnp.repeat(jnp.arange(bb), t)])[None, :]
    dist = q_pos - k_pos
    valid = (dist >= 0) & (dist < WINDOW) & (q_seq == k_seq)
    slopes = jnp.tile(jnp.repeat(_alibi_slopes().reshape(N_KV_HEADS, GQA), t, axis=1), (1, bb))
    score = -slopes[:, :, None] * dist.astype(F32)[None]
    return jnp.where(valid[None], score, NEG_INF)


def _layer_params(norm_mix_g, w_in, conv_w, conv_b, lru_wa, lru_ba, lru_wx, lru_bx, lru_lambda,
                  attn_sinks, w_lru_out, w_attn_out, w_o, norm_moe_g, w_router_group,
                  w_router_expert, moe_w_gate, moe_w_up, moe_w_down):
    per_pack = GATE_PACK // LRU_BLOCK_DIM

    def pack_diag(w):
        w4 = w.reshape(LRU_WIDTH // GATE_PACK, per_pack, LRU_BLOCK_DIM, LRU_BLOCK_DIM)
        eye = jnp.eye(per_pack, dtype=w.dtype)
        return jnp.einsum('pbij,bc->pbicj', w4, eye).reshape(-1, GATE_PACK, GATE_PACK)

    wbd = jnp.concatenate([pack_diag(lru_wa), pack_diag(lru_wx)], axis=-1).astype(BF16)
    w_router = jnp.concatenate(
        [w_router_group, jnp.transpose(w_router_expert, (1, 0, 2)).reshape(D_MODEL, N_EXPERTS)],
        axis=1)
    w_router = jnp.pad(w_router, ((0, 0), (0, ROUTER_COLS - w_router.shape[1])))
    wr_hi = w_router.astype(BF16)
    wr = jnp.concatenate([wr_hi, (w_router - wr_hi.astype(F32)).astype(BF16)], axis=1)
    wg = moe_w_gate.reshape(N_EXPERTS, D_MODEL, D_EXPERT)
    wu = moe_w_up.reshape(N_EXPERTS, D_MODEL, D_EXPERT)
    wd = moe_w_down.reshape(N_EXPERTS, D_EXPERT, D_MODEL)
    row = lambda v: v.reshape(1, -1)
    return dict(
        norm_mix_g=row(norm_mix_g), w_in=w_in.astype(BF16), conv_w=conv_w, conv_b=row(conv_b),
        wbd=wbd, ba=row(lru_ba), bx=row(lru_bx), lam=row(lru_lambda), sinks=attn_sinks,
        wlo=w_lru_out.astype(BF16), wao=w_attn_out.astype(BF16), wo=w_o.astype(BF16),
        norm_moe_g=row(norm_moe_g), wr=wr, wg=wg, wu=wu, wd=wd)


def _prompt_mixers(x, p):
    bsz, t, _ = x.shape
    assert t % WINDOW == 0 and t >= CONV_W - 1
    n = bsz * t
    x2 = x.reshape(n, D_MODEL)
    xl, q, k, v, gl, ga = _inproj(x2, p['norm_mix_g'], p['w_in'], _row_tile(n, 512))
    tc = _row_tile(t, 512)
    yl, h_last = _lru(xl, jnp.zeros((bsz, SUBLANES, LRU_WIDTH), F32),
                      jnp.zeros((bsz, 1, LRU_WIDTH), F32), p['conv_w'], p['conv_b'], p['wbd'],
                      p['ba'], p['bx'], p['lam'], nseq=bsz, seqlen=t, bb=1, tc=tc)
    k3 = k.reshape(bsz, t, KV_COLS)
    v3 = v.reshape(bsz, t, KV_COLS)
    nblk = _divisor_at_most(t // WINDOW, ATTN_BLOCKS)
    ya = _attn_prompt(q.reshape(bsz, t, -1), k3, v3, p['sinks'], _prompt_bias(), nblk)
    new_conv = xl.reshape(bsz, t, LRU_WIDTH)[:, t - (CONV_W - 1):]
    new_k = k3[:, t - WINDOW:].reshape(bsz, WINDOW, N_KV_HEADS, HEAD_DIM)
    new_v = v3[:, t - WINDOW:].reshape(bsz, WINDOW, N_KV_HEADS, HEAD_DIM)
    return ((x2, yl, ya.reshape(n, -1), gl, ga),
            (new_conv, h_last.reshape(bsz, LRU_WIDTH), new_k, new_v))


def _sample_mixers(x, conv_buf, h0, k_buf, v_buf, p):
    bsz, t, _ = x.shape
    assert t % SUBLANES == 0 and CONV_W - 1 <= t <= WINDOW
    n = bsz * t
    x2 = x.reshape(n, D_MODEL)
    xl, q, k, v, gl, ga = _inproj(x2, p['norm_mix_g'], p['w_in'], _row_tile(n, 512))
    bb = _row_tile(bsz, 16)
    prev8 = jnp.pad(conv_buf, ((0, 0), (SUBLANES - (CONV_W - 1), 0), (0, 0)))
    yl, h_last = _lru(xl, prev8, h0.reshape(bsz, 1, LRU_WIDTH), p['conv_w'], p['conv_b'],
                      p['wbd'], p['ba'], p['bx'], p['lam'], nseq=bsz, seqlen=t, bb=bb, tc=t)
    q4 = q.reshape(bsz, t, N_KV_HEADS, GQA, HEAD_DIM).transpose(0, 2, 3, 1, 4).reshape(
        bsz, N_KV_HEADS, GQA * t, HEAD_DIM)
    bb_attn = _row_tile(bsz, SAMPLE_SEQS)
    sink_rows = jnp.tile(jnp.repeat(p['sinks'].reshape(N_KV_HEADS, GQA), t, axis=1),
                         (1, bb_attn))[:, :, None]
    o4, new_k, new_v = _attn_sample(
        q4, k.reshape(bsz, t, KV_COLS), v.reshape(bsz, t, KV_COLS),
        k_buf.reshape(bsz, WINDOW, KV_COLS), v_buf.reshape(bsz, WINDOW, KV_COLS),
        _sample_bias(t, bb_attn), sink_rows, bb_attn)
    ya = o4.reshape(bsz, N_KV_HEADS, GQA, t, HEAD_DIM).transpose(0, 3, 1, 2, 4).reshape(n, -1)
    new_conv = xl.reshape(bsz, t, LRU_WIDTH)[:, t - (CONV_W - 1):]
    shape = (bsz, WINDOW, N_KV_HEADS, HEAD_DIM)
    return ((x2, yl, ya, gl, ga),
            (new_conv, h_last.reshape(bsz, LRU_WIDTH), new_k.reshape(shape), new_v.reshape(shape)))


def _merge_and_moe(acts_a, acts_b, p, norm_final_g):
    n_a, n_b = acts_a[0].shape[0], acts_b[0].shape[0]
    n = n_a + n_b
    tm = _row_tile(math.gcd(n_a, n_b), MERGE_ROWS)
    x1, xn, rec, cnt = _merge(acts_a, acts_b, p, tm)
    plan, pad_plan, tile_expert, n_used, n_rows = _moe_plan(cnt, n, tm, EXPERT_ROWS)
    xs = _dispatch(plan, pad_plan, n_used, xn, rec, n_rows, tm, EXPERT_ROWS)
    ys = _experts(tile_expert, n_used, xs, p['wg'], p['wu'], p['wd'], EXPERT_ROWS)
    return _combine(plan, ys, x1, rec, norm_final_g.reshape(1, -1), n_a, tm)


def kernel(x_prompt, x_sample, state_conv, state_lru_h, cache_win_k, cache_win_v, norm_mix_g, w_in, conv_w, conv_b, lru_wa, lru_ba, lru_wx, lru_bx, lru_lambda, attn_sinks, w_lru_out, w_attn_out, w_o, norm_moe_g, w_router_group, w_router_expert, moe_w_gate, moe_w_up, moe_w_down, norm_final_g):
    depth = w_in.shape[0]
    assert depth == 1, "the final norm is fused into the single layer's MoE kernel"
    p = _layer_params(norm_mix_g[0], w_in[0], conv_w[0], conv_b[0], lru_wa[0], lru_ba[0],
                      lru_wx[0], lru_bx[0], lru_lambda[0], attn_sinks[0], w_lru_out[0],
                      w_attn_out[0], w_o[0], norm_moe_g[0], w_router_group[0],
                      w_router_expert[0], moe_w_gate[0], moe_w_up[0], moe_w_down[0])
    acts_p, (c1, h1, k1, v1) = _prompt_mixers(x_prompt, p)
    acts_s, (c2, h2, k2, v2) = _sample_mixers(x_sample, state_conv[0], state_lru_h[0],
                                              cache_win_k[0], cache_win_v[0], p)
    yp, ys = _merge_and_moe(acts_p, acts_s, p, norm_final_g)
    return (yp.reshape(x_prompt.shape), ys.reshape(x_sample.shape), c1[None], h1[None], k1[None],
            v1[None], c2[None], h2[None], k2[None], v2[None])
```

```python
import functools
import math

import jax
import jax.numpy as jnp
from jax import lax
from jax.experimental import pallas as pl
from jax.experimental.pallas import tpu as pltpu

D_MODEL = 1024
LRU_WIDTH = D_MODEL
LRU_BLOCKS = 16
LRU_BLOCK_DIM = LRU_WIDTH // LRU_BLOCKS
LRU_C = 8.0
CONV_W = 4
N_HEADS = 16
N_KV_HEADS = 4
GQA = N_HEADS // N_KV_HEADS
HEAD_DIM = D_MODEL // N_HEADS
KV_COLS = N_KV_HEADS * HEAD_DIM
WINDOW = 128
N_GROUPS = 4
EXPERTS_PER_GROUP = 8
N_EXPERTS = N_GROUPS * EXPERTS_PER_GROUP
D_EXPERT = D_MODEL // 4
RMS_EPS = 1e-6
NEG_INF = -1e30
IN_COLS = LRU_WIDTH + (N_HEADS + 2 * N_KV_HEADS) * HEAD_DIM + 2 * D_MODEL

SUBLANES = 8
LANES = 128
ROUTER_COLS = LANES
GATE_PACK = 256

MERGE_ROWS = 512
EXPERT_ROWS = 512
RUN_ALIGN = 16
PIECE_ROWS = (32, RUN_ALIGN)
ATTN_BLOCKS = 4
DISPATCH_SLOTS = 3
SAMPLE_SEQS = 8

F32 = jnp.float32
BF16 = jnp.bfloat16


def _sigmoid(z):
    return 1.0 / (1.0 + jnp.exp(-z))


def _one_minus_exp2(x, ex):
    return -jnp.tanh(x) * (ex * ex + 1.0)


def _rms_scale(x):
    return x * lax.rsqrt(jnp.mean(x * x, axis=-1, keepdims=True) + RMS_EPS)


def _inproj_kernel(x_ref, g_ref, w_ref, xl_ref, q_ref, k_ref, v_ref, gl_ref, ga_ref):
    xb = (_rms_scale(x_ref[...]) * g_ref[...]).astype(BF16)

    def proj(c0, c1):
        return jnp.dot(xb, w_ref[:, c0:c1], preferred_element_type=F32)

    c0 = LRU_WIDTH
    c1 = c0 + N_HEADS * HEAD_DIM
    c2 = c1 + KV_COLS
    c3 = c2 + KV_COLS
    c4 = c3 + D_MODEL
    xl_ref[...] = proj(0, c0)
    q_ref[...] = (proj(c0, c1) * (HEAD_DIM ** -0.5)).astype(BF16)
    k_ref[...] = proj(c1, c2)
    v_ref[...] = proj(c2, c3)
    gl_ref[...] = _sigmoid(proj(c3, c4)).astype(BF16)
    ga_ref[...] = _sigmoid(proj(c4, IN_COLS)).astype(BF16)


def _inproj(x2, g, w_bf, tm):
    n = x2.shape[0]
    row = lambda i: (i, 0)
    fixed = lambda i: (0, 0)
    outs = [
        jax.ShapeDtypeStruct((n, LRU_WIDTH), F32),
        jax.ShapeDtypeStruct((n, N_HEADS * HEAD_DIM), BF16),
        jax.ShapeDtypeStruct((n, KV_COLS), F32),
        jax.ShapeDtypeStruct((n, KV_COLS), F32),
        jax.ShapeDtypeStruct((n, D_MODEL), BF16),
        jax.ShapeDtypeStruct((n, D_MODEL), BF16),
    ]
    return pl.pallas_call(
        _inproj_kernel,
        out_shape=outs,
        grid=(n // tm,),
        in_specs=[
            pl.BlockSpec((tm, D_MODEL), row),
            pl.BlockSpec((1, D_MODEL), fixed),
            pl.BlockSpec((D_MODEL, IN_COLS), fixed),
        ],
        out_specs=[pl.BlockSpec((tm, o.shape[1]), row) for o in outs],
        compiler_params=pltpu.CompilerParams(dimension_semantics=("parallel",)),
        name="inproj",
    )(x2, g, w_bf)


def _lru_kernel(xl_ref, prev_ref, h0_ref, cw_ref, cb_ref, wbd_ref, ba_ref, bx_ref, lam_ref,
                y_ref, hl_ref, xs_ref, xc_ref, a_ref, b_ref, hc_ref, *, bb, tc):
    t = pl.program_id(1)

    @pl.when(t == 0)
    def _():
        xs_ref[:, 0:SUBLANES, :] = prev_ref[...]
        hc_ref[...] = h0_ref[...]

    @pl.when(t > 0)
    def _():
        xs_ref[:, 0:SUBLANES, :] = xs_ref[:, tc:tc + SUBLANES, :]

    for s in range(bb):
        xs_ref[s, SUBLANES:SUBLANES + tc, :] = xl_ref[s * tc:(s + 1) * tc, :]
    cw = cw_ref[...]
    for s in range(bb):
        acc = cb_ref[...] + xs_ref[s, SUBLANES:SUBLANES + tc, :] * cw[CONV_W - 1:CONV_W, :]
        for j in range(CONV_W - 1):
            off = SUBLANES - (CONV_W - 1) + j
            acc = acc + xs_ref[s, off:off + tc, :] * cw[j:j + 1, :]
        xc_ref[s * tc:(s + 1) * tc, :] = acc

    lam = lam_ref[...]
    softplus_neg_lam = jnp.maximum(-lam, 0.0) + jnp.log1p(jnp.exp(-jnp.abs(lam)))
    for g in range(LRU_WIDTH // GATE_PACK):
        cols = slice(g * GATE_PACK, (g + 1) * GATE_PACK)
        xc = xc_ref[:, cols]
        gates = jnp.dot(xc.astype(BF16), wbd_ref[g], preferred_element_type=F32)
        r = _sigmoid(gates[:, :GATE_PACK] + ba_ref[:, cols])
        ig = _sigmoid(gates[:, GATE_PACK:] + bx_ref[:, cols])
        log_a = (-LRU_C) * r * softplus_neg_lam[:, cols]
        a = jnp.exp(log_a)
        a_ref[:, cols] = a
        b_ref[:, cols] = jnp.sqrt(_one_minus_exp2(log_a, a)) * (ig * xc)

    row = lax.broadcasted_iota(jnp.int32, (SUBLANES, LRU_WIDTH), 0)
    for s in range(bb):
        def group(gi, h8, s=s):
            r0 = pl.multiple_of(s * tc + gi * SUBLANES, SUBLANES)
            a = a_ref[pl.ds(r0, SUBLANES), :]
            b = b_ref[pl.ds(r0, SUBLANES), :]
            for d in (1, 2, 4):
                a_up = jnp.where(row >= d, pltpu.roll(a, d, 0), 1.0)
                b_up = jnp.where(row >= d, pltpu.roll(b, d, 0), 0.0)
                b = a * b_up + b
                a = a * a_up
            h = a * h8 + b
            b_ref[pl.ds(r0, SUBLANES), :] = h
            return jnp.broadcast_to(h[SUBLANES - 1:SUBLANES, :], (SUBLANES, LRU_WIDTH))

        h8 = lax.fori_loop(0, tc // SUBLANES, group,
                           jnp.broadcast_to(hc_ref[s], (SUBLANES, LRU_WIDTH)))
        hc_ref[s] = h8[0:1, :]

    y_ref[...] = b_ref[...].astype(y_ref.dtype)

    @pl.when(t == pl.num_programs(1) - 1)
    def _():
        hl_ref[...] = hc_ref[...]


def _lru(xl2, prev8, h0, cw, cb, wbd, ba, bx, lam, *, nseq, seqlen, bb, tc):
    nt = seqlen // tc
    assert bb == 1 or nt == 1
    rows = bb * tc
    fixed2 = lambda b, t: (0, 0)
    fixed3 = lambda b, t: (0, 0, 0)
    return pl.pallas_call(
        functools.partial(_lru_kernel, bb=bb, tc=tc),
        out_shape=[
            jax.ShapeDtypeStruct((nseq * seqlen, LRU_WIDTH), BF16),
            jax.ShapeDtypeStruct((nseq, 1, LRU_WIDTH), F32),
        ],
        grid=(nseq // bb, nt),
        in_specs=[
            pl.BlockSpec((rows, LRU_WIDTH), lambda b, t: (b * nt + t, 0)),
            pl.BlockSpec((bb, SUBLANES, LRU_WIDTH), lambda b, t: (b, 0, 0)),
            pl.BlockSpec((bb, 1, LRU_WIDTH), lambda b, t: (b, 0, 0)),
            pl.BlockSpec((CONV_W, LRU_WIDTH), fixed2),
            pl.BlockSpec((1, LRU_WIDTH), fixed2),
            pl.BlockSpec((LRU_WIDTH // GATE_PACK, GATE_PACK, 2 * GATE_PACK), fixed3),
            pl.BlockSpec((1, LRU_WIDTH), fixed2),
            pl.BlockSpec((1, LRU_WIDTH), fixed2),
            pl.BlockSpec((1, LRU_WIDTH), fixed2),
        ],
        out_specs=[
            pl.BlockSpec((rows, LRU_WIDTH), lambda b, t: (b * nt + t, 0)),
            pl.BlockSpec((bb, 1, LRU_WIDTH), lambda b, t: (b, 0, 0)),
        ],
        scratch_shapes=[
            pltpu.VMEM((bb, tc + SUBLANES, LRU_WIDTH), F32),
            pltpu.VMEM((rows, LRU_WIDTH), F32),
            pltpu.VMEM((rows, LRU_WIDTH), F32),
            pltpu.VMEM((rows, LRU_WIDTH), F32),
            pltpu.VMEM((bb, 1, LRU_WIDTH), F32),
        ],
        compiler_params=pltpu.CompilerParams(dimension_semantics=("parallel", "arbitrary")),
        name="conv_rglru",
    )(xl2, prev8, h0, cw, cb, wbd, ba, bx, lam)


def _softmax_sink_pv(s, sink, v_bf):
    m = jnp.maximum(jnp.max(s, axis=-1, keepdims=True), sink)
    p = jnp.exp(s - m)
    denom = jnp.sum(p, axis=-1, keepdims=True) + jnp.exp(sink - m)
    o = jnp.dot(p.astype(BF16), v_bf, preferred_element_type=F32)
    return o / denom


def _attn_prompt_kernel(sink_ref, q_ref, kc_ref, kp_ref, vc_ref, vp_ref, bias_ref, o_ref,
                        k_ref, v_ref, *, nblk):
    k_ref[0:WINDOW, :] = kp_ref[0].astype(BF16)
    k_ref[WINDOW:, :] = kc_ref[0].astype(BF16)
    v_ref[0:WINDOW, :] = vp_ref[0].astype(BF16)
    v_ref[WINDOW:, :] = vc_ref[0].astype(BF16)

    def block(j, carry):
        bsel = jnp.where((pl.program_id(1) == 0) & (j == 0), 0, 1)
        r0 = pl.multiple_of(j * WINDOW, WINDOW)
        q = q_ref[0, pl.ds(r0, WINDOW), :]
        kcat = k_ref[pl.ds(r0, 2 * WINDOW), :]
        vcat = v_ref[pl.ds(r0, 2 * WINDOW), :]
        outs = []
        for h in range(N_HEADS):
            kv = slice((h // GQA) * HEAD_DIM, (h // GQA + 1) * HEAD_DIM)
            s = lax.dot_general(q[:, h * HEAD_DIM:(h + 1) * HEAD_DIM], kcat[:, kv],
                                (((1,), (1,)), ((), ())), preferred_element_type=F32)
            s = s + bias_ref[bsel, h]
            outs.append(_softmax_sink_pv(s, sink_ref[h], vcat[:, kv]))
        o_ref[0, pl.ds(r0, WINDOW), :] = jnp.concatenate(outs, axis=-1).astype(o_ref.dtype)
        return carry

    lax.fori_loop(0, nblk, block, 0)


def _attn_prompt(q3, k3, v3, sinks, bias, nblk):
    bsz, t, _ = q3.shape
    rows = nblk * WINDOW
    cur = lambda b, i: (b, i, 0)
    prev = lambda b, i: (b, jnp.maximum(i * nblk - 1, 0), 0)
    return pl.pallas_call(
        functools.partial(_attn_prompt_kernel, nblk=nblk),
        out_shape=jax.ShapeDtypeStruct((bsz, t, N_HEADS * HEAD_DIM), BF16),
        grid=(bsz, t // rows),
        in_specs=[
            pl.BlockSpec(memory_space=pltpu.SMEM),
            pl.BlockSpec((1, rows, N_HEADS * HEAD_DIM), cur),
            pl.BlockSpec((1, rows, KV_COLS), cur),
            pl.BlockSpec((1, WINDOW, KV_COLS), prev),
            pl.BlockSpec((1, rows, KV_COLS), cur),
            pl.BlockSpec((1, WINDOW, KV_COLS), prev),
            pl.BlockSpec((2, N_HEADS, WINDOW, 2 * WINDOW), lambda b, i: (0, 0, 0, 0)),
        ],
        out_specs=pl.BlockSpec((1, rows, N_HEADS * HEAD_DIM), cur),
        scratch_shapes=[pltpu.VMEM((rows + WINDOW, KV_COLS), BF16),
                        pltpu.VMEM((rows + WINDOW, KV_COLS), BF16)],
        compiler_params=pltpu.CompilerParams(dimension_semantics=("parallel", "arbitrary")),
        name="swa_prompt",
    )(sinks, q3, k3, k3, v3, v3, bias)


def _attn_sample_kernel(q_ref, kn_ref, vn_ref, ck_ref, cv_ref, bias_ref, sink_ref,
                        o_ref, nk_ref, nv_ref, *, bb, t):
    keep = WINDOW - t
    for s in range(bb):
        nk_ref[s, 0:keep, :] = ck_ref[s, t:WINDOW, :]
        nk_ref[s, keep:WINDOW, :] = kn_ref[s]
        nv_ref[s, 0:keep, :] = cv_ref[s, t:WINDOW, :]
        nv_ref[s, keep:WINDOW, :] = vn_ref[s]
    kall = jnp.concatenate([ck_ref[...].reshape(bb * WINDOW, KV_COLS),
                            kn_ref[...].reshape(bb * t, KV_COLS)], axis=0).astype(BF16)
    vall = jnp.concatenate([cv_ref[...].reshape(bb * WINDOW, KV_COLS),
                            vn_ref[...].reshape(bb * t, KV_COLS)], axis=0).astype(BF16)
    rows = q_ref.shape[2]
    for kh in range(N_KV_HEADS):
        kv = slice(kh * HEAD_DIM, (kh + 1) * HEAD_DIM)
        qh = q_ref[:, kh].reshape(bb * rows, HEAD_DIM)
        sc = lax.dot_general(qh, kall[:, kv], (((1,), (1,)), ((), ())),
                             preferred_element_type=F32)
        o = _softmax_sink_pv(sc + bias_ref[kh], sink_ref[kh], vall[:, kv])
        o_ref[:, kh] = o.reshape(bb, rows, HEAD_DIM).astype(o_ref.dtype)


def _attn_sample(q4, kn3, vn3, ck3, cv3, bias, sink_rows, bb):
    nseq, _, rows, _ = q4.shape
    t = kn3.shape[1]
    b3 = lambda i: (i, 0, 0)
    b4 = lambda i: (i, 0, 0, 0)
    return pl.pallas_call(
        functools.partial(_attn_sample_kernel, bb=bb, t=t),
        out_shape=[
            jax.ShapeDtypeStruct((nseq, N_KV_HEADS, rows, HEAD_DIM), BF16),
            jax.ShapeDtypeStruct((nseq, WINDOW, KV_COLS), F32),
            jax.ShapeDtypeStruct((nseq, WINDOW, KV_COLS), F32),
        ],
        grid=(nseq // bb,),
        in_specs=[
            pl.BlockSpec((bb, N_KV_HEADS, rows, HEAD_DIM), b4),
            pl.BlockSpec((bb, t, KV_COLS), b3),
            pl.BlockSpec((bb, t, KV_COLS), b3),
            pl.BlockSpec((bb, WINDOW, KV_COLS), b3),
            pl.BlockSpec((bb, WINDOW, KV_COLS), b3),
            pl.BlockSpec((N_KV_HEADS, bb * rows, bb * (WINDOW + t)), lambda i: (0, 0, 0)),
            pl.BlockSpec((N_KV_HEADS, bb * rows, 1), lambda i: (0, 0, 0)),
        ],
        out_specs=[
            pl.BlockSpec((bb, N_KV_HEADS, rows, HEAD_DIM), b4),
            pl.BlockSpec((bb, WINDOW, KV_COLS), b3),
            pl.BlockSpec((bb, WINDOW, KV_COLS), b3),
        ],
        compiler_params=pltpu.CompilerParams(dimension_semantics=("parallel",)),
        name="swa_sample",
    )(q4, kn3, vn3, ck3, cv3, bias, sink_rows)


R_E1, R_E2, R_D1, R_D2, R_W1, R_W2 = range(6)


def _route(logits):
    rows = logits.shape[0]
    lane = lax.broadcasted_iota(jnp.int32, logits.shape, 1)
    big = jnp.int32(ROUTER_COLS)
    gl = jnp.where(lane < N_GROUPS, logits, -jnp.inf)
    gmax = jnp.max(gl, axis=-1, keepdims=True)
    p_grp = 1.0 / jnp.sum(jnp.exp(gl - gmax), axis=-1, keepdims=True)
    g_idx = jnp.min(jnp.where(gl == gmax, lane, big), axis=-1, keepdims=True)
    e_lo = N_GROUPS + EXPERTS_PER_GROUP * g_idx
    el = jnp.where((lane >= e_lo) & (lane < e_lo + EXPERTS_PER_GROUP), logits, -jnp.inf)
    m1 = jnp.max(el, axis=-1, keepdims=True)
    i1 = jnp.min(jnp.where(el == m1, lane, big), axis=-1, keepdims=True)
    el2 = jnp.where(lane == i1, -jnp.inf, el)
    m2 = jnp.max(el2, axis=-1, keepdims=True)
    i2 = jnp.min(jnp.where(el2 == m2, lane, big), axis=-1, keepdims=True)
    e21 = jnp.exp(m2 - m1)
    w1 = p_grp / (1.0 + e21)
    w2 = e21 * w1
    hit1 = lane == i1
    hit2 = lane == i2
    onehot = jnp.where(hit1 | hit2, 1.0, 0.0).astype(BF16)
    r_i = lax.broadcasted_iota(jnp.int32, (rows, rows), 0)
    c_i = lax.broadcasted_iota(jnp.int32, (rows, rows), 1)
    tril = jnp.where(c_i <= r_i, 1.0, 0.0).astype(BF16)
    upto = jnp.dot(tril, onehot, preferred_element_type=F32)
    count = upto[rows - 1:rows, :]
    run_u = jnp.floor((count + (RUN_ALIGN - 1)) * (1.0 / RUN_ALIGN))
    k_i = lax.broadcasted_iota(jnp.int32, (ROUTER_COLS, ROUTER_COLS), 0)
    j_i = lax.broadcasted_iota(jnp.int32, (ROUTER_COLS, ROUTER_COLS), 1)
    before = jnp.where(k_i < j_i, 1.0, 0.0).astype(BF16)
    start_u = jnp.dot(jnp.broadcast_to(run_u, (SUBLANES, ROUTER_COLS)).astype(BF16), before,
                     preferred_element_type=F32)[0:1, :]
    dest = RUN_ALIGN * start_u + upto - 1.0
    d1 = jnp.sum(jnp.where(hit1, dest, 0.0), axis=-1, keepdims=True)
    d2 = jnp.sum(jnp.where(hit2, dest, 0.0), axis=-1, keepdims=True)
    rec = jnp.zeros(logits.shape, F32)
    for idx, val in ((R_E1, (i1 - N_GROUPS).astype(F32)), (R_E2, (i2 - N_GROUPS).astype(F32)),
                     (R_D1, d1), (R_D2, d2), (R_W1, w1), (R_W2, w2)):
        rec = jnp.where(lane == idx, val, rec)
    return rec, count


def _merge_kernel(*refs, a_tiles):
    acts_a, acts_b = refs[0:5], refs[5:10]
    wlo_ref, wao_ref, wo_ref, g_ref, wr_ref, x1_ref, xn_ref, rec_ref, cnt_ref = refs[10:]

    def tile(x_ref, yl_ref, ya_ref, gl_ref, ga_ref):
        a = jnp.dot(yl_ref[...], wlo_ref[...], preferred_element_type=F32)
        b = jnp.dot(ya_ref[...], wao_ref[...], preferred_element_type=F32)
        merged = gl_ref[...].astype(F32) * a + ga_ref[...].astype(F32) * b
        x1 = x_ref[...] + jnp.dot(merged.astype(BF16), wo_ref[...], preferred_element_type=F32)
        x1_ref[...] = x1
        xn = _rms_scale(x1) * g_ref[...]
        xh = xn.astype(BF16)
        xn_ref[...] = xh
        xlo = (xn - xh.astype(F32)).astype(BF16)
        both = jnp.dot(xh, wr_ref[...], preferred_element_type=F32)
        logits = (both[:, :ROUTER_COLS] + both[:, ROUTER_COLS:]
                  + jnp.dot(xlo, wr_ref[:, :ROUTER_COLS], preferred_element_type=F32))
        rec, count = _route(logits)
        rec_ref[...] = rec
        cnt_ref[0] = jnp.broadcast_to(count, (SUBLANES, ROUTER_COLS))

    pl.when(pl.program_id(0) < a_tiles)(lambda: tile(*acts_a))
    pl.when(pl.program_id(0) >= a_tiles)(lambda: tile(*acts_b))


def _merge(acts_a, acts_b, p, tm):
    n_a, n_b = acts_a[0].shape[0], acts_b[0].shape[0]
    n = n_a + n_b
    a_tiles = n_a // tm
    row = lambda i: (i, 0)
    fixed = lambda i: (0, 0)
    act_a = pl.BlockSpec((tm, D_MODEL), lambda i: (jnp.minimum(i, a_tiles - 1), 0))
    act_b = pl.BlockSpec((tm, D_MODEL), lambda i: (jnp.maximum(i - a_tiles, 0), 0))
    wsq = pl.BlockSpec((D_MODEL, D_MODEL), fixed)
    wr = pl.BlockSpec((D_MODEL, 2 * ROUTER_COLS), fixed)
    return pl.pallas_call(
        functools.partial(_merge_kernel, a_tiles=a_tiles),
        out_shape=[
            jax.ShapeDtypeStruct((n, D_MODEL), F32),
            jax.ShapeDtypeStruct((n, D_MODEL), BF16),
            jax.ShapeDtypeStruct((n, ROUTER_COLS), F32),
            jax.ShapeDtypeStruct((n // tm, SUBLANES, ROUTER_COLS), F32),
        ],
        grid=(n // tm,),
        in_specs=[act_a] * 5 + [act_b] * 5 + [wsq, wsq, wsq, pl.BlockSpec((1, D_MODEL), fixed), wr],
        out_specs=[pl.BlockSpec((tm, D_MODEL), row),
                   pl.BlockSpec((tm, D_MODEL), row),
                   pl.BlockSpec((tm, ROUTER_COLS), row),
                   pl.BlockSpec((1, SUBLANES, ROUTER_COLS), lambda i: (i, 0, 0))],
        compiler_params=pltpu.CompilerParams(dimension_semantics=("parallel",)),
        name="merge_router",
    )(*acts_a, *acts_b, p['wlo'], p['wao'], p['wo'], p['norm_moe_g'], p['wr'])


def _tile_rows(tm):
    return -(-(2 * tm + N_EXPERTS * (RUN_ALIGN - 1)) // LANES) * LANES


def _max_pieces(rv):
    big, small = PIECE_ROWS
    return rv // big, N_EXPERTS * (big // small - 1)


def _for_each_piece(i, plan_refs, rv, fn):
    big_src, big_dst, small_src, small_dst, nbig_ref, nsmall_ref = plan_refs
    max_big, max_small = _max_pieces(rv)
    for src_ref, dst_ref, n_ref, stride, rows in ((big_src, big_dst, nbig_ref, max_big, PIECE_ROWS[0]),
                                                  (small_src, small_dst, nsmall_ref, max_small,
                                                   PIECE_ROWS[1])):
        def piece(k, carry, src_ref=src_ref, dst_ref=dst_ref, stride=stride, rows=rows):
            fn(pl.multiple_of(src_ref[i * stride + k], RUN_ALIGN),
               pl.multiple_of(dst_ref[i * stride + k], RUN_ALIGN), rows)
            return carry

        lax.fori_loop(0, n_ref[i], piece, 0)


def _drain(copy_of, n_big, n_small):
    big, small = PIECE_ROWS
    lax.fori_loop(0, n_big, lambda k, c: (copy_of(big).wait(), c)[1], 0)
    lax.fori_loop(0, n_small, lambda k, c: (copy_of(small).wait(), c)[1], 0)


def _dispatch_kernel(*refs):
    plan_refs = refs[:6]
    nbig_ref, nsmall_ref = plan_refs[4:]
    (pad_start_ref, pad_nu_ref, pad_tot_ref, used_ref, xn_ref, rec_ref, xs_ref, buf_ref, zero_ref,
     sem, zsem) = refs[6:]
    i = pl.program_id(0)
    nslot = buf_ref.shape[0]
    slot = i % nslot
    rv = buf_ref.shape[1]
    rec_t = jnp.transpose(rec_ref[...])
    row = lax.broadcasted_iota(jnp.int32, (rv, rec_t.shape[1]), 0).astype(F32)
    place = jnp.where((row == rec_t[R_D1:R_D1 + 1, :]) | (row == rec_t[R_D2:R_D2 + 1, :]), 1.0, 0.0)
    buf_ref[slot] = jnp.dot(place.astype(BF16), xn_ref[...],
                            preferred_element_type=F32).astype(buf_ref.dtype)

    def piece_copy(s, src_row, dst_row, rows):
        return pltpu.make_async_copy(buf_ref.at[s, pl.ds(src_row, rows)],
                                     xs_ref.at[pl.ds(dst_row, rows)], sem.at[s])

    def drain(s, n_big, n_small):
        _drain(lambda rows: piece_copy(s, 0, 0, rows), n_big, n_small)

    @pl.when(i >= nslot - 1)
    def _():
        drain((i + 1) % nslot, nbig_ref[i - (nslot - 1)], nsmall_ref[i - (nslot - 1)])

    _for_each_piece(i, plan_refs, rv,
                    lambda src_row, dst_row, rows: piece_copy(slot, src_row, dst_row, rows).start())

    @pl.when(i == pl.num_programs(0) - 1)
    def _():
        zero_ref[...] = jnp.zeros_like(zero_ref)

        def pad_expert(e, carry):
            def piece(k, c):
                dst_row = pl.multiple_of(pad_start_ref[e] + k * RUN_ALIGN, RUN_ALIGN)
                pltpu.make_async_copy(zero_ref.at[pl.ds(0, RUN_ALIGN)],
                                      xs_ref.at[pl.ds(dst_row, RUN_ALIGN)], sem.at[slot]).start()
                return c

            lax.fori_loop(0, pad_nu_ref[e], piece, 0)
            return carry

        lax.fori_loop(0, N_EXPERTS, pad_expert, 0)

        te_rows = zero_ref.shape[0]

        def tail_copy(t):
            return pltpu.make_async_copy(
                zero_ref, xs_ref.at[pl.ds(pl.multiple_of(t * te_rows, te_rows), te_rows)], zsem)

        n_tiles = xs_ref.shape[0] // te_rows
        lax.fori_loop(used_ref[0], n_tiles, lambda t, c: (tail_copy(t).start(), c)[1], 0)
        for back in range(1, nslot - 1):
            @pl.when(i >= back)
            def _(back=back):
                drain((i - back) % nslot, nbig_ref[i - back], nsmall_ref[i - back])
        drain(slot, nbig_ref[i], nsmall_ref[i] + pad_tot_ref[0])
        lax.fori_loop(used_ref[0], n_tiles, lambda t, c: (tail_copy(t).wait(), c)[1], 0)


def _dispatch(plan, pad_plan, n_used, xn, rec, n_rows, tm, te_rows):
    n = xn.shape[0]
    rv = _tile_rows(tm)
    row = lambda i, *_: (i, 0)
    return pl.pallas_call(
        _dispatch_kernel,
        out_shape=jax.ShapeDtypeStruct((n_rows, D_MODEL), BF16),
        grid_spec=pltpu.PrefetchScalarGridSpec(
            num_scalar_prefetch=10,
            grid=(n // tm,),
            in_specs=[pl.BlockSpec((tm, D_MODEL), row), pl.BlockSpec((tm, ROUTER_COLS), row)],
            out_specs=pl.BlockSpec(memory_space=pl.ANY),
            scratch_shapes=[pltpu.VMEM((DISPATCH_SLOTS, rv, D_MODEL), BF16),
                            pltpu.VMEM((te_rows, D_MODEL), BF16),
                            pltpu.SemaphoreType.DMA((DISPATCH_SLOTS,)), pltpu.SemaphoreType.DMA],
        ),
        compiler_params=pltpu.CompilerParams(dimension_semantics=("arbitrary",)),
        name="moe_dispatch",
    )(*plan, *pad_plan, n_used, xn, rec)


def _experts_kernel(te_ref, nu_ref, first_ref, next_ref, slot_ref, xs_ref, wg_hbm, wu_hbm, wd_hbm,
                    ys_ref, wg_stage, wu_stage, wd_stage, wg_bf, wu_bf, wd_bf, sem):
    t = pl.program_id(0)

    def weight_copies(e, s):
        return (pltpu.make_async_copy(wg_hbm.at[e], wg_stage.at[s], sem.at[s, 0]),
                pltpu.make_async_copy(wu_hbm.at[e], wu_stage.at[s], sem.at[s, 1]),
                pltpu.make_async_copy(wd_hbm.at[e], wd_stage.at[s], sem.at[s, 2]))

    @pl.when(t < nu_ref[0])
    def _():
        @pl.when(first_ref[t] == 1)
        def _():
            s = slot_ref[t]

            @pl.when(t == 0)
            def _():
                for copy in weight_copies(te_ref[0], s):
                    copy.start()

            for copy in weight_copies(te_ref[t], s):
                copy.wait()
            wg_bf[...] = wg_stage[s].astype(BF16)
            wu_bf[...] = wu_stage[s].astype(BF16)
            wd_bf[...] = wd_stage[s].astype(BF16)

            @pl.when(next_ref[t] >= 0)
            def _():
                for copy in weight_copies(next_ref[t], 1 - s):
                    copy.start()

        x = xs_ref[...]
        gate = jnp.dot(x, wg_bf[...], preferred_element_type=F32)
        up = jnp.dot(x, wu_bf[...], preferred_element_type=F32)
        h = gate * _sigmoid(gate) * up
        ys_ref[...] = jnp.dot(h.astype(BF16), wd_bf[...],
                              preferred_element_type=F32).astype(ys_ref.dtype)

    @pl.when(t >= nu_ref[0])
    def _():
        ys_ref[...] = jnp.zeros_like(ys_ref)


def _experts(tile_plan, xs, wg, wu, wd, te_rows):
    n_rows, c = xs.shape
    used = lambda t, te, nu, *_: (jnp.minimum(t, nu[0] - 1), 0)
    hbm = pl.BlockSpec(memory_space=pl.ANY)
    return pl.pallas_call(
        _experts_kernel,
        out_shape=jax.ShapeDtypeStruct((n_rows, c), xs.dtype),
        grid_spec=pltpu.PrefetchScalarGridSpec(
            num_scalar_prefetch=5,
            grid=(n_rows // te_rows,),
            in_specs=[pl.BlockSpec((te_rows, c), used), hbm, hbm, hbm],
            out_specs=pl.BlockSpec((te_rows, c), lambda t, *_: (t, 0)),
            scratch_shapes=[
                pltpu.VMEM((2,) + wg.shape[1:], wg.dtype),
                pltpu.VMEM((2,) + wu.shape[1:], wu.dtype),
                pltpu.VMEM((2,) + wd.shape[1:], wd.dtype),
                pltpu.VMEM(wg.shape[1:], BF16),
                pltpu.VMEM(wu.shape[1:], BF16),
                pltpu.VMEM(wd.shape[1:], BF16),
                pltpu.SemaphoreType.DMA((2, 3)),
            ],
        ),
        compiler_params=pltpu.CompilerParams(dimension_semantics=("arbitrary",)),
        name="moe_experts",
    )(*tile_plan, xs, wg, wu, wd)


def _combine_kernel(*refs, a_tiles):
    plan_refs = refs[:6]
    nbig_ref, nsmall_ref = plan_refs[4:]
    ys_ref, x1_ref, rec_ref, g_ref, ya_ref, yb_ref, buf_ref, sem = refs[6:]
    i = pl.program_id(0)
    last = pl.num_programs(0) - 1
    slot = i % 2
    rv = buf_ref.shape[1]

    def piece_copy(s, src_row, dst_row, rows):
        return pltpu.make_async_copy(ys_ref.at[pl.ds(src_row, rows)],
                                     buf_ref.at[s, pl.ds(dst_row, rows)], sem.at[s])

    def fetch(tile, s):
        _for_each_piece(tile, plan_refs, rv,
                        lambda row, src_row, rows: piece_copy(s, src_row, row, rows).start())

    @pl.when(i == 0)
    def _():
        buf_ref[...] = jnp.zeros_like(buf_ref)
        fetch(0, 0)

    @pl.when(i < last)
    def _():
        fetch(i + 1, 1 - slot)

    _drain(lambda rows: piece_copy(slot, 0, 0, rows), nbig_ref[i], nsmall_ref[i])
    rec = rec_ref[...]
    col = lax.broadcasted_iota(jnp.int32, (rec.shape[0], rv), 1).astype(F32)
    weigh = (jnp.where(col == rec[:, R_D1:R_D1 + 1], rec[:, R_W1:R_W1 + 1], 0.0)
             + jnp.where(col == rec[:, R_D2:R_D2 + 1], rec[:, R_W2:R_W2 + 1], 0.0))
    moe = jnp.dot(weigh.astype(BF16), buf_ref[slot], preferred_element_type=F32)
    y = _rms_scale(x1_ref[...] + moe) * g_ref[...]

    @pl.when(i < a_tiles)
    def _():
        ya_ref[...] = y

    @pl.when(i >= a_tiles)
    def _():
        yb_ref[...] = y


def _combine(plan, ys, x1, rec, g, n_a, tm):
    n = x1.shape[0]
    a_tiles = n_a // tm
    rv = _tile_rows(tm)
    row = lambda i, *_: (i, 0)
    return pl.pallas_call(
        functools.partial(_combine_kernel, a_tiles=a_tiles),
        out_shape=[jax.ShapeDtypeStruct((n_a, D_MODEL), F32),
                   jax.ShapeDtypeStruct((n - n_a, D_MODEL), F32)],
        grid_spec=pltpu.PrefetchScalarGridSpec(
            num_scalar_prefetch=6,
            grid=(n // tm,),
            in_specs=[
                pl.BlockSpec(memory_space=pl.ANY),
                pl.BlockSpec((tm, D_MODEL), row),
                pl.BlockSpec((tm, ROUTER_COLS), row),
                pl.BlockSpec((1, D_MODEL), lambda i, *_: (0, 0)),
            ],
            out_specs=[
                pl.BlockSpec((tm, D_MODEL), lambda i, *_: (jnp.minimum(i, a_tiles - 1), 0)),
                pl.BlockSpec((tm, D_MODEL), lambda i, *_: (jnp.maximum(i - a_tiles, 0), 0))],
            scratch_shapes=[pltpu.VMEM((2, rv, D_MODEL), BF16), pltpu.SemaphoreType.DMA((2,))],
        ),
        compiler_params=pltpu.CompilerParams(dimension_semantics=("arbitrary",)),
        name="moe_combine",
    )(*plan, ys, x1, rec, g)


def _moe_plan(counts, n, tm, te_rows):
    cnt = counts[:, 0, N_GROUPS:N_GROUPS + N_EXPERTS].astype(jnp.int32)
    nu = (cnt + RUN_ALIGN - 1) // RUN_ALIGN
    run = nu * RUN_ALIGN
    tiles = (jnp.sum(run, axis=0) + te_rows - 1) // te_rows
    tile_end = jnp.cumsum(tiles)
    first_row = (tile_end - tiles) * te_rows
    start = first_row + jnp.cumsum(run, axis=0) - run
    pad_nu = (tiles * te_rows - jnp.sum(run, axis=0)) // RUN_ALIGN
    pad_plan = (first_row + jnp.sum(run, axis=0), pad_nu, jnp.sum(pad_nu, keepdims=True))
    n_tiles = (2 * n + (n // tm) * N_EXPERTS * (RUN_ALIGN - 1) + te_rows - 1) // te_rows + N_EXPERTS
    tile_expert = jnp.minimum(
        jnp.sum(jnp.arange(n_tiles)[:, None] >= tile_end[None, :], axis=1), N_EXPERTS - 1)
    big, small = PIECE_ROWS
    ratio = big // small
    layout_start = jnp.cumsum(run, axis=1) - run
    n_big, n_small = nu // ratio, nu % ratio
    max_big, max_small = _max_pieces(_tile_rows(tm))

    def piece_list(per_run, limit, rows, first):
        ends = jnp.cumsum(per_run, axis=1)
        p = jnp.arange(limit)
        e = jnp.minimum(jnp.sum(p[None, :, None] >= ends[:, None, :], axis=-1), N_EXPERTS - 1)
        at = lambda a: jnp.take_along_axis(a, e, axis=1)
        off = at(first) + (p[None, :] - at(ends - per_run)) * rows
        return [(at(layout_start) + off).reshape(-1).astype(jnp.int32),
                (at(start) + off).reshape(-1).astype(jnp.int32)]

    plan = tuple(piece_list(n_big, max_big, big, jnp.zeros_like(nu))
                 + piece_list(n_small, max_small, small, n_big * big)
                 + [jnp.sum(n_big, axis=1), jnp.sum(n_small, axis=1)])
    n_used = tile_end[-1]
    tile = jnp.arange(n_tiles)
    first = (tile < n_used) & (tile_expert != jnp.concatenate([-jnp.ones(1, jnp.int32),
                                                              tile_expert[:-1]]))
    after = tile_end[tile_expert]
    next_expert = jnp.where(after < n_used, tile_expert[jnp.minimum(after, n_tiles - 1)], -1)
    slot = (jnp.cumsum(first) - 1) % 2
    tile_plan = tuple(a.astype(jnp.int32) for a in
                      (tile_expert, n_used[None], first, next_expert, slot))
    return plan, pad_plan, tile_plan, n_tiles * te_rows


def _row_tile(n, want):
    tm = min(n, want)
    while n % tm or tm % SUBLANES:
        tm -= 1
    return tm


def _divisor_at_most(n, want):
    d = min(n, want)
    while n % d:
        d -= 1
    return d


def _alibi_slopes():
    return jnp.exp2(-8.0 * jnp.arange(1, N_HEADS + 1, dtype=F32) / N_HEADS)


def _prompt_bias():
    qi = jnp.arange(WINDOW)[:, None]
    kj = jnp.arange(2 * WINDOW)[None, :]
    dist = WINDOW + qi - kj
    in_window = (dist >= 0) & (dist < WINDOW)
    valid = jnp.stack([in_window & (kj >= WINDOW), in_window])
    score = -_alibi_slopes()[None, :, None, None] * dist.astype(F32)[None, None]
    return jnp.where(valid[:, None], score, NEG_INF)


def _sample_bias(t, bb):
    q_pos = jnp.tile(jnp.arange(t), bb * GQA)[:, None]
    q_seq = jnp.repeat(jnp.arange(bb), GQA * t)[:, None]
    k_pos = jnp.concatenate([jnp.tile(jnp.arange(WINDOW) - WINDOW, bb),
                             jnp.tile(jnp.arange(t), bb)])[None, :]
    k_seq = jnp.concatenate([jnp.repeat(jnp.arange(bb), WINDOW),
                             jnp.repeat(jnp.arange(bb), t)])[None, :]
    dist = q_pos - k_pos
    valid = (dist >= 0) & (dist < WINDOW) & (q_seq == k_seq)
    slopes = jnp.tile(jnp.repeat(_alibi_slopes().reshape(N_KV_HEADS, GQA), t, axis=1), (1, bb))
    score = -slopes[:, :, None] * dist.astype(F32)[None]
    return jnp.where(valid[None], score, NEG_INF)


def _layer_params(norm_mix_g, w_in, conv_w, conv_b, lru_wa, lru_ba, lru_wx, lru_bx, lru_lambda,
                  attn_sinks, w_lru_out, w_attn_out, w_o, norm_moe_g, w_router_group,
                  w_router_expert, moe_w_gate, moe_w_up, moe_w_down):
    per_pack = GATE_PACK // LRU_BLOCK_DIM

    def pack_diag(w):
        w4 = w.reshape(LRU_WIDTH // GATE_PACK, per_pack, LRU_BLOCK_DIM, LRU_BLOCK_DIM)
        eye = jnp.eye(per_pack, dtype=w.dtype)
        return jnp.einsum('pbij,bc->pbicj', w4, eye).reshape(-1, GATE_PACK, GATE_PACK)

    wbd = jnp.concatenate([pack_diag(lru_wa), pack_diag(lru_wx)], axis=-1).astype(BF16)
    w_router = jnp.concatenate(
        [w_router_group, jnp.transpose(w_router_expert, (1, 0, 2)).reshape(D_MODEL, N_EXPERTS)],
        axis=1)
    w_router = jnp.pad(w_router, ((0, 0), (0, ROUTER_COLS - w_router.shape[1])))
    wr_hi = w_router.astype(BF16)
    wr = jnp.concatenate([wr_hi, (w_router - wr_hi.astype(F32)).astype(BF16)], axis=1)
    wg = moe_w_gate.reshape(N_EXPERTS, D_MODEL, D_EXPERT)
    wu = moe_w_up.reshape(N_EXPERTS, D_MODEL, D_EXPERT)
    wd = moe_w_down.reshape(N_EXPERTS, D_EXPERT, D_MODEL)
    row = lambda v: v.reshape(1, -1)
    return dict(
        norm_mix_g=row(norm_mix_g), w_in=w_in.astype(BF16), conv_w=conv_w, conv_b=row(conv_b),
        wbd=wbd, ba=row(lru_ba), bx=row(lru_bx), lam=row(lru_lambda), sinks=attn_sinks,
        wlo=w_lru_out.astype(BF16), wao=w_attn_out.astype(BF16), wo=w_o.astype(BF16),
        norm_moe_g=row(norm_moe_g), wr=wr, wg=wg, wu=wu, wd=wd)


def _prompt_mixers(x, p):
    bsz, t, _ = x.shape
    assert t % WINDOW == 0 and t >= CONV_W - 1
    n = bsz * t
    x2 = x.reshape(n, D_MODEL)
    xl, q, k, v, gl, ga = _inproj(x2, p['norm_mix_g'], p['w_in'], _row_tile(n, 512))
    tc = _row_tile(t, 512)
    yl, h_last = _lru(xl, jnp.zeros((bsz, SUBLANES, LRU_WIDTH), F32),
                      jnp.zeros((bsz, 1, LRU_WIDTH), F32), p['conv_w'], p['conv_b'], p['wbd'],
                      p['ba'], p['bx'], p['lam'], nseq=bsz, seqlen=t, bb=1, tc=tc)
    k3 = k.reshape(bsz, t, KV_COLS)
    v3 = v.reshape(bsz, t, KV_COLS)
    nblk = _divisor_at_most(t // WINDOW, ATTN_BLOCKS)
    ya = _attn_prompt(q.reshape(bsz, t, -1), k3, v3, p['sinks'], _prompt_bias(), nblk)
    new_conv = xl.reshape(bsz, t, LRU_WIDTH)[:, t - (CONV_W - 1):]
    new_k = k3[:, t - WINDOW:].reshape(bsz, WINDOW, N_KV_HEADS, HEAD_DIM)
    new_v = v3[:, t - WINDOW:].reshape(bsz, WINDOW, N_KV_HEADS, HEAD_DIM)
    return ((x2, yl, ya.reshape(n, -1), gl, ga),
            (new_conv, h_last.reshape(bsz, LRU_WIDTH), new_k, new_v))


def _sample_mixers(x, conv_buf, h0, k_buf, v_buf, p):
    bsz, t, _ = x.shape
    assert t % SUBLANES == 0 and CONV_W - 1 <= t <= WINDOW
    n = bsz * t
    x2 = x.reshape(n, D_MODEL)
    xl, q, k, v, gl, ga = _inproj(x2, p['norm_mix_g'], p['w_in'], _row_tile(n, 512))
    bb = _row_tile(bsz, 16)
    prev8 = jnp.pad(conv_buf, ((0, 0), (SUBLANES - (CONV_W - 1), 0), (0, 0)))
    yl, h_last = _lru(xl, prev8, h0.reshape(bsz, 1, LRU_WIDTH), p['conv_w'], p['conv_b'],
                      p['wbd'], p['ba'], p['bx'], p['lam'], nseq=bsz, seqlen=t, bb=bb, tc=t)
    q4 = q.reshape(bsz, t, N_KV_HEADS, GQA, HEAD_DIM).transpose(0, 2, 3, 1, 4).reshape(
        bsz, N_KV_HEADS, GQA * t, HEAD_DIM)
    bb_attn = _row_tile(bsz, SAMPLE_SEQS)
    sink_rows = jnp.tile(jnp.repeat(p['sinks'].reshape(N_KV_HEADS, GQA), t, axis=1),
                         (1, bb_attn))[:, :, None]
    o4, new_k, new_v = _attn_sample(
        q4, k.reshape(bsz, t, KV_COLS), v.reshape(bsz, t, KV_COLS),
        k_buf.reshape(bsz, WINDOW, KV_COLS), v_buf.reshape(bsz, WINDOW, KV_COLS),
        _sample_bias(t, bb_attn), sink_rows, bb_attn)
    ya = o4.reshape(bsz, N_KV_HEADS, GQA, t, HEAD_DIM).transpose(0, 3, 1, 2, 4).reshape(n, -1)
    new_conv = xl.reshape(bsz, t, LRU_WIDTH)[:, t - (CONV_W - 1):]
    shape = (bsz, WINDOW, N_KV_HEADS, HEAD_DIM)
    return ((x2, yl, ya, gl, ga),
            (new_conv, h_last.reshape(bsz, LRU_WIDTH), new_k.reshape(shape), new_v.reshape(shape)))


def _merge_and_moe(acts_a, acts_b, p, norm_final_g):
    n_a, n_b = acts_a[0].shape[0], acts_b[0].shape[0]
    n = n_a + n_b
    tm = _row_tile(math.gcd(n_a, n_b), MERGE_ROWS)
    x1, xn, rec, cnt = _merge(acts_a, acts_b, p, tm)
    plan, pad_plan, tile_plan, n_rows = _moe_plan(cnt, n, tm, EXPERT_ROWS)
    xs = _dispatch(plan, pad_plan, tile_plan[1], xn, rec, n_rows, tm, EXPERT_ROWS)
    ys = _experts(tile_plan, xs, p['wg'], p['wu'], p['wd'], EXPERT_ROWS)
    return _combine(plan, ys, x1, rec, norm_final_g.reshape(1, -1), n_a, tm)


def kernel(x_prompt, x_sample, state_conv, state_lru_h, cache_win_k, cache_win_v, norm_mix_g, w_in, conv_w, conv_b, lru_wa, lru_ba, lru_wx, lru_bx, lru_lambda, attn_sinks, w_lru_out, w_attn_out, w_o, norm_moe_g, w_router_group, w_router_expert, moe_w_gate, moe_w_up, moe_w_down, norm_final_g):
    depth = w_in.shape[0]
    assert depth == 1, "the final norm is fused into the single layer's MoE kernel"
    p = _layer_params(norm_mix_g[0], w_in[0], conv_w[0], conv_b[0], lru_wa[0], lru_ba[0],
                      lru_wx[0], lru_bx[0], lru_lambda[0], attn_sinks[0], w_lru_out[0],
                      w_attn_out[0], w_o[0], norm_moe_g[0], w_router_group[0],
                      w_router_expert[0], moe_w_gate[0], moe_w_up[0], moe_w_down[0])
    acts_p, (c1, h1, k1, v1) = _prompt_mixers(x_prompt, p)
    acts_s, (c2, h2, k2, v2) = _sample_mixers(x_sample, state_conv[0], state_lru_h[0],
                                              cache_win_k[0], cache_win_v[0], p)
    yp, ys = _merge_and_moe(acts_p, acts_s, p, norm_final_g)
    return (yp.reshape(x_prompt.shape), ys.reshape(x_sample.shape), c1[None], h1[None], k1[None],
            v1[None], c2[None], h2[None], k2[None], v2[None])
```

```python
import functools
import math

import jax
import jax.numpy as jnp
from jax import lax
from jax.experimental import pallas as pl
from jax.experimental.pallas import tpu as pltpu

D_MODEL = 1024
LRU_WIDTH = D_MODEL
LRU_BLOCKS = 16
LRU_BLOCK_DIM = LRU_WIDTH // LRU_BLOCKS
LRU_C = 8.0
CONV_W = 4
N_HEADS = 16
N_KV_HEADS = 4
GQA = N_HEADS // N_KV_HEADS
HEAD_DIM = D_MODEL // N_HEADS
KV_COLS = N_KV_HEADS * HEAD_DIM
WINDOW = 128
N_GROUPS = 4
EXPERTS_PER_GROUP = 8
N_EXPERTS = N_GROUPS * EXPERTS_PER_GROUP
D_EXPERT = D_MODEL // 4
RMS_EPS = 1e-6
NEG_INF = -1e30
IN_COLS = LRU_WIDTH + (N_HEADS + 2 * N_KV_HEADS) * HEAD_DIM + 2 * D_MODEL

SUBLANES = 8
LANES = 128
ROUTER_COLS = LANES
GATE_PACK = 256

MERGE_ROWS = 512
EXPERT_ROWS = 512
RUN_ALIGN = 16
PIECE_ROWS = (32, RUN_ALIGN)
ATTN_BLOCKS = 4
DISPATCH_SLOTS = 3
SAMPLE_SEQS = 8

F32 = jnp.float32
BF16 = jnp.bfloat16


def _sigmoid(z):
    return 1.0 / (1.0 + jnp.exp(-z))


def _one_minus_exp2(x, ex):
    return -jnp.tanh(x) * (ex * ex + 1.0)


def _rms_scale(x):
    return x * lax.rsqrt(jnp.mean(x * x, axis=-1, keepdims=True) + RMS_EPS)


def _inproj_kernel(x_ref, g_ref, w_ref, xl_ref, q_ref, k_ref, v_ref, gl_ref, ga_ref):
    xb = (_rms_scale(x_ref[...]) * g_ref[...]).astype(BF16)

    def proj(c0, c1):
        return jnp.dot(xb, w_ref[:, c0:c1], preferred_element_type=F32)

    c0 = LRU_WIDTH
    c1 = c0 + N_HEADS * HEAD_DIM
    c2 = c1 + KV_COLS
    c3 = c2 + KV_COLS
    c4 = c3 + D_MODEL
    xl_ref[...] = proj(0, c0)
    q_ref[...] = (proj(c0, c1) * (HEAD_DIM ** -0.5)).astype(BF16)
    k_ref[...] = proj(c1, c2)
    v_ref[...] = proj(c2, c3)
    gl_ref[...] = _sigmoid(proj(c3, c4)).astype(BF16)
    ga_ref[...] = _sigmoid(proj(c4, IN_COLS)).astype(BF16)


def _inproj(x2, g, w_bf, tm):
    n = x2.shape[0]
    row = lambda i: (i, 0)
    fixed = lambda i: (0, 0)
    outs = [
        jax.ShapeDtypeStruct((n, LRU_WIDTH), F32),
        jax.ShapeDtypeStruct((n, N_HEADS * HEAD_DIM), BF16),
        jax.ShapeDtypeStruct((n, KV_COLS), F32),
        jax.ShapeDtypeStruct((n, KV_COLS), F32),
        jax.ShapeDtypeStruct((n, D_MODEL), BF16),
        jax.ShapeDtypeStruct((n, D_MODEL), BF16),
    ]
    return pl.pallas_call(
        _inproj_kernel,
        out_shape=outs,
        grid=(n // tm,),
        in_specs=[
            pl.BlockSpec((tm, D_MODEL), row),
            pl.BlockSpec((1, D_MODEL), fixed),
            pl.BlockSpec((D_MODEL, IN_COLS), fixed),
        ],
        out_specs=[pl.BlockSpec((tm, o.shape[1]), row) for o in outs],
        compiler_params=pltpu.CompilerParams(dimension_semantics=("parallel",)),
        name="inproj",
    )(x2, g, w_bf)


def _lru_kernel(xl_ref, prev_ref, h0_ref, cw_ref, cb_ref, wbd_ref, ba_ref, bx_ref, lam_ref,
                y_ref, hl_ref, xs_ref, xc_ref, a_ref, b_ref, hc_ref, *, bb, tc):
    t = pl.program_id(1)

    @pl.when(t == 0)
    def _():
        xs_ref[:, 0:SUBLANES, :] = prev_ref[...]
        hc_ref[...] = h0_ref[...]

    @pl.when(t > 0)
    def _():
        xs_ref[:, 0:SUBLANES, :] = xs_ref[:, tc:tc + SUBLANES, :]

    for s in range(bb):
        xs_ref[s, SUBLANES:SUBLANES + tc, :] = xl_ref[s * tc:(s + 1) * tc, :]
    cw = cw_ref[...]
    for s in range(bb):
        acc = cb_ref[...] + xs_ref[s, SUBLANES:SUBLANES + tc, :] * cw[CONV_W - 1:CONV_W, :]
        for j in range(CONV_W - 1):
            off = SUBLANES - (CONV_W - 1) + j
            acc = acc + xs_ref[s, off:off + tc, :] * cw[j:j + 1, :]
        xc_ref[s * tc:(s + 1) * tc, :] = acc

    lam = lam_ref[...]
    softplus_neg_lam = jnp.maximum(-lam, 0.0) + jnp.log1p(jnp.exp(-jnp.abs(lam)))
    for g in range(LRU_WIDTH // GATE_PACK):
        cols = slice(g * GATE_PACK, (g + 1) * GATE_PACK)
        xc = xc_ref[:, cols]
        gates = jnp.dot(xc.astype(BF16), wbd_ref[g], preferred_element_type=F32)
        r = _sigmoid(gates[:, :GATE_PACK] + ba_ref[:, cols])
        ig = _sigmoid(gates[:, GATE_PACK:] + bx_ref[:, cols])
        log_a = (-LRU_C) * r * softplus_neg_lam[:, cols]
        a = jnp.exp(log_a)
        a_ref[:, cols] = a
        b_ref[:, cols] = jnp.sqrt(_one_minus_exp2(log_a, a)) * (ig * xc)

    row = lax.broadcasted_iota(jnp.int32, (SUBLANES, LRU_WIDTH), 0)
    for s in range(bb):
        def group(gi, h8, s=s):
            r0 = pl.multiple_of(s * tc + gi * SUBLANES, SUBLANES)
            a = a_ref[pl.ds(r0, SUBLANES), :]
            b = b_ref[pl.ds(r0, SUBLANES), :]
            for d in (1, 2, 4):
                a_up = jnp.where(row >= d, pltpu.roll(a, d, 0), 1.0)
                b_up = jnp.where(row >= d, pltpu.roll(b, d, 0), 0.0)
                b = a * b_up + b
                a = a * a_up
            h = a * h8 + b
            b_ref[pl.ds(r0, SUBLANES), :] = h
            return jnp.broadcast_to(h[SUBLANES - 1:SUBLANES, :], (SUBLANES, LRU_WIDTH))

        h8 = lax.fori_loop(0, tc // SUBLANES, group,
                           jnp.broadcast_to(hc_ref[s], (SUBLANES, LRU_WIDTH)))
        hc_ref[s] = h8[0:1, :]

    y_ref[...] = b_ref[...].astype(y_ref.dtype)

    @pl.when(t == pl.num_programs(1) - 1)
    def _():
        hl_ref[...] = hc_ref[...]


def _lru(xl2, prev8, h0, cw, cb, wbd, ba, bx, lam, *, nseq, seqlen, bb, tc):
    nt = seqlen // tc
    assert bb == 1 or nt == 1
    rows = bb * tc
    fixed2 = lambda b, t: (0, 0)
    fixed3 = lambda b, t: (0, 0, 0)
    return pl.pallas_call(
        functools.partial(_lru_kernel, bb=bb, tc=tc),
        out_shape=[
            jax.ShapeDtypeStruct((nseq * seqlen, LRU_WIDTH), BF16),
            jax.ShapeDtypeStruct((nseq, 1, LRU_WIDTH), F32),
        ],
        grid=(nseq // bb, nt),
        in_specs=[
            pl.BlockSpec((rows, LRU_WIDTH), lambda b, t: (b * nt + t, 0)),
            pl.BlockSpec((bb, SUBLANES, LRU_WIDTH), lambda b, t: (b, 0, 0)),
            pl.BlockSpec((bb, 1, LRU_WIDTH), lambda b, t: (b, 0, 0)),
            pl.BlockSpec((CONV_W, LRU_WIDTH), fixed2),
            pl.BlockSpec((1, LRU_WIDTH), fixed2),
            pl.BlockSpec((LRU_WIDTH // GATE_PACK, GATE_PACK, 2 * GATE_PACK), fixed3),
            pl.BlockSpec((1, LRU_WIDTH), fixed2),
            pl.BlockSpec((1, LRU_WIDTH), fixed2),
            pl.BlockSpec((1, LRU_WIDTH), fixed2),
        ],
        out_specs=[
            pl.BlockSpec((rows, LRU_WIDTH), lambda b, t: (b * nt + t, 0)),
            pl.BlockSpec((bb, 1, LRU_WIDTH), lambda b, t: (b, 0, 0)),
        ],
        scratch_shapes=[
            pltpu.VMEM((bb, tc + SUBLANES, LRU_WIDTH), F32),
            pltpu.VMEM((rows, LRU_WIDTH), F32),
            pltpu.VMEM((rows, LRU_WIDTH), F32),
            pltpu.VMEM((rows, LRU_WIDTH), F32),
            pltpu.VMEM((bb, 1, LRU_WIDTH), F32),
        ],
        compiler_params=pltpu.CompilerParams(dimension_semantics=("parallel", "arbitrary")),
        name="conv_rglru",
    )(xl2, prev8, h0, cw, cb, wbd, ba, bx, lam)


def _softmax_sink_pv(s, sink, v_bf):
    m = jnp.maximum(jnp.max(s, axis=-1, keepdims=True), sink)
    p = jnp.exp(s - m)
    denom = jnp.sum(p, axis=-1, keepdims=True) + jnp.exp(sink - m)
    o = jnp.dot(p.astype(BF16), v_bf, preferred_element_type=F32)
    return o / denom


def _attn_prompt_kernel(sink_ref, q_ref, kc_ref, kp_ref, vc_ref, vp_ref, bias_ref, o_ref,
                        k_ref, v_ref, *, nblk):
    k_ref[0:WINDOW, :] = kp_ref[0].astype(BF16)
    k_ref[WINDOW:, :] = kc_ref[0].astype(BF16)
    v_ref[0:WINDOW, :] = vp_ref[0].astype(BF16)
    v_ref[WINDOW:, :] = vc_ref[0].astype(BF16)

    def block(j, carry):
        bsel = jnp.where((pl.program_id(1) == 0) & (j == 0), 0, 1)
        r0 = pl.multiple_of(j * WINDOW, WINDOW)
        q = q_ref[0, pl.ds(r0, WINDOW), :]
        kcat = k_ref[pl.ds(r0, 2 * WINDOW), :]
        vcat = v_ref[pl.ds(r0, 2 * WINDOW), :]
        outs = []
        for h in range(N_HEADS):
            kv = slice((h // GQA) * HEAD_DIM, (h // GQA + 1) * HEAD_DIM)
            s = lax.dot_general(q[:, h * HEAD_DIM:(h + 1) * HEAD_DIM], kcat[:, kv],
                                (((1,), (1,)), ((), ())), preferred_element_type=F32)
            s = s + bias_ref[bsel, h]
            outs.append(_softmax_sink_pv(s, sink_ref[h], vcat[:, kv]))
        o_ref[0, pl.ds(r0, WINDOW), :] = jnp.concatenate(outs, axis=-1).astype(o_ref.dtype)
        return carry

    lax.fori_loop(0, nblk, block, 0)


def _attn_prompt(q3, k3, v3, sinks, bias, nblk):
    bsz, t, _ = q3.shape
    rows = nblk * WINDOW
    cur = lambda b, i: (b, i, 0)
    prev = lambda b, i: (b, jnp.maximum(i * nblk - 1, 0), 0)
    return pl.pallas_call(
        functools.partial(_attn_prompt_kernel, nblk=nblk),
        out_shape=jax.ShapeDtypeStruct((bsz, t, N_HEADS * HEAD_DIM), BF16),
        grid=(bsz, t // rows),
        in_specs=[
            pl.BlockSpec(memory_space=pltpu.SMEM),
            pl.BlockSpec((1, rows, N_HEADS * HEAD_DIM), cur),
            pl.BlockSpec((1, rows, KV_COLS), cur),
            pl.BlockSpec((1, WINDOW, KV_COLS), prev),
            pl.BlockSpec((1, rows, KV_COLS), cur),
            pl.BlockSpec((1, WINDOW, KV_COLS), prev),
            pl.BlockSpec((2, N_HEADS, WINDOW, 2 * WINDOW), lambda b, i: (0, 0, 0, 0)),
        ],
        out_specs=pl.BlockSpec((1, rows, N_HEADS * HEAD_DIM), cur),
        scratch_shapes=[pltpu.VMEM((rows + WINDOW, KV_COLS), BF16),
                        pltpu.VMEM((rows + WINDOW, KV_COLS), BF16)],
        compiler_params=pltpu.CompilerParams(dimension_semantics=("parallel", "arbitrary")),
        name="swa_prompt",
    )(sinks, q3, k3, k3, v3, v3, bias)


def _attn_sample_kernel(q_ref, kn_ref, vn_ref, ck_ref, cv_ref, bias_ref, sink_ref,
                        o_ref, nk_ref, nv_ref, *, bb, t):
    keep = WINDOW - t
    for s in range(bb):
        nk_ref[s, 0:keep, :] = ck_ref[s, t:WINDOW, :]
        nk_ref[s, keep:WINDOW, :] = kn_ref[s]
        nv_ref[s, 0:keep, :] = cv_ref[s, t:WINDOW, :]
        nv_ref[s, keep:WINDOW, :] = vn_ref[s]
    kall = jnp.concatenate([ck_ref[...].reshape(bb * WINDOW, KV_COLS),
                            kn_ref[...].reshape(bb * t, KV_COLS)], axis=0).astype(BF16)
    vall = jnp.concatenate([cv_ref[...].reshape(bb * WINDOW, KV_COLS),
                            vn_ref[...].reshape(bb * t, KV_COLS)], axis=0).astype(BF16)
    rows = q_ref.shape[2]
    for kh in range(N_KV_HEADS):
        kv = slice(kh * HEAD_DIM, (kh + 1) * HEAD_DIM)
        qh = q_ref[:, kh].reshape(bb * rows, HEAD_DIM)
        sc = lax.dot_general(qh, kall[:, kv], (((1,), (1,)), ((), ())),
                             preferred_element_type=F32)
        o = _softmax_sink_pv(sc + bias_ref[kh], sink_ref[kh], vall[:, kv])
        o_ref[:, kh] = o.reshape(bb, rows, HEAD_DIM).astype(o_ref.dtype)


def _attn_sample(q4, kn3, vn3, ck3, cv3, bias, sink_rows, bb):
    nseq, _, rows, _ = q4.shape
    t = kn3.shape[1]
    b3 = lambda i: (i, 0, 0)
    b4 = lambda i: (i, 0, 0, 0)
    return pl.pallas_call(
        functools.partial(_attn_sample_kernel, bb=bb, t=t),
        out_shape=[
            jax.ShapeDtypeStruct((nseq, N_KV_HEADS, rows, HEAD_DIM), BF16),
            jax.ShapeDtypeStruct((nseq, WINDOW, KV_COLS), F32),
            jax.ShapeDtypeStruct((nseq, WINDOW, KV_COLS), F32),
        ],
        grid=(nseq // bb,),
        in_specs=[
            pl.BlockSpec((bb, N_KV_HEADS, rows, HEAD_DIM), b4),
            pl.BlockSpec((bb, t, KV_COLS), b3),
            pl.BlockSpec((bb, t, KV_COLS), b3),
            pl.BlockSpec((bb, WINDOW, KV_COLS), b3),
            pl.BlockSpec((bb, WINDOW, KV_COLS), b3),
            pl.BlockSpec((N_KV_HEADS, bb * rows, bb * (WINDOW + t)), lambda i: (0, 0, 0)),
            pl.BlockSpec((N_KV_HEADS, bb * rows, 1), lambda i: (0, 0, 0)),
        ],
        out_specs=[
            pl.BlockSpec((bb, N_KV_HEADS, rows, HEAD_DIM), b4),
            pl.BlockSpec((bb, WINDOW, KV_COLS), b3),
            pl.BlockSpec((bb, WINDOW, KV_COLS), b3),
        ],
        compiler_params=pltpu.CompilerParams(dimension_semantics=("parallel",)),
        name="swa_sample",
    )(q4, kn3, vn3, ck3, cv3, bias, sink_rows)


R_E1, R_E2, R_D1, R_D2, R_W1, R_W2 = range(6)


def _route(logits):
    rows = logits.shape[0]
    lane = lax.broadcasted_iota(jnp.int32, logits.shape, 1)
    big = jnp.int32(ROUTER_COLS)
    gl = jnp.where(lane < N_GROUPS, logits, -jnp.inf)
    gmax = jnp.max(gl, axis=-1, keepdims=True)
    p_grp = 1.0 / jnp.sum(jnp.exp(gl - gmax), axis=-1, keepdims=True)
    g_idx = jnp.min(jnp.where(gl == gmax, lane, big), axis=-1, keepdims=True)
    e_lo = N_GROUPS + EXPERTS_PER_GROUP * g_idx
    el = jnp.where((lane >= e_lo) & (lane < e_lo + EXPERTS_PER_GROUP), logits, -jnp.inf)
    m1 = jnp.max(el, axis=-1, keepdims=True)
    i1 = jnp.min(jnp.where(el == m1, lane, big), axis=-1, keepdims=True)
    el2 = jnp.where(lane == i1, -jnp.inf, el)
    m2 = jnp.max(el2, axis=-1, keepdims=True)
    i2 = jnp.min(jnp.where(el2 == m2, lane, big), axis=-1, keepdims=True)
    e21 = jnp.exp(m2 - m1)
    w1 = p_grp / (1.0 + e21)
    w2 = e21 * w1
    hit1 = lane == i1
    hit2 = lane == i2
    onehot = jnp.where(hit1 | hit2, 1.0, 0.0).astype(BF16)
    r_i = lax.broadcasted_iota(jnp.int32, (rows, rows), 0)
    c_i = lax.broadcasted_iota(jnp.int32, (rows, rows), 1)
    tril = jnp.where(c_i <= r_i, 1.0, 0.0).astype(BF16)
    upto = jnp.dot(tril, onehot, preferred_element_type=F32)
    count = upto[rows - 1:rows, :]
    run_u = jnp.floor((count + (RUN_ALIGN - 1)) * (1.0 / RUN_ALIGN))
    k_i = lax.broadcasted_iota(jnp.int32, (ROUTER_COLS, ROUTER_COLS), 0)
    j_i = lax.broadcasted_iota(jnp.int32, (ROUTER_COLS, ROUTER_COLS), 1)
    before = jnp.where(k_i < j_i, 1.0, 0.0).astype(BF16)
    start_u = jnp.dot(jnp.broadcast_to(run_u, (SUBLANES, ROUTER_COLS)).astype(BF16), before,
                     preferred_element_type=F32)[0:1, :]
    dest = RUN_ALIGN * start_u + upto - 1.0
    d1 = jnp.sum(jnp.where(hit1, dest, 0.0), axis=-1, keepdims=True)
    d2 = jnp.sum(jnp.where(hit2, dest, 0.0), axis=-1, keepdims=True)
    rec = jnp.zeros(logits.shape, F32)
    for idx, val in ((R_E1, (i1 - N_GROUPS).astype(F32)), (R_E2, (i2 - N_GROUPS).astype(F32)),
                     (R_D1, d1), (R_D2, d2), (R_W1, w1), (R_W2, w2)):
        rec = jnp.where(lane == idx, val, rec)
    return rec, count


def _merge_kernel(*refs, a_tiles):
    acts_a, acts_b = refs[0:5], refs[5:10]
    wlo_ref, wao_ref, wo_ref, g_ref, wr_ref, x1_ref, xn_ref, rec_ref, cnt_ref = refs[10:]

    def tile(x_ref, yl_ref, ya_ref, gl_ref, ga_ref):
        a = jnp.dot(yl_ref[...], wlo_ref[...], preferred_element_type=F32)
        b = jnp.dot(ya_ref[...], wao_ref[...], preferred_element_type=F32)
        merged = gl_ref[...].astype(F32) * a + ga_ref[...].astype(F32) * b
        x1 = x_ref[...] + jnp.dot(merged.astype(BF16), wo_ref[...], preferred_element_type=F32)
        x1_ref[...] = x1
        xn = _rms_scale(x1) * g_ref[...]
        xh = xn.astype(BF16)
        xn_ref[...] = xh
        xlo = (xn - xh.astype(F32)).astype(BF16)
        both = jnp.dot(xh, wr_ref[...], preferred_element_type=F32)
        logits = (both[:, :ROUTER_COLS] + both[:, ROUTER_COLS:]
                  + jnp.dot(xlo, wr_ref[:, :ROUTER_COLS], preferred_element_type=F32))
        rec, count = _route(logits)
        rec_ref[...] = rec
        cnt_ref[0] = jnp.broadcast_to(count, (SUBLANES, ROUTER_COLS))

    pl.when(pl.program_id(0) < a_tiles)(lambda: tile(*acts_a))
    pl.when(pl.program_id(0) >= a_tiles)(lambda: tile(*acts_b))


def _merge(acts_a, acts_b, p, tm):
    n_a, n_b = acts_a[0].shape[0], acts_b[0].shape[0]
    n = n_a + n_b
    a_tiles = n_a // tm
    row = lambda i: (i, 0)
    fixed = lambda i: (0, 0)
    act_a = pl.BlockSpec((tm, D_MODEL), lambda i: (jnp.minimum(i, a_tiles - 1), 0))
    act_b = pl.BlockSpec((tm, D_MODEL), lambda i: (jnp.maximum(i - a_tiles, 0), 0))
    wsq = pl.BlockSpec((D_MODEL, D_MODEL), fixed)
    wr = pl.BlockSpec((D_MODEL, 2 * ROUTER_COLS), fixed)
    return pl.pallas_call(
        functools.partial(_merge_kernel, a_tiles=a_tiles),
        out_shape=[
            jax.ShapeDtypeStruct((n, D_MODEL), F32),
            jax.ShapeDtypeStruct((n, D_MODEL), BF16),
            jax.ShapeDtypeStruct((n, ROUTER_COLS), F32),
            jax.ShapeDtypeStruct((n // tm, SUBLANES, ROUTER_COLS), F32),
        ],
        grid=(n // tm,),
        in_specs=[act_a] * 5 + [act_b] * 5 + [wsq, wsq, wsq, pl.BlockSpec((1, D_MODEL), fixed), wr],
        out_specs=[pl.BlockSpec((tm, D_MODEL), row),
                   pl.BlockSpec((tm, D_MODEL), row),
                   pl.BlockSpec((tm, ROUTER_COLS), row),
                   pl.BlockSpec((1, SUBLANES, ROUTER_COLS), lambda i: (i, 0, 0))],
        compiler_params=pltpu.CompilerParams(dimension_semantics=("parallel",)),
        name="merge_router",
    )(*acts_a, *acts_b, p['wlo'], p['wao'], p['wo'], p['norm_moe_g'], p['wr'])


def _tile_rows(tm):
    return -(-(2 * tm + N_EXPERTS * (RUN_ALIGN - 1)) // LANES) * LANES


def _max_pieces(rv):
    big, small = PIECE_ROWS
    return rv // big, N_EXPERTS * (big // small - 1)


def _for_each_piece(i, plan_refs, rv, fn):
    big_src, big_dst, small_src, small_dst, nbig_ref, nsmall_ref = plan_refs
    max_big, max_small = _max_pieces(rv)
    for src_ref, dst_ref, n_ref, stride, rows in ((big_src, big_dst, nbig_ref, max_big, PIECE_ROWS[0]),
                                                  (small_src, small_dst, nsmall_ref, max_small,
                                                   PIECE_ROWS[1])):
        def piece(k, carry, src_ref=src_ref, dst_ref=dst_ref, stride=stride, rows=rows):
            fn(pl.multiple_of(src_ref[i * stride + k], RUN_ALIGN),
               pl.multiple_of(dst_ref[i * stride + k], RUN_ALIGN), rows)
            return carry

        lax.fori_loop(0, n_ref[i], piece, 0)


def _drain(copy_of, n_big, n_small):
    big, small = PIECE_ROWS
    lax.fori_loop(0, n_big, lambda k, c: (copy_of(big).wait(), c)[1], 0)
    lax.fori_loop(0, n_small, lambda k, c: (copy_of(small).wait(), c)[1], 0)


def _dispatch_kernel(*refs):
    plan_refs = refs[:6]
    nbig_ref, nsmall_ref = plan_refs[4:]
    (pad_start_ref, pad_nu_ref, pad_tot_ref, used_ref, xn_ref, rec_ref, xs_ref, buf_ref, zero_ref,
     sem, zsem) = refs[6:]
    i = pl.program_id(0)
    nslot = buf_ref.shape[0]
    slot = i % nslot
    rv = buf_ref.shape[1]
    rec_t = jnp.transpose(rec_ref[...])
    row = lax.broadcasted_iota(jnp.int32, (rv, rec_t.shape[1]), 0).astype(F32)
    place = jnp.where((row == rec_t[R_D1:R_D1 + 1, :]) | (row == rec_t[R_D2:R_D2 + 1, :]), 1.0, 0.0)
    buf_ref[slot] = jnp.dot(place.astype(BF16), xn_ref[...],
                            preferred_element_type=F32).astype(buf_ref.dtype)

    def piece_copy(s, src_row, dst_row, rows):
        return pltpu.make_async_copy(buf_ref.at[s, pl.ds(src_row, rows)],
                                     xs_ref.at[pl.ds(dst_row, rows)], sem.at[s])

    def drain(s, n_big, n_small):
        _drain(lambda rows: piece_copy(s, 0, 0, rows), n_big, n_small)

    @pl.when(i >= nslot - 1)
    def _():
        drain((i + 1) % nslot, nbig_ref[i - (nslot - 1)], nsmall_ref[i - (nslot - 1)])

    _for_each_piece(i, plan_refs, rv,
                    lambda src_row, dst_row, rows: piece_copy(slot, src_row, dst_row, rows).start())

    @pl.when(i == pl.num_programs(0) - 1)
    def _():
        zero_ref[...] = jnp.zeros_like(zero_ref)

        def pad_expert(e, carry):
            def piece(k, c):
                dst_row = pl.multiple_of(pad_start_ref[e] + k * RUN_ALIGN, RUN_ALIGN)
                pltpu.make_async_copy(zero_ref.at[pl.ds(0, RUN_ALIGN)],
                                      xs_ref.at[pl.ds(dst_row, RUN_ALIGN)], sem.at[slot]).start()
                return c

            lax.fori_loop(0, pad_nu_ref[e], piece, 0)
            return carry

        lax.fori_loop(0, N_EXPERTS, pad_expert, 0)

        te_rows = zero_ref.shape[0]

        def tail_copy(t):
            return pltpu.make_async_copy(
                zero_ref, xs_ref.at[pl.ds(pl.multiple_of(t * te_rows, te_rows), te_rows)], zsem)

        n_tiles = xs_ref.shape[0] // te_rows
        lax.fori_loop(used_ref[0], n_tiles, lambda t, c: (tail_copy(t).start(), c)[1], 0)
        for back in range(1, nslot - 1):
            @pl.when(i >= back)
            def _(back=back):
                drain((i - back) % nslot, nbig_ref[i - back], nsmall_ref[i - back])
        drain(slot, nbig_ref[i], nsmall_ref[i] + pad_tot_ref[0])
        lax.fori_loop(used_ref[0], n_tiles, lambda t, c: (tail_copy(t).wait(), c)[1], 0)


def _dispatch(plan, pad_plan, n_used, xn, rec, n_rows, tm, te_rows):
    n = xn.shape[0]
    rv = _tile_rows(tm)
    row = lambda i, *_: (i, 0)
    return pl.pallas_call(
        _dispatch_kernel,
        out_shape=jax.ShapeDtypeStruct((n_rows, D_MODEL), BF16),
        grid_spec=pltpu.PrefetchScalarGridSpec(
            num_scalar_prefetch=10,
            grid=(n // tm,),
            in_specs=[pl.BlockSpec((tm, D_MODEL), row), pl.BlockSpec((tm, ROUTER_COLS), row)],
            out_specs=pl.BlockSpec(memory_space=pl.ANY),
            scratch_shapes=[pltpu.VMEM((DISPATCH_SLOTS, rv, D_MODEL), BF16),
                            pltpu.VMEM((te_rows, D_MODEL), BF16),
                            pltpu.SemaphoreType.DMA((DISPATCH_SLOTS,)), pltpu.SemaphoreType.DMA],
        ),
        compiler_params=pltpu.CompilerParams(dimension_semantics=("arbitrary",)),
        name="moe_dispatch",
    )(*plan, *pad_plan, n_used, xn, rec)


def _experts_kernel(te_ref, nu_ref, first_ref, next_ref, slot_ref, xs_ref, wg_hbm, wu_hbm, wd_hbm,
                    ys_ref, wg_stage, wu_stage, wd_stage, wg_bf, wu_bf, wd_bf, sem):
    t = pl.program_id(0)

    def weight_copies(e, s):
        return (pltpu.make_async_copy(wg_hbm.at[e], wg_stage.at[s], sem.at[s, 0]),
                pltpu.make_async_copy(wu_hbm.at[e], wu_stage.at[s], sem.at[s, 1]),
                pltpu.make_async_copy(wd_hbm.at[e], wd_stage.at[s], sem.at[s, 2]))

    @pl.when(t < nu_ref[0])
    def _():
        @pl.when(first_ref[t] == 1)
        def _():
            s = slot_ref[t]

            @pl.when(t == 0)
            def _():
                for copy in weight_copies(te_ref[0], s):
                    copy.start()

            for copy in weight_copies(te_ref[t], s):
                copy.wait()
            wg_bf[...] = wg_stage[s].astype(BF16)
            wu_bf[...] = wu_stage[s].astype(BF16)
            wd_bf[...] = wd_stage[s].astype(BF16)

            @pl.when(next_ref[t] >= 0)
            def _():
                for copy in weight_copies(next_ref[t], 1 - s):
                    copy.start()

        x = xs_ref[...]
        gate = jnp.dot(x, wg_bf[...], preferred_element_type=F32)
        up = jnp.dot(x, wu_bf[...], preferred_element_type=F32)
        h = gate * _sigmoid(gate) * up
        ys_ref[...] = jnp.dot(h.astype(BF16), wd_bf[...],
                              preferred_element_type=F32).astype(ys_ref.dtype)

    @pl.when(t >= nu_ref[0])
    def _():
        ys_ref[...] = jnp.zeros_like(ys_ref)


def _experts(tile_plan, xs, wg, wu, wd, te_rows):
    n_rows, c = xs.shape
    used = lambda t, te, nu, *_: (jnp.minimum(t, nu[0] - 1), 0)
    hbm = pl.BlockSpec(memory_space=pl.ANY)
    return pl.pallas_call(
        _experts_kernel,
        out_shape=jax.ShapeDtypeStruct((n_rows, c), xs.dtype),
        grid_spec=pltpu.PrefetchScalarGridSpec(
            num_scalar_prefetch=5,
            grid=(n_rows // te_rows,),
            in_specs=[pl.BlockSpec((te_rows, c), used), hbm, hbm, hbm],
            out_specs=pl.BlockSpec((te_rows, c), lambda t, *_: (t, 0)),
            scratch_shapes=[
                pltpu.VMEM((2,) + wg.shape[1:], wg.dtype),
                pltpu.VMEM((2,) + wu.shape[1:], wu.dtype),
                pltpu.VMEM((2,) + wd.shape[1:], wd.dtype),
                pltpu.VMEM(wg.shape[1:], BF16),
                pltpu.VMEM(wu.shape[1:], BF16),
                pltpu.VMEM(wd.shape[1:], BF16),
                pltpu.SemaphoreType.DMA((2, 3)),
            ],
        ),
        compiler_params=pltpu.CompilerParams(dimension_semantics=("arbitrary",)),
        name="moe_experts",
    )(*tile_plan, xs, wg, wu, wd)


def _combine_kernel(*refs, a_tiles):
    plan_refs = refs[:6]
    nbig_ref, nsmall_ref = plan_refs[4:]
    ys_ref, x1_ref, rec_ref, g_ref, ya_ref, yb_ref, buf_ref, sem = refs[6:]
    i = pl.program_id(0)
    last = pl.num_programs(0) - 1
    slot = i % 2
    rv = buf_ref.shape[1]

    def piece_copy(s, src_row, dst_row, rows):
        return pltpu.make_async_copy(ys_ref.at[pl.ds(src_row, rows)],
                                     buf_ref.at[s, pl.ds(dst_row, rows)], sem.at[s])

    def fetch(tile, s):
        _for_each_piece(tile, plan_refs, rv,
                        lambda row, src_row, rows: piece_copy(s, src_row, row, rows).start())

    @pl.when(i == 0)
    def _():
        buf_ref[...] = jnp.zeros_like(buf_ref)
        fetch(0, 0)

    @pl.when(i < last)
    def _():
        fetch(i + 1, 1 - slot)

    _drain(lambda rows: piece_copy(slot, 0, 0, rows), nbig_ref[i], nsmall_ref[i])
    rec = rec_ref[...]
    col = lax.broadcasted_iota(jnp.int32, (rec.shape[0], rv), 1).astype(F32)
    weigh = (jnp.where(col == rec[:, R_D1:R_D1 + 1], rec[:, R_W1:R_W1 + 1], 0.0)
             + jnp.where(col == rec[:, R_D2:R_D2 + 1], rec[:, R_W2:R_W2 + 1], 0.0))
    moe = jnp.dot(weigh.astype(BF16), buf_ref[slot], preferred_element_type=F32)
    y = _rms_scale(x1_ref[...] + moe) * g_ref[...]

    @pl.when(i < a_tiles)
    def _():
        ya_ref[...] = y

    @pl.when(i >= a_tiles)
    def _():
        yb_ref[...] = y


def _combine(plan, ys, x1, rec, g, n_a, tm):
    n = x1.shape[0]
    a_tiles = n_a // tm
    rv = _tile_rows(tm)
    row = lambda i, *_: (i, 0)
    return pl.pallas_call(
        functools.partial(_combine_kernel, a_tiles=a_tiles),
        out_shape=[jax.ShapeDtypeStruct((n_a, D_MODEL), F32),
                   jax.ShapeDtypeStruct((n - n_a, D_MODEL), F32)],
        grid_spec=pltpu.PrefetchScalarGridSpec(
            num_scalar_prefetch=6,
            grid=(n // tm,),
            in_specs=[
                pl.BlockSpec(memory_space=pl.ANY),
                pl.BlockSpec((tm, D_MODEL), row),
                pl.BlockSpec((tm, ROUTER_COLS), row),
                pl.BlockSpec((1, D_MODEL), lambda i, *_: (0, 0)),
            ],
            out_specs=[
                pl.BlockSpec((tm, D_MODEL), lambda i, *_: (jnp.minimum(i, a_tiles - 1), 0)),
                pl.BlockSpec((tm, D_MODEL), lambda i, *_: (jnp.maximum(i - a_tiles, 0), 0))],
            scratch_shapes=[pltpu.VMEM((2, rv, D_MODEL), BF16), pltpu.SemaphoreType.DMA((2,))],
        ),
        compiler_params=pltpu.CompilerParams(dimension_semantics=("arbitrary",)),
        name="moe_combine",
    )(*plan, ys, x1, rec, g)


def _moe_plan(counts, n, tm, te_rows):
    cnt = counts[:, 0, N_GROUPS:N_GROUPS + N_EXPERTS].astype(jnp.int32)
    nu = (cnt + RUN_ALIGN - 1) // RUN_ALIGN
    run = nu * RUN_ALIGN
    tiles = (jnp.sum(run, axis=0) + te_rows - 1) // te_rows
    tile_end = jnp.cumsum(tiles)
    first_row = (tile_end - tiles) * te_rows
    start = first_row + jnp.cumsum(run, axis=0) - run
    pad_nu = (tiles * te_rows - jnp.sum(run, axis=0)) // RUN_ALIGN
    pad_plan = (first_row + jnp.sum(run, axis=0), pad_nu, jnp.sum(pad_nu, keepdims=True))
    n_tiles = (2 * n + (n // tm) * N_EXPERTS * (RUN_ALIGN - 1) + te_rows - 1) // te_rows + N_EXPERTS
    tile_expert = jnp.minimum(
        jnp.sum(jnp.arange(n_tiles)[:, None] >= tile_end[None, :], axis=1), N_EXPERTS - 1)
    big, small = PIECE_ROWS
    ratio = big // small
    layout_start = jnp.cumsum(run, axis=1) - run
    n_big, n_small = nu // ratio, nu % ratio
    max_big, max_small = _max_pieces(_tile_rows(tm))

    def piece_list(per_run, limit, rows, first):
        ends = jnp.cumsum(per_run, axis=1)
        p = jnp.arange(limit)
        e = jnp.minimum(jnp.sum(p[None, :, None] >= ends[:, None, :], axis=-1), N_EXPERTS - 1)
        hit = e[:, :, None] == jnp.arange(N_EXPERTS)
        at = lambda a: jnp.sum(jnp.where(hit, a[:, None, :], 0), axis=-1)
        off = at(first) + (p[None, :] - at(ends - per_run)) * rows
        return [(at(layout_start) + off).reshape(-1).astype(jnp.int32),
                (at(start) + off).reshape(-1).astype(jnp.int32)]

    plan = tuple(piece_list(n_big, max_big, big, jnp.zeros_like(nu))
                 + piece_list(n_small, max_small, small, n_big * big)
                 + [jnp.sum(n_big, axis=1), jnp.sum(n_small, axis=1)])
    n_used = tile_end[-1]
    tile = jnp.arange(n_tiles)
    first = (tile < n_used) & (tile_expert != jnp.concatenate([-jnp.ones(1, jnp.int32),
                                                              tile_expert[:-1]]))
    experts = jnp.arange(N_EXPERTS)
    after = jnp.sum(jnp.where(tile_expert[:, None] == experts, tile_end, 0), axis=1)
    next_expert = jnp.where(after < n_used,
                            jnp.minimum(jnp.sum(after[:, None] >= tile_end[None, :], axis=1),
                                        N_EXPERTS - 1), -1)
    slot = (jnp.cumsum(first) - 1) % 2
    tile_plan = tuple(a.astype(jnp.int32) for a in
                      (tile_expert, n_used[None], first, next_expert, slot))
    return plan, pad_plan, tile_plan, n_tiles * te_rows


def _row_tile(n, want):
    tm = min(n, want)
    while n % tm or tm % SUBLANES:
        tm -= 1
    return tm


def _divisor_at_most(n, want):
    d = min(n, want)
    while n % d:
        d -= 1
    return d


def _alibi_slopes():
    return jnp.exp2(-8.0 * jnp.arange(1, N_HEADS + 1, dtype=F32) / N_HEADS)


def _prompt_bias():
    qi = jnp.arange(WINDOW)[:, None]
    kj = jnp.arange(2 * WINDOW)[None, :]
    dist = WINDOW + qi - kj
    in_window = (dist >= 0) & (dist < WINDOW)
    valid = jnp.stack([in_window & (kj >= WINDOW), in_window])
    score = -_alibi_slopes()[None, :, None, None] * dist.astype(F32)[None, None]
    return jnp.where(valid[:, None], score, NEG_INF)


def _sample_bias(t, bb):
    q_pos = jnp.tile(jnp.arange(t), bb * GQA)[:, None]
    q_seq = jnp.repeat(jnp.arange(bb), GQA * t)[:, None]
    k_pos = jnp.concatenate([jnp.tile(jnp.arange(WINDOW) - WINDOW, bb),
                             jnp.tile(jnp.arange(t), bb)])[None, :]
    k_seq = jnp.concatenate([jnp.repeat(jnp.arange(bb), WINDOW),
                             jnp.repeat(jnp.arange(bb), t)])[None, :]
    dist = q_pos - k_pos
    valid = (dist >= 0) & (dist < WINDOW) & (q_seq == k_seq)
    slopes = jnp.tile(jnp.repeat(_alibi_slopes().reshape(N_KV_HEADS, GQA), t, axis=1), (1, bb))
    score = -slopes[:, :, None] * dist.astype(F32)[None]
    return jnp.where(valid[None], score, NEG_INF)


def _layer_params(norm_mix_g, w_in, conv_w, conv_b, lru_wa, lru_ba, lru_wx, lru_bx, lru_lambda,
                  attn_sinks, w_lru_out, w_attn_out, w_o, norm_moe_g, w_router_group,
                  w_router_expert, moe_w_gate, moe_w_up, moe_w_down):
    per_pack = GATE_PACK // LRU_BLOCK_DIM

    def pack_diag(w):
        w4 = w.reshape(LRU_WIDTH // GATE_PACK, per_pack, LRU_BLOCK_DIM, LRU_BLOCK_DIM)
        eye = jnp.eye(per_pack, dtype=w.dtype)
        return jnp.einsum('pbij,bc->pbicj', w4, eye).reshape(-1, GATE_PACK, GATE_PACK)

    wbd = jnp.concatenate([pack_diag(lru_wa), pack_diag(lru_wx)], axis=-1).astype(BF16)
    w_router = jnp.concatenate(
        [w_router_group, jnp.transpose(w_router_expert, (1, 0, 2)).reshape(D_MODEL, N_EXPERTS)],
        axis=1)
    w_router = jnp.pad(w_router, ((0, 0), (0, ROUTER_COLS - w_router.shape[1])))
    wr_hi = w_router.astype(BF16)
    wr = jnp.concatenate([wr_hi, (w_router - wr_hi.astype(F32)).astype(BF16)], axis=1)
    wg = moe_w_gate.reshape(N_EXPERTS, D_MODEL, D_EXPERT)
    wu = moe_w_up.reshape(N_EXPERTS, D_MODEL, D_EXPERT)
    wd = moe_w_down.reshape(N_EXPERTS, D_EXPERT, D_MODEL)
    row = lambda v: v.reshape(1, -1)
    return dict(
        norm_mix_g=row(norm_mix_g), w_in=w_in.astype(BF16), conv_w=conv_w, conv_b=row(conv_b),
        wbd=wbd, ba=row(lru_ba), bx=row(lru_bx), lam=row(lru_lambda), sinks=attn_sinks,
        wlo=w_lru_out.astype(BF16), wao=w_attn_out.astype(BF16), wo=w_o.astype(BF16),
        norm_moe_g=row(norm_moe_g), wr=wr, wg=wg, wu=wu, wd=wd)


def _prompt_mixers(x, p):
    bsz, t, _ = x.shape
    assert t % WINDOW == 0 and t >= CONV_W - 1
    n = bsz * t
    x2 = x.reshape(n, D_MODEL)
    xl, q, k, v, gl, ga = _inproj(x2, p['norm_mix_g'], p['w_in'], _row_tile(n, 512))
    tc = _row_tile(t, 512)
    yl, h_last = _lru(xl, jnp.zeros((bsz, SUBLANES, LRU_WIDTH), F32),
                      jnp.zeros((bsz, 1, LRU_WIDTH), F32), p['conv_w'], p['conv_b'], p['wbd'],
                      p['ba'], p['bx'], p['lam'], nseq=bsz, seqlen=t, bb=1, tc=tc)
    k3 = k.reshape(bsz, t, KV_COLS)
    v3 = v.reshape(bsz, t, KV_COLS)
    nblk = _divisor_at_most(t // WINDOW, ATTN_BLOCKS)
    ya = _attn_prompt(q.reshape(bsz, t, -1), k3, v3, p['sinks'], _prompt_bias(), nblk)
    new_conv = xl.reshape(bsz, t, LRU_WIDTH)[:, t - (CONV_W - 1):]
    new_k = k3[:, t - WINDOW:].reshape(bsz, WINDOW, N_KV_HEADS, HEAD_DIM)
    new_v = v3[:, t - WINDOW:].reshape(bsz, WINDOW, N_KV_HEADS, HEAD_DIM)
    return ((x2, yl, ya.reshape(n, -1), gl, ga),
            (new_conv, h_last.reshape(bsz, LRU_WIDTH), new_k, new_v))


def _sample_mixers(x, conv_buf, h0, k_buf, v_buf, p):
    bsz, t, _ = x.shape
    assert t % SUBLANES == 0 and CONV_W - 1 <= t <= WINDOW
    n = bsz * t
    x2 = x.reshape(n, D_MODEL)
    xl, q, k, v, gl, ga = _inproj(x2, p['norm_mix_g'], p['w_in'], _row_tile(n, 512))
    bb = _row_tile(bsz, 16)
    prev8 = jnp.pad(conv_buf, ((0, 0), (SUBLANES - (CONV_W - 1), 0), (0, 0)))
    yl, h_last = _lru(xl, prev8, h0.reshape(bsz, 1, LRU_WIDTH), p['conv_w'], p['conv_b'],
                      p['wbd'], p['ba'], p['bx'], p['lam'], nseq=bsz, seqlen=t, bb=bb, tc=t)
    q4 = q.reshape(bsz, t, N_KV_HEADS, GQA, HEAD_DIM).transpose(0, 2, 3, 1, 4).reshape(
        bsz, N_KV_HEADS, GQA * t, HEAD_DIM)
    bb_attn = _row_tile(bsz, SAMPLE_SEQS)
    sink_rows = jnp.tile(jnp.repeat(p['sinks'].reshape(N_KV_HEADS, GQA), t, axis=1),
                         (1, bb_attn))[:, :, None]
    o4, new_k, new_v = _attn_sample(
        q4, k.reshape(bsz, t, KV_COLS), v.reshape(bsz, t, KV_COLS),
        k_buf.reshape(bsz, WINDOW, KV_COLS), v_buf.reshape(bsz, WINDOW, KV_COLS),
        _sample_bias(t, bb_attn), sink_rows, bb_attn)
    ya = o4.reshape(bsz, N_KV_HEADS, GQA, t, HEAD_DIM).transpose(0, 3, 1, 2, 4).reshape(n, -1)
    new_conv = xl.reshape(bsz, t, LRU_WIDTH)[:, t - (CONV_W - 1):]
    shape = (bsz, WINDOW, N_KV_HEADS, HEAD_DIM)
    return ((x2, yl, ya, gl, ga),
            (new_conv, h_last.reshape(bsz, LRU_WIDTH), new_k.reshape(shape), new_v.reshape(shape)))


def _merge_and_moe(acts_a, acts_b, p, norm_final_g):
    n_a, n_b = acts_a[0].shape[0], acts_b[0].shape[0]
    n = n_a + n_b
    tm = _row_tile(math.gcd(n_a, n_b), MERGE_ROWS)
    x1, xn, rec, cnt = _merge(acts_a, acts_b, p, tm)
    plan, pad_plan, tile_plan, n_rows = _moe_plan(cnt, n, tm, EXPERT_ROWS)
    xs = _dispatch(plan, pad_plan, tile_plan[1], xn, rec, n_rows, tm, EXPERT_ROWS)
    ys = _experts(tile_plan, xs, p['wg'], p['wu'], p['wd'], EXPERT_ROWS)
    return _combine(plan, ys, x1, rec, norm_final_g.reshape(1, -1), n_a, tm)


def kernel(x_prompt, x_sample, state_conv, state_lru_h, cache_win_k, cache_win_v, norm_mix_g, w_in, conv_w, conv_b, lru_wa, lru_ba, lru_wx, lru_bx, lru_lambda, attn_sinks, w_lru_out, w_attn_out, w_o, norm_moe_g, w_router_group, w_router_expert, moe_w_gate, moe_w_up, moe_w_down, norm_final_g):
    depth = w_in.shape[0]
    assert depth == 1, "the final norm is fused into the single layer's MoE kernel"
    p = _layer_params(norm_mix_g[0], w_in[0], conv_w[0], conv_b[0], lru_wa[0], lru_ba[0],
                      lru_wx[0], lru_bx[0], lru_lambda[0], attn_sinks[0], w_lru_out[0],
                      w_attn_out[0], w_o[0], norm_moe_g[0], w_router_group[0],
                      w_router_expert[0], moe_w_gate[0], moe_w_up[0], moe_w_down[0])
    acts_p, (c1, h1, k1, v1) = _prompt_mixers(x_prompt, p)
    acts_s, (c2, h2, k2, v2) = _sample_mixers(x_sample, state_conv[0], state_lru_h[0],
                                              cache_win_k[0], cache_win_v[0], p)
    yp, ys = _merge_and_moe(acts_p, acts_s, p, norm_final_g)
    return (yp.reshape(x_prompt.shape), ys.reshape(x_sample.shape), c1[None], h1[None], k1[None],
            v1[None], c2[None], h2[None], k2[None], v2[None])
```

```python
import functools
import math

import jax
import jax.numpy as jnp
from jax import lax
from jax.experimental import pallas as pl
from jax.experimental.pallas import tpu as pltpu

D_MODEL = 1024
LRU_WIDTH = D_MODEL
LRU_BLOCKS = 16
LRU_BLOCK_DIM = LRU_WIDTH // LRU_BLOCKS
LRU_C = 8.0
CONV_W = 4
N_HEADS = 16
N_KV_HEADS = 4
GQA = N_HEADS // N_KV_HEADS
HEAD_DIM = D_MODEL // N_HEADS
KV_COLS = N_KV_HEADS * HEAD_DIM
WINDOW = 128
N_GROUPS = 4
EXPERTS_PER_GROUP = 8
N_EXPERTS = N_GROUPS * EXPERTS_PER_GROUP
D_EXPERT = D_MODEL // 4
RMS_EPS = 1e-6
NEG_INF = -1e30
IN_COLS = LRU_WIDTH + (N_HEADS + 2 * N_KV_HEADS) * HEAD_DIM + 2 * D_MODEL

SUBLANES = 8
LANES = 128
ROUTER_COLS = LANES
GATE_PACK = 256

MERGE_ROWS = 512
EXPERT_ROWS = 512
RUN_ALIGN = 16
PIECE_ROWS = (32, RUN_ALIGN)
ATTN_BLOCKS = 4
DISPATCH_SLOTS = 3
SAMPLE_SEQS = 8

F32 = jnp.float32
BF16 = jnp.bfloat16


def _sigmoid(z):
    return 1.0 / (1.0 + jnp.exp(-z))


def _one_minus_exp2(x, ex):
    return -jnp.tanh(x) * (ex * ex + 1.0)


def _rms_scale(x):
    return x * lax.rsqrt(jnp.mean(x * x, axis=-1, keepdims=True) + RMS_EPS)


def _inproj_kernel(x_ref, g_ref, w_ref, xl_ref, q_ref, k_ref, v_ref, gl_ref, ga_ref):
    xb = (_rms_scale(x_ref[...]) * g_ref[...]).astype(BF16)

    def proj(c0, c1):
        return jnp.dot(xb, w_ref[:, c0:c1], preferred_element_type=F32)

    c0 = LRU_WIDTH
    c1 = c0 + N_HEADS * HEAD_DIM
    c2 = c1 + KV_COLS
    c3 = c2 + KV_COLS
    c4 = c3 + D_MODEL
    xl_ref[...] = proj(0, c0)
    q_ref[...] = (proj(c0, c1) * (HEAD_DIM ** -0.5)).astype(BF16)
    k_ref[...] = proj(c1, c2)
    v_ref[...] = proj(c2, c3)
    gl_ref[...] = _sigmoid(proj(c3, c4)).astype(BF16)
    ga_ref[...] = _sigmoid(proj(c4, IN_COLS)).astype(BF16)


def _inproj(x2, g, w_bf, tm):
    n = x2.shape[0]
    row = lambda i: (i, 0)
    fixed = lambda i: (0, 0)
    outs = [
        jax.ShapeDtypeStruct((n, LRU_WIDTH), F32),
        jax.ShapeDtypeStruct((n, N_HEADS * HEAD_DIM), BF16),
        jax.ShapeDtypeStruct((n, KV_COLS), F32),
        jax.ShapeDtypeStruct((n, KV_COLS), F32),
        jax.ShapeDtypeStruct((n, D_MODEL), BF16),
        jax.ShapeDtypeStruct((n, D_MODEL), BF16),
    ]
    return pl.pallas_call(
        _inproj_kernel,
        out_shape=outs,
        grid=(n // tm,),
        in_specs=[
            pl.BlockSpec((tm, D_MODEL), row),
            pl.BlockSpec((1, D_MODEL), fixed),
            pl.BlockSpec((D_MODEL, IN_COLS), fixed),
        ],
        out_specs=[pl.BlockSpec((tm, o.shape[1]), row) for o in outs],
        compiler_params=pltpu.CompilerParams(dimension_semantics=("parallel",)),
        name="inproj",
    )(x2, g, w_bf)


def _lru_kernel(xl_ref, prev_ref, h0_ref, cw_ref, cb_ref, wbd_ref, ba_ref, bx_ref, lam_ref,
                y_ref, hl_ref, xs_ref, xc_ref, a_ref, b_ref, hc_ref, *, bb, tc):
    t = pl.program_id(1)

    @pl.when(t == 0)
    def _():
        xs_ref[:, 0:SUBLANES, :] = prev_ref[...]
        hc_ref[...] = h0_ref[...]

    @pl.when(t > 0)
    def _():
        xs_ref[:, 0:SUBLANES, :] = xs_ref[:, tc:tc + SUBLANES, :]

    for s in range(bb):
        xs_ref[s, SUBLANES:SUBLANES + tc, :] = xl_ref[s * tc:(s + 1) * tc, :]
    cw = cw_ref[...]
    for s in range(bb):
        acc = cb_ref[...] + xs_ref[s, SUBLANES:SUBLANES + tc, :] * cw[CONV_W - 1:CONV_W, :]
        for j in range(CONV_W - 1):
            off = SUBLANES - (CONV_W - 1) + j
            acc = acc + xs_ref[s, off:off + tc, :] * cw[j:j + 1, :]
        xc_ref[s * tc:(s + 1) * tc, :] = acc

    lam = lam_ref[...]
    softplus_neg_lam = jnp.maximum(-lam, 0.0) + jnp.log1p(jnp.exp(-jnp.abs(lam)))
    for g in range(LRU_WIDTH // GATE_PACK):
        cols = slice(g * GATE_PACK, (g + 1) * GATE_PACK)
        xc = xc_ref[:, cols]
        gates = jnp.dot(xc.astype(BF16), wbd_ref[g], preferred_element_type=F32)
        r = _sigmoid(gates[:, :GATE_PACK] + ba_ref[:, cols])
        ig = _sigmoid(gates[:, GATE_PACK:] + bx_ref[:, cols])
        log_a = (-LRU_C) * r * softplus_neg_lam[:, cols]
        a = jnp.exp(log_a)
        a_ref[:, cols] = a
        b_ref[:, cols] = jnp.sqrt(_one_minus_exp2(log_a, a)) * (ig * xc)

    row = lax.broadcasted_iota(jnp.int32, (SUBLANES, LRU_WIDTH), 0)
    for s in range(bb):
        def group(gi, h8, s=s):
            r0 = pl.multiple_of(s * tc + gi * SUBLANES, SUBLANES)
            a = a_ref[pl.ds(r0, SUBLANES), :]
            b = b_ref[pl.ds(r0, SUBLANES), :]
            for d in (1, 2, 4):
                a_up = jnp.where(row >= d, pltpu.roll(a, d, 0), 1.0)
                b_up = jnp.where(row >= d, pltpu.roll(b, d, 0), 0.0)
                b = a * b_up + b
                a = a * a_up
            h = a * h8 + b
            b_ref[pl.ds(r0, SUBLANES), :] = h
            return jnp.broadcast_to(h[SUBLANES - 1:SUBLANES, :], (SUBLANES, LRU_WIDTH))

        h8 = lax.fori_loop(0, tc // SUBLANES, group,
                           jnp.broadcast_to(hc_ref[s], (SUBLANES, LRU_WIDTH)))
        hc_ref[s] = h8[0:1, :]

    y_ref[...] = b_ref[...].astype(y_ref.dtype)

    @pl.when(t == pl.num_programs(1) - 1)
    def _():
        hl_ref[...] = hc_ref[...]


def _lru(xl2, prev8, h0, cw, cb, wbd, ba, bx, lam, *, nseq, seqlen, bb, tc):
    nt = seqlen // tc
    assert bb == 1 or nt == 1
    rows = bb * tc
    fixed2 = lambda b, t: (0, 0)
    fixed3 = lambda b, t: (0, 0, 0)
    return pl.pallas_call(
        functools.partial(_lru_kernel, bb=bb, tc=tc),
        out_shape=[
            jax.ShapeDtypeStruct((nseq * seqlen, LRU_WIDTH), BF16),
            jax.ShapeDtypeStruct((nseq, 1, LRU_WIDTH), F32),
        ],
        grid=(nseq // bb, nt),
        in_specs=[
            pl.BlockSpec((rows, LRU_WIDTH), lambda b, t: (b * nt + t, 0)),
            pl.BlockSpec((bb, SUBLANES, LRU_WIDTH), lambda b, t: (b, 0, 0)),
            pl.BlockSpec((bb, 1, LRU_WIDTH), lambda b, t: (b, 0, 0)),
            pl.BlockSpec((CONV_W, LRU_WIDTH), fixed2),
            pl.BlockSpec((1, LRU_WIDTH), fixed2),
            pl.BlockSpec((LRU_WIDTH // GATE_PACK, GATE_PACK, 2 * GATE_PACK), fixed3),
            pl.BlockSpec((1, LRU_WIDTH), fixed2),
            pl.BlockSpec((1, LRU_WIDTH), fixed2),
            pl.BlockSpec((1, LRU_WIDTH), fixed2),
        ],
        out_specs=[
            pl.BlockSpec((rows, LRU_WIDTH), lambda b, t: (b * nt + t, 0)),
            pl.BlockSpec((bb, 1, LRU_WIDTH), lambda b, t: (b, 0, 0)),
        ],
        scratch_shapes=[
            pltpu.VMEM((bb, tc + SUBLANES, LRU_WIDTH), F32),
            pltpu.VMEM((rows, LRU_WIDTH), F32),
            pltpu.VMEM((rows, LRU_WIDTH), F32),
            pltpu.VMEM((rows, LRU_WIDTH), F32),
            pltpu.VMEM((bb, 1, LRU_WIDTH), F32),
        ],
        compiler_params=pltpu.CompilerParams(dimension_semantics=("parallel", "arbitrary")),
        name="conv_rglru",
    )(xl2, prev8, h0, cw, cb, wbd, ba, bx, lam)


def _softmax_sink_pv(s, sink, v_bf):
    m = jnp.maximum(jnp.max(s, axis=-1, keepdims=True), sink)
    p = jnp.exp(s - m)
    denom = jnp.sum(p, axis=-1, keepdims=True) + jnp.exp(sink - m)
    o = jnp.dot(p.astype(BF16), v_bf, preferred_element_type=F32)
    return o / denom


def _attn_prompt_kernel(sink_ref, q_ref, kc_ref, kp_ref, vc_ref, vp_ref, bias_ref, o_ref,
                        k_ref, v_ref, *, nblk):
    k_ref[0:WINDOW, :] = kp_ref[0].astype(BF16)
    k_ref[WINDOW:, :] = kc_ref[0].astype(BF16)
    v_ref[0:WINDOW, :] = vp_ref[0].astype(BF16)
    v_ref[WINDOW:, :] = vc_ref[0].astype(BF16)

    def block(j, carry):
        bsel = jnp.where((pl.program_id(1) == 0) & (j == 0), 0, 1)
        r0 = pl.multiple_of(j * WINDOW, WINDOW)
        q = q_ref[0, pl.ds(r0, WINDOW), :]
        kcat = k_ref[pl.ds(r0, 2 * WINDOW), :]
        vcat = v_ref[pl.ds(r0, 2 * WINDOW), :]
        outs = []
        for h in range(N_HEADS):
            kv = slice((h // GQA) * HEAD_DIM, (h // GQA + 1) * HEAD_DIM)
            s = lax.dot_general(q[:, h * HEAD_DIM:(h + 1) * HEAD_DIM], kcat[:, kv],
                                (((1,), (1,)), ((), ())), preferred_element_type=F32)
            s = s + bias_ref[bsel, h]
            outs.append(_softmax_sink_pv(s, sink_ref[h], vcat[:, kv]))
        o_ref[0, pl.ds(r0, WINDOW), :] = jnp.concatenate(outs, axis=-1).astype(o_ref.dtype)
        return carry

    lax.fori_loop(0, nblk, block, 0)


def _attn_prompt(q3, k3, v3, sinks, bias, nblk):
    bsz, t, _ = q3.shape
    rows = nblk * WINDOW
    cur = lambda b, i: (b, i, 0)
    prev = lambda b, i: (b, jnp.maximum(i * nblk - 1, 0), 0)
    return pl.pallas_call(
        functools.partial(_attn_prompt_kernel, nblk=nblk),
        out_shape=jax.ShapeDtypeStruct((bsz, t, N_HEADS * HEAD_DIM), BF16),
        grid=(bsz, t // rows),
        in_specs=[
            pl.BlockSpec(memory_space=pltpu.SMEM),
            pl.BlockSpec((1, rows, N_HEADS * HEAD_DIM), cur),
            pl.BlockSpec((1, rows, KV_COLS), cur),
            pl.BlockSpec((1, WINDOW, KV_COLS), prev),
            pl.BlockSpec((1, rows, KV_COLS), cur),
            pl.BlockSpec((1, WINDOW, KV_COLS), prev),
            pl.BlockSpec((2, N_HEADS, WINDOW, 2 * WINDOW), lambda b, i: (0, 0, 0, 0)),
        ],
        out_specs=pl.BlockSpec((1, rows, N_HEADS * HEAD_DIM), cur),
        scratch_shapes=[pltpu.VMEM((rows + WINDOW, KV_COLS), BF16),
                        pltpu.VMEM((rows + WINDOW, KV_COLS), BF16)],
        compiler_params=pltpu.CompilerParams(dimension_semantics=("parallel", "arbitrary")),
        name="swa_prompt",
    )(sinks, q3, k3, k3, v3, v3, bias)


def _attn_sample_kernel(q_ref, kn_ref, vn_ref, ck_ref, cv_ref, bias_ref, sink_ref,
                        o_ref, nk_ref, nv_ref, *, bb, t):
    keep = WINDOW - t
    for s in range(bb):
        nk_ref[s, 0:keep, :] = ck_ref[s, t:WINDOW, :]
        nk_ref[s, keep:WINDOW, :] = kn_ref[s]
        nv_ref[s, 0:keep, :] = cv_ref[s, t:WINDOW, :]
        nv_ref[s, keep:WINDOW, :] = vn_ref[s]
    kall = jnp.concatenate([ck_ref[...].reshape(bb * WINDOW, KV_COLS),
                            kn_ref[...].reshape(bb * t, KV_COLS)], axis=0).astype(BF16)
    vall = jnp.concatenate([cv_ref[...].reshape(bb * WINDOW, KV_COLS),
                            vn_ref[...].reshape(bb * t, KV_COLS)], axis=0).astype(BF16)
    nq = bb * t
    q = q_ref[...]
    head = lambda h: slice(h * HEAD_DIM, (h + 1) * HEAD_DIM)
    for kh in range(N_KV_HEADS):
        kv = slice(kh * HEAD_DIM, (kh + 1) * HEAD_DIM)
        qh = jnp.concatenate([q[:, head(kh * GQA + g)] for g in range(GQA)], axis=0)
        sc = lax.dot_general(qh, kall[:, kv], (((1,), (1,)), ((), ())),
                             preferred_element_type=F32)
        o = _softmax_sink_pv(sc + bias_ref[kh], sink_ref[kh], vall[:, kv])
        for g in range(GQA):
            o_ref[:, head(kh * GQA + g)] = o[g * nq:(g + 1) * nq, :].astype(o_ref.dtype)


def _attn_sample(q2, kn3, vn3, ck3, cv3, bias, sink_rows, bb):
    nseq, t = kn3.shape[0], kn3.shape[1]
    rows = GQA * bb * t
    b3 = lambda i: (i, 0, 0)
    b2 = lambda i: (i, 0)
    return pl.pallas_call(
        functools.partial(_attn_sample_kernel, bb=bb, t=t),
        out_shape=[
            jax.ShapeDtypeStruct(q2.shape, BF16),
            jax.ShapeDtypeStruct((nseq, WINDOW, KV_COLS), F32),
            jax.ShapeDtypeStruct((nseq, WINDOW, KV_COLS), F32),
        ],
        grid=(nseq // bb,),
        in_specs=[
            pl.BlockSpec((bb * t, q2.shape[1]), b2),
            pl.BlockSpec((bb, t, KV_COLS), b3),
            pl.BlockSpec((bb, t, KV_COLS), b3),
            pl.BlockSpec((bb, WINDOW, KV_COLS), b3),
            pl.BlockSpec((bb, WINDOW, KV_COLS), b3),
            pl.BlockSpec((N_KV_HEADS, rows, bb * (WINDOW + t)), lambda i: (0, 0, 0)),
            pl.BlockSpec((N_KV_HEADS, rows, 1), lambda i: (0, 0, 0)),
        ],
        out_specs=[
            pl.BlockSpec((bb * t, q2.shape[1]), b2),
            pl.BlockSpec((bb, WINDOW, KV_COLS), b3),
            pl.BlockSpec((bb, WINDOW, KV_COLS), b3),
        ],
        compiler_params=pltpu.CompilerParams(dimension_semantics=("parallel",)),
        name="swa_sample",
    )(q2, kn3, vn3, ck3, cv3, bias, sink_rows)


R_E1, R_E2, R_D1, R_D2, R_W1, R_W2 = range(6)


def _route(logits):
    rows = logits.shape[0]
    lane = lax.broadcasted_iota(jnp.int32, logits.shape, 1)
    big = jnp.int32(ROUTER_COLS)
    gl = jnp.where(lane < N_GROUPS, logits, -jnp.inf)
    gmax = jnp.max(gl, axis=-1, keepdims=True)
    p_grp = 1.0 / jnp.sum(jnp.exp(gl - gmax), axis=-1, keepdims=True)
    g_idx = jnp.min(jnp.where(gl == gmax, lane, big), axis=-1, keepdims=True)
    e_lo = N_GROUPS + EXPERTS_PER_GROUP * g_idx
    el = jnp.where((lane >= e_lo) & (lane < e_lo + EXPERTS_PER_GROUP), logits, -jnp.inf)
    m1 = jnp.max(el, axis=-1, keepdims=True)
    i1 = jnp.min(jnp.where(el == m1, lane, big), axis=-1, keepdims=True)
    el2 = jnp.where(lane == i1, -jnp.inf, el)
    m2 = jnp.max(el2, axis=-1, keepdims=True)
    i2 = jnp.min(jnp.where(el2 == m2, lane, big), axis=-1, keepdims=True)
    e21 = jnp.exp(m2 - m1)
    w1 = p_grp / (1.0 + e21)
    w2 = e21 * w1
    hit1 = lane == i1
    hit2 = lane == i2
    onehot = jnp.where(hit1 | hit2, 1.0, 0.0).astype(BF16)
    r_i = lax.broadcasted_iota(jnp.int32, (rows, rows), 0)
    c_i = lax.broadcasted_iota(jnp.int32, (rows, rows), 1)
    tril = jnp.where(c_i <= r_i, 1.0, 0.0).astype(BF16)
    upto = jnp.dot(tril, onehot, preferred_element_type=F32)
    count = upto[rows - 1:rows, :]
    run_u = jnp.floor((count + (RUN_ALIGN - 1)) * (1.0 / RUN_ALIGN))
    k_i = lax.broadcasted_iota(jnp.int32, (ROUTER_COLS, ROUTER_COLS), 0)
    j_i = lax.broadcasted_iota(jnp.int32, (ROUTER_COLS, ROUTER_COLS), 1)
    before = jnp.where(k_i < j_i, 1.0, 0.0).astype(BF16)
    start_u = jnp.dot(jnp.broadcast_to(run_u, (SUBLANES, ROUTER_COLS)).astype(BF16), before,
                     preferred_element_type=F32)[0:1, :]
    dest = RUN_ALIGN * start_u + upto - 1.0
    d1 = jnp.sum(jnp.where(hit1, dest, 0.0), axis=-1, keepdims=True)
    d2 = jnp.sum(jnp.where(hit2, dest, 0.0), axis=-1, keepdims=True)
    rec = jnp.zeros(logits.shape, F32)
    for idx, val in ((R_E1, (i1 - N_GROUPS).astype(F32)), (R_E2, (i2 - N_GROUPS).astype(F32)),
                     (R_D1, d1), (R_D2, d2), (R_W1, w1), (R_W2, w2)):
        rec = jnp.where(lane == idx, val, rec)
    return rec, count


def _merge_kernel(*refs, a_tiles):
    acts_a, acts_b = refs[0:5], refs[5:10]
    wlo_ref, wao_ref, wo_ref, g_ref, wr_ref, x1_ref, xn_ref, rec_ref, cnt_ref = refs[10:]

    def tile(x_ref, yl_ref, ya_ref, gl_ref, ga_ref):
        a = jnp.dot(yl_ref[...], wlo_ref[...], preferred_element_type=F32)
        b = jnp.dot(ya_ref[...], wao_ref[...], preferred_element_type=F32)
        merged = gl_ref[...].astype(F32) * a + ga_ref[...].astype(F32) * b
        x1 = x_ref[...] + jnp.dot(merged.astype(BF16), wo_ref[...], preferred_element_type=F32)
        x1_ref[...] = x1
        xn = _rms_scale(x1) * g_ref[...]
        xh = xn.astype(BF16)
        xn_ref[...] = xh
        xlo = (xn - xh.astype(F32)).astype(BF16)
        both = jnp.dot(xh, wr_ref[...], preferred_element_type=F32)
        logits = (both[:, :ROUTER_COLS] + both[:, ROUTER_COLS:]
                  + jnp.dot(xlo, wr_ref[:, :ROUTER_COLS], preferred_element_type=F32))
        rec, count = _route(logits)
        rec_ref[...] = rec
        cnt_ref[0] = jnp.broadcast_to(count, (SUBLANES, ROUTER_COLS))

    pl.when(pl.program_id(0) < a_tiles)(lambda: tile(*acts_a))
    pl.when(pl.program_id(0) >= a_tiles)(lambda: tile(*acts_b))


def _merge(acts_a, acts_b, p, tm):
    n_a, n_b = acts_a[0].shape[0], acts_b[0].shape[0]
    n = n_a + n_b
    a_tiles = n_a // tm
    row = lambda i: (i, 0)
    fixed = lambda i: (0, 0)
    act_a = pl.BlockSpec((tm, D_MODEL), lambda i: (jnp.minimum(i, a_tiles - 1), 0))
    act_b = pl.BlockSpec((tm, D_MODEL), lambda i: (jnp.maximum(i - a_tiles, 0), 0))
    wsq = pl.BlockSpec((D_MODEL, D_MODEL), fixed)
    wr = pl.BlockSpec((D_MODEL, 2 * ROUTER_COLS), fixed)
    return pl.pallas_call(
        functools.partial(_merge_kernel, a_tiles=a_tiles),
        out_shape=[
            jax.ShapeDtypeStruct((n, D_MODEL), F32),
            jax.ShapeDtypeStruct((n, D_MODEL), BF16),
            jax.ShapeDtypeStruct((n, ROUTER_COLS), F32),
            jax.ShapeDtypeStruct((n // tm, SUBLANES, ROUTER_COLS), F32),
        ],
        grid=(n // tm,),
        in_specs=[act_a] * 5 + [act_b] * 5 + [wsq, wsq, wsq, pl.BlockSpec((1, D_MODEL), fixed), wr],
        out_specs=[pl.BlockSpec((tm, D_MODEL), row),
                   pl.BlockSpec((tm, D_MODEL), row),
                   pl.BlockSpec((tm, ROUTER_COLS), row),
                   pl.BlockSpec((1, SUBLANES, ROUTER_COLS), lambda i: (i, 0, 0))],
        compiler_params=pltpu.CompilerParams(dimension_semantics=("parallel",)),
        name="merge_router",
    )(*acts_a, *acts_b, p['wlo'], p['wao'], p['wo'], p['norm_moe_g'], p['wr'])


def _tile_rows(tm):
    return -(-(2 * tm + N_EXPERTS * (RUN_ALIGN - 1)) // LANES) * LANES


def _max_pieces(rv):
    big, small = PIECE_ROWS
    return rv // big, N_EXPERTS * (big // small - 1)


def _for_each_piece(i, plan_refs, rv, fn):
    big_src, big_dst, small_src, small_dst, nbig_ref, nsmall_ref = plan_refs
    max_big, max_small = _max_pieces(rv)
    for src_ref, dst_ref, n_ref, stride, rows in ((big_src, big_dst, nbig_ref, max_big, PIECE_ROWS[0]),
                                                  (small_src, small_dst, nsmall_ref, max_small,
                                                   PIECE_ROWS[1])):
        def piece(k, carry, src_ref=src_ref, dst_ref=dst_ref, stride=stride, rows=rows):
            fn(pl.multiple_of(src_ref[i * stride + k], RUN_ALIGN),
               pl.multiple_of(dst_ref[i * stride + k], RUN_ALIGN), rows)
            return carry

        lax.fori_loop(0, n_ref[i], piece, 0)


def _drain(copy_of, n_big, n_small):
    big, small = PIECE_ROWS
    lax.fori_loop(0, n_big, lambda k, c: (copy_of(big).wait(), c)[1], 0)
    lax.fori_loop(0, n_small, lambda k, c: (copy_of(small).wait(), c)[1], 0)


def _dispatch_kernel(*refs):
    plan_refs = refs[:6]
    nbig_ref, nsmall_ref = plan_refs[4:]
    (pad_start_ref, pad_nu_ref, pad_tot_ref, used_ref, xn_ref, rec_ref, xs_ref, buf_ref, zero_ref,
     sem, zsem) = refs[6:]
    i = pl.program_id(0)
    nslot = buf_ref.shape[0]
    slot = i % nslot
    rv = buf_ref.shape[1]
    rec_t = jnp.transpose(rec_ref[...])
    row = lax.broadcasted_iota(jnp.int32, (rv, rec_t.shape[1]), 0).astype(F32)
    place = jnp.where((row == rec_t[R_D1:R_D1 + 1, :]) | (row == rec_t[R_D2:R_D2 + 1, :]), 1.0, 0.0)
    buf_ref[slot] = jnp.dot(place.astype(BF16), xn_ref[...],
                            preferred_element_type=F32).astype(buf_ref.dtype)

    def piece_copy(s, src_row, dst_row, rows):
        return pltpu.make_async_copy(buf_ref.at[s, pl.ds(src_row, rows)],
                                     xs_ref.at[pl.ds(dst_row, rows)], sem.at[s])

    def drain(s, n_big, n_small):
        _drain(lambda rows: piece_copy(s, 0, 0, rows), n_big, n_small)

    @pl.when(i >= nslot - 1)
    def _():
        drain((i + 1) % nslot, nbig_ref[i - (nslot - 1)], nsmall_ref[i - (nslot - 1)])

    _for_each_piece(i, plan_refs, rv,
                    lambda src_row, dst_row, rows: piece_copy(slot, src_row, dst_row, rows).start())

    @pl.when(i == pl.num_programs(0) - 1)
    def _():
        zero_ref[...] = jnp.zeros_like(zero_ref)

        def pad_expert(e, carry):
            def piece(k, c):
                dst_row = pl.multiple_of(pad_start_ref[e] + k * RUN_ALIGN, RUN_ALIGN)
                pltpu.make_async_copy(zero_ref.at[pl.ds(0, RUN_ALIGN)],
                                      xs_ref.at[pl.ds(dst_row, RUN_ALIGN)], sem.at[slot]).start()
                return c

            lax.fori_loop(0, pad_nu_ref[e], piece, 0)
            return carry

        lax.fori_loop(0, N_EXPERTS, pad_expert, 0)

        te_rows = zero_ref.shape[0]

        def tail_copy(t):
            return pltpu.make_async_copy(
                zero_ref, xs_ref.at[pl.ds(pl.multiple_of(t * te_rows, te_rows), te_rows)], zsem)

        n_tiles = xs_ref.shape[0] // te_rows
        lax.fori_loop(used_ref[0], n_tiles, lambda t, c: (tail_copy(t).start(), c)[1], 0)
        for back in range(1, nslot - 1):
            @pl.when(i >= back)
            def _(back=back):
                drain((i - back) % nslot, nbig_ref[i - back], nsmall_ref[i - back])
        drain(slot, nbig_ref[i], nsmall_ref[i] + pad_tot_ref[0])
        lax.fori_loop(used_ref[0], n_tiles, lambda t, c: (tail_copy(t).wait(), c)[1], 0)


def _dispatch(plan, pad_plan, n_used, xn, rec, n_rows, tm, te_rows):
    n = xn.shape[0]
    rv = _tile_rows(tm)
    row = lambda i, *_: (i, 0)
    return pl.pallas_call(
        _dispatch_kernel,
        out_shape=jax.ShapeDtypeStruct((n_rows, D_MODEL), BF16),
        grid_spec=pltpu.PrefetchScalarGridSpec(
            num_scalar_prefetch=10,
            grid=(n // tm,),
            in_specs=[pl.BlockSpec((tm, D_MODEL), row), pl.BlockSpec((tm, ROUTER_COLS), row)],
            out_specs=pl.BlockSpec(memory_space=pl.ANY),
            scratch_shapes=[pltpu.VMEM((DISPATCH_SLOTS, rv, D_MODEL), BF16),
                            pltpu.VMEM((te_rows, D_MODEL), BF16),
                            pltpu.SemaphoreType.DMA((DISPATCH_SLOTS,)), pltpu.SemaphoreType.DMA],
        ),
        compiler_params=pltpu.CompilerParams(dimension_semantics=("arbitrary",)),
        name="moe_dispatch",
    )(*plan, *pad_plan, n_used, xn, rec)


def _experts_kernel(te_ref, nu_ref, first_ref, next_ref, slot_ref, xs_ref, wg_hbm, wu_hbm, wd_hbm,
                    ys_ref, wg_stage, wu_stage, wd_stage, wg_bf, wu_bf, wd_bf, sem):
    t = pl.program_id(0)

    def weight_copies(e, s):
        return (pltpu.make_async_copy(wg_hbm.at[e], wg_stage.at[s], sem.at[s, 0]),
                pltpu.make_async_copy(wu_hbm.at[e], wu_stage.at[s], sem.at[s, 1]),
                pltpu.make_async_copy(wd_hbm.at[e], wd_stage.at[s], sem.at[s, 2]))

    @pl.when(t < nu_ref[0])
    def _():
        @pl.when(first_ref[t] == 1)
        def _():
            s = slot_ref[t]

            @pl.when(t == 0)
            def _():
                for copy in weight_copies(te_ref[0], s):
                    copy.start()

            for copy in weight_copies(te_ref[t], s):
                copy.wait()
            wg_bf[...] = wg_stage[s].astype(BF16)
            wu_bf[...] = wu_stage[s].astype(BF16)
            wd_bf[...] = wd_stage[s].astype(BF16)

            @pl.when(next_ref[t] >= 0)
            def _():
                for copy in weight_copies(next_ref[t], 1 - s):
                    copy.start()

        x = xs_ref[...]
        gate = jnp.dot(x, wg_bf[...], preferred_element_type=F32)
        up = jnp.dot(x, wu_bf[...], preferred_element_type=F32)
        h = gate * _sigmoid(gate) * up
        ys_ref[...] = jnp.dot(h.astype(BF16), wd_bf[...],
                              preferred_element_type=F32).astype(ys_ref.dtype)

    @pl.when(t >= nu_ref[0])
    def _():
        ys_ref[...] = jnp.zeros_like(ys_ref)


def _experts(tile_plan, xs, wg, wu, wd, te_rows):
    n_rows, c = xs.shape
    used = lambda t, te, nu, *_: (jnp.minimum(t, nu[0] - 1), 0)
    hbm = pl.BlockSpec(memory_space=pl.ANY)
    return pl.pallas_call(
        _experts_kernel,
        out_shape=jax.ShapeDtypeStruct((n_rows, c), xs.dtype),
        grid_spec=pltpu.PrefetchScalarGridSpec(
            num_scalar_prefetch=5,
            grid=(n_rows // te_rows,),
            in_specs=[pl.BlockSpec((te_rows, c), used), hbm, hbm, hbm],
            out_specs=pl.BlockSpec((te_rows, c), lambda t, *_: (t, 0)),
            scratch_shapes=[
                pltpu.VMEM((2,) + wg.shape[1:], wg.dtype),
                pltpu.VMEM((2,) + wu.shape[1:], wu.dtype),
                pltpu.VMEM((2,) + wd.shape[1:], wd.dtype),
                pltpu.VMEM(wg.shape[1:], BF16),
                pltpu.VMEM(wu.shape[1:], BF16),
                pltpu.VMEM(wd.shape[1:], BF16),
                pltpu.SemaphoreType.DMA((2, 3)),
            ],
        ),
        compiler_params=pltpu.CompilerParams(dimension_semantics=("arbitrary",)),
        name="moe_experts",
    )(*tile_plan, xs, wg, wu, wd)


def _combine_kernel(*refs, a_tiles):
    plan_refs = refs[:6]
    nbig_ref, nsmall_ref = plan_refs[4:]
    ys_ref, x1_ref, rec_ref, g_ref, ya_ref, yb_ref, buf_ref, sem = refs[6:]
    i = pl.program_id(0)
    last = pl.num_programs(0) - 1
    slot = i % 2
    rv = buf_ref.shape[1]

    def piece_copy(s, src_row, dst_row, rows):
        return pltpu.make_async_copy(ys_ref.at[pl.ds(src_row, rows)],
                                     buf_ref.at[s, pl.ds(dst_row, rows)], sem.at[s])

    def fetch(tile, s):
        _for_each_piece(tile, plan_refs, rv,
                        lambda row, src_row, rows: piece_copy(s, src_row, row, rows).start())

    @pl.when(i == 0)
    def _():
        buf_ref[...] = jnp.zeros_like(buf_ref)
        fetch(0, 0)

    @pl.when(i < last)
    def _():
        fetch(i + 1, 1 - slot)

    _drain(lambda rows: piece_copy(slot, 0, 0, rows), nbig_ref[i], nsmall_ref[i])
    rec = rec_ref[...]
    col = lax.broadcasted_iota(jnp.int32, (rec.shape[0], rv), 1).astype(F32)
    weigh = (jnp.where(col == rec[:, R_D1:R_D1 + 1], rec[:, R_W1:R_W1 + 1], 0.0)
             + jnp.where(col == rec[:, R_D2:R_D2 + 1], rec[:, R_W2:R_W2 + 1], 0.0))
    moe = jnp.dot(weigh.astype(BF16), buf_ref[slot], preferred_element_type=F32)
    y = _rms_scale(x1_ref[...] + moe) * g_ref[...]

    @pl.when(i < a_tiles)
    def _():
        ya_ref[...] = y

    @pl.when(i >= a_tiles)
    def _():
        yb_ref[...] = y


def _combine(plan, ys, x1, rec, g, n_a, tm):
    n = x1.shape[0]
    a_tiles = n_a // tm
    rv = _tile_rows(tm)
    row = lambda i, *_: (i, 0)
    return pl.pallas_call(
        functools.partial(_combine_kernel, a_tiles=a_tiles),
        out_shape=[jax.ShapeDtypeStruct((n_a, D_MODEL), F32),
                   jax.ShapeDtypeStruct((n - n_a, D_MODEL), F32)],
        grid_spec=pltpu.PrefetchScalarGridSpec(
            num_scalar_prefetch=6,
            grid=(n // tm,),
            in_specs=[
                pl.BlockSpec(memory_space=pl.ANY),
                pl.BlockSpec((tm, D_MODEL), row),
                pl.BlockSpec((tm, ROUTER_COLS), row),
                pl.BlockSpec((1, D_MODEL), lambda i, *_: (0, 0)),
            ],
            out_specs=[
                pl.BlockSpec((tm, D_MODEL), lambda i, *_: (jnp.minimum(i, a_tiles - 1), 0)),
                pl.BlockSpec((tm, D_MODEL), lambda i, *_: (jnp.maximum(i - a_tiles, 0), 0))],
            scratch_shapes=[pltpu.VMEM((2, rv, D_MODEL), BF16), pltpu.SemaphoreType.DMA((2,))],
        ),
        compiler_params=pltpu.CompilerParams(dimension_semantics=("arbitrary",)),
        name="moe_combine",
    )(*plan, ys, x1, rec, g)


def _moe_plan(counts, n, tm, te_rows):
    cnt = counts[:, 0, N_GROUPS:N_GROUPS + N_EXPERTS].astype(jnp.int32)
    nu = (cnt + RUN_ALIGN - 1) // RUN_ALIGN
    run = nu * RUN_ALIGN
    tiles = (jnp.sum(run, axis=0) + te_rows - 1) // te_rows
    tile_end = jnp.cumsum(tiles)
    first_row = (tile_end - tiles) * te_rows
    start = first_row + jnp.cumsum(run, axis=0) - run
    pad_nu = (tiles * te_rows - jnp.sum(run, axis=0)) // RUN_ALIGN
    pad_plan = (first_row + jnp.sum(run, axis=0), pad_nu, jnp.sum(pad_nu, keepdims=True))
    n_tiles = (2 * n + (n // tm) * N_EXPERTS * (RUN_ALIGN - 1) + te_rows - 1) // te_rows + N_EXPERTS
    tile_expert = jnp.minimum(
        jnp.sum(jnp.arange(n_tiles)[:, None] >= tile_end[None, :], axis=1), N_EXPERTS - 1)
    big, small = PIECE_ROWS
    ratio = big // small
    layout_start = jnp.cumsum(run, axis=1) - run
    n_big, n_small = nu // ratio, nu % ratio
    max_big, max_small = _max_pieces(_tile_rows(tm))

    def piece_list(per_run, limit, rows, first):
        ends = jnp.cumsum(per_run, axis=1)
        p = jnp.arange(limit)
        e = jnp.minimum(jnp.sum(p[None, :, None] >= ends[:, None, :], axis=-1), N_EXPERTS - 1)
        hit = e[:, :, None] == jnp.arange(N_EXPERTS)
        at = lambda a: jnp.sum(jnp.where(hit, a[:, None, :], 0), axis=-1)
        off = at(first) + (p[None, :] - at(ends - per_run)) * rows
        return [(at(layout_start) + off).reshape(-1).astype(jnp.int32),
                (at(start) + off).reshape(-1).astype(jnp.int32)]

    plan = tuple(piece_list(n_big, max_big, big, jnp.zeros_like(nu))
                 + piece_list(n_small, max_small, small, n_big * big)
                 + [jnp.sum(n_big, axis=1), jnp.sum(n_small, axis=1)])
    n_used = tile_end[-1]
    tile = jnp.arange(n_tiles)
    first = (tile < n_used) & (tile_expert != jnp.concatenate([-jnp.ones(1, jnp.int32),
                                                              tile_expert[:-1]]))
    experts = jnp.arange(N_EXPERTS)
    after = jnp.sum(jnp.where(tile_expert[:, None] == experts, tile_end, 0), axis=1)
    next_expert = jnp.where(after < n_used,
                            jnp.minimum(jnp.sum(after[:, None] >= tile_end[None, :], axis=1),
                                        N_EXPERTS - 1), -1)
    slot = (jnp.cumsum(first) - 1) % 2
    tile_plan = tuple(a.astype(jnp.int32) for a in
                      (tile_expert, n_used[None], first, next_expert, slot))
    return plan, pad_plan, tile_plan, n_tiles * te_rows


def _row_tile(n, want):
    tm = min(n, want)
    while n % tm or tm % SUBLANES:
        tm -= 1
    return tm


def _divisor_at_most(n, want):
    d = min(n, want)
    while n % d:
        d -= 1
    return d


def _alibi_slopes():
    return jnp.exp2(-8.0 * jnp.arange(1, N_HEADS + 1, dtype=F32) / N_HEADS)


def _prompt_bias():
    qi = jnp.arange(WINDOW)[:, None]
    kj = jnp.arange(2 * WINDOW)[None, :]
    dist = WINDOW + qi - kj
    in_window = (dist >= 0) & (dist < WINDOW)
    valid = jnp.stack([in_window & (kj >= WINDOW), in_window])
    score = -_alibi_slopes()[None, :, None, None] * dist.astype(F32)[None, None]
    return jnp.where(valid[:, None], score, NEG_INF)


def _sample_bias(t, bb):
    q_pos = jnp.tile(jnp.arange(t), bb * GQA)[:, None]
    q_seq = jnp.tile(jnp.repeat(jnp.arange(bb), t), GQA)[:, None]
    k_pos = jnp.concatenate([jnp.tile(jnp.arange(WINDOW) - WINDOW, bb),
                             jnp.tile(jnp.arange(t), bb)])[None, :]
    k_seq = jnp.concatenate([jnp.repeat(jnp.arange(bb), WINDOW),
                             jnp.repeat(jnp.arange(bb), t)])[None, :]
    dist = q_pos - k_pos
    valid = (dist >= 0) & (dist < WINDOW) & (q_seq == k_seq)
    slopes = jnp.repeat(_alibi_slopes().reshape(N_KV_HEADS, GQA), bb * t, axis=1)
    score = -slopes[:, :, None] * dist.astype(F32)[None]
    return jnp.where(valid[None], score, NEG_INF)


def _layer_params(norm_mix_g, w_in, conv_w, conv_b, lru_wa, lru_ba, lru_wx, lru_bx, lru_lambda,
                  attn_sinks, w_lru_out, w_attn_out, w_o, norm_moe_g, w_router_group,
                  w_router_expert, moe_w_gate, moe_w_up, moe_w_down):
    per_pack = GATE_PACK // LRU_BLOCK_DIM

    def pack_diag(w):
        w4 = w.reshape(LRU_WIDTH // GATE_PACK, per_pack, LRU_BLOCK_DIM, LRU_BLOCK_DIM)
        eye = jnp.eye(per_pack, dtype=w.dtype)
        return jnp.einsum('pbij,bc->pbicj', w4, eye).reshape(-1, GATE_PACK, GATE_PACK)

    wbd = jnp.concatenate([pack_diag(lru_wa), pack_diag(lru_wx)], axis=-1).astype(BF16)
    w_router = jnp.concatenate(
        [w_router_group, jnp.transpose(w_router_expert, (1, 0, 2)).reshape(D_MODEL, N_EXPERTS)],
        axis=1)
    w_router = jnp.pad(w_router, ((0, 0), (0, ROUTER_COLS - w_router.shape[1])))
    wr_hi = w_router.astype(BF16)
    wr = jnp.concatenate([wr_hi, (w_router - wr_hi.astype(F32)).astype(BF16)], axis=1)
    wg = moe_w_gate.reshape(N_EXPERTS, D_MODEL, D_EXPERT)
    wu = moe_w_up.reshape(N_EXPERTS, D_MODEL, D_EXPERT)
    wd = moe_w_down.reshape(N_EXPERTS, D_EXPERT, D_MODEL)
    row = lambda v: v.reshape(1, -1)
    return dict(
        norm_mix_g=row(norm_mix_g), w_in=w_in.astype(BF16), conv_w=conv_w, conv_b=row(conv_b),
        wbd=wbd, ba=row(lru_ba), bx=row(lru_bx), lam=row(lru_lambda), sinks=attn_sinks,
        wlo=w_lru_out.astype(BF16), wao=w_attn_out.astype(BF16), wo=w_o.astype(BF16),
        norm_moe_g=row(norm_moe_g), wr=wr, wg=wg, wu=wu, wd=wd)


def _prompt_mixers(x, p):
    bsz, t, _ = x.shape
    assert t % WINDOW == 0 and t >= CONV_W - 1
    n = bsz * t
    x2 = x.reshape(n, D_MODEL)
    xl, q, k, v, gl, ga = _inproj(x2, p['norm_mix_g'], p['w_in'], _row_tile(n, 512))
    tc = _row_tile(t, 512)
    yl, h_last = _lru(xl, jnp.zeros((bsz, SUBLANES, LRU_WIDTH), F32),
                      jnp.zeros((bsz, 1, LRU_WIDTH), F32), p['conv_w'], p['conv_b'], p['wbd'],
                      p['ba'], p['bx'], p['lam'], nseq=bsz, seqlen=t, bb=1, tc=tc)
    k3 = k.reshape(bsz, t, KV_COLS)
    v3 = v.reshape(bsz, t, KV_COLS)
    nblk = _divisor_at_most(t // WINDOW, ATTN_BLOCKS)
    ya = _attn_prompt(q.reshape(bsz, t, -1), k3, v3, p['sinks'], _prompt_bias(), nblk)
    new_conv = xl.reshape(bsz, t, LRU_WIDTH)[:, t - (CONV_W - 1):]
    new_k = k3[:, t - WINDOW:].reshape(bsz, WINDOW, N_KV_HEADS, HEAD_DIM)
    new_v = v3[:, t - WINDOW:].reshape(bsz, WINDOW, N_KV_HEADS, HEAD_DIM)
    return ((x2, yl, ya.reshape(n, -1), gl, ga),
            (new_conv, h_last.reshape(bsz, LRU_WIDTH), new_k, new_v))


def _sample_mixers(x, conv_buf, h0, k_buf, v_buf, p):
    bsz, t, _ = x.shape
    assert t % SUBLANES == 0 and CONV_W - 1 <= t <= WINDOW
    n = bsz * t
    x2 = x.reshape(n, D_MODEL)
    xl, q, k, v, gl, ga = _inproj(x2, p['norm_mix_g'], p['w_in'], _row_tile(n, 512))
    bb = _row_tile(bsz, 16)
    prev8 = jnp.pad(conv_buf, ((0, 0), (SUBLANES - (CONV_W - 1), 0), (0, 0)))
    yl, h_last = _lru(xl, prev8, h0.reshape(bsz, 1, LRU_WIDTH), p['conv_w'], p['conv_b'],
                      p['wbd'], p['ba'], p['bx'], p['lam'], nseq=bsz, seqlen=t, bb=bb, tc=t)
    bb_attn = _row_tile(bsz, SAMPLE_SEQS)
    sink_rows = jnp.repeat(p['sinks'].reshape(N_KV_HEADS, GQA), bb_attn * t, axis=1)[:, :, None]
    ya, new_k, new_v = _attn_sample(
        q, k.reshape(bsz, t, KV_COLS), v.reshape(bsz, t, KV_COLS),
        k_buf.reshape(bsz, WINDOW, KV_COLS), v_buf.reshape(bsz, WINDOW, KV_COLS),
        _sample_bias(t, bb_attn), sink_rows, bb_attn)
    new_conv = xl.reshape(bsz, t, LRU_WIDTH)[:, t - (CONV_W - 1):]
    shape = (bsz, WINDOW, N_KV_HEADS, HEAD_DIM)
    return ((x2, yl, ya, gl, ga),
            (new_conv, h_last.reshape(bsz, LRU_WIDTH), new_k.reshape(shape), new_v.reshape(shape)))


def _merge_and_moe(acts_a, acts_b, p, norm_final_g):
    n_a, n_b = acts_a[0].shape[0], acts_b[0].shape[0]
    n = n_a + n_b
    tm = _row_tile(math.gcd(n_a, n_b), MERGE_ROWS)
    x1, xn, rec, cnt = _merge(acts_a, acts_b, p, tm)
    plan, pad_plan, tile_plan, n_rows = _moe_plan(cnt, n, tm, EXPERT_ROWS)
    xs = _dispatch(plan, pad_plan, tile_plan[1], xn, rec, n_rows, tm, EXPERT_ROWS)
    ys = _experts(tile_plan, xs, p['wg'], p['wu'], p['wd'], EXPERT_ROWS)
    return _combine(plan, ys, x1, rec, norm_final_g.reshape(1, -1), n_a, tm)


def kernel(x_prompt, x_sample, state_conv, state_lru_h, cache_win_k, cache_win_v, norm_mix_g, w_in, conv_w, conv_b, lru_wa, lru_ba, lru_wx, lru_bx, lru_lambda, attn_sinks, w_lru_out, w_attn_out, w_o, norm_moe_g, w_router_group, w_router_expert, moe_w_gate, moe_w_up, moe_w_down, norm_final_g):
    depth = w_in.shape[0]
    assert depth == 1, "the final norm is fused into the single layer's MoE kernel"
    p = _layer_params(norm_mix_g[0], w_in[0], conv_w[0], conv_b[0], lru_wa[0], lru_ba[0],
                      lru_wx[0], lru_bx[0], lru_lambda[0], attn_sinks[0], w_lru_out[0],
                      w_attn_out[0], w_o[0], norm_moe_g[0], w_router_group[0],
                      w_router_expert[0], moe_w_gate[0], moe_w_up[0], moe_w_down[0])
    acts_p, (c1, h1, k1, v1) = _prompt_mixers(x_prompt, p)
    acts_s, (c2, h2, k2, v2) = _sample_mixers(x_sample, state_conv[0], state_lru_h[0],
                                              cache_win_k[0], cache_win_v[0], p)
    yp, ys = _merge_and_moe(acts_p, acts_s, p, norm_final_g)
    return (yp.reshape(x_prompt.shape), ys.reshape(x_sample.shape), c1[None], h1[None], k1[None],
            v1[None], c2[None], h2[None], k2[None], v2[None])
```
